```python
import math
import jax, jax.numpy as jnp
from jax import lax
import numpy as np

D_MODEL = 2048
BATCH = 1
SEQ = 16384
DEPTH = 4
DEC_BATCH = 16
DEC_SEQ = 2048
PAST_LEN = 128

GRID_W = 64
EPS = 1e-6
NA_HEADS = 4
NA_HEAD_DIM = 128
NA_WIDTH = NA_HEADS * NA_HEAD_DIM
NA_WIN_H = 8
NA_WIN_W = 16
SSD_HEADS = 16
SSD_HEAD_DIM = 64
SSD_WIDTH = SSD_HEADS * SSD_HEAD_DIM
SSD_STATE = 128
SSD_GROUPS = 2
SSD_CONV = 5
SSD_CHUNK = 128
SSD_XBC = SSD_WIDTH + 2 * SSD_GROUPS * SSD_STATE
GQA_HEADS = 4
GQA_KV_HEADS = 2
GQA_HEAD_DIM = 128
GQA_WIDTH = GQA_HEADS * GQA_HEAD_DIM
GQA_KV_WIDTH = GQA_KV_HEADS * GQA_HEAD_DIM
ATTN_BLOCK = 128
ROPE_THETA = 10000.0
D_FF = 5632
FFN_CONV = 3
N_BRANCH = 3
MIX_WIDTH = NA_WIDTH + SSD_WIDTH + GQA_WIDTH
IN_SIZES = (3 * NA_WIDTH, GQA_WIDTH, GQA_KV_WIDTH, GQA_KV_WIDTH, SSD_WIDTH, SSD_XBC, 2 * SSD_HEADS, N_BRANCH * D_MODEL)
IN_COLS = 3 * NA_WIDTH + GQA_WIDTH + 2 * GQA_KV_WIDTH + SSD_WIDTH + SSD_XBC + 2 * SSD_HEADS + N_BRANCH * D_MODEL

kernel_name = "hybrid_na_ssd_gqa_encoder"


def _split(t, sizes):
    idx, acc = [], 0
    for s in sizes[:-1]:
        acc += s
        idx.append(acc)
    return jnp.split(t, idx, axis=-1)


def rmsnorm(x, w):
    xf = x.astype(jnp.float32)
    y = xf * lax.rsqrt(jnp.mean(xf * xf, axis=-1, keepdims=True) + EPS)
    return (y * w.astype(jnp.float32)).astype(x.dtype)


def dwconv_centred(x, w, b):
    K = w.shape[0]
    L = x.shape[1]
    pad = K // 2
    xp = jnp.pad(x, ((0, 0), (pad, pad), (0, 0)))
    out = b
    for i in range(K):
        out = out + xp[:, i:i + L] * w[i]
    return out


def neighbourhood_attention(q, k, v, rpb, q_norm, k_norm):
    B, L = q.shape[:2]
    rows = L // GRID_W
    kh = min(NA_WIN_H, rows)
    q = rmsnorm(q, q_norm)
    k = rmsnorm(k, k_norm)
    grid = lambda t: t.reshape(B, rows, GRID_W, NA_HEADS, NA_HEAD_DIM).transpose(0, 3, 1, 2, 4)
    qg, kg, vg = grid(q), grid(k), grid(v)
    r = jnp.arange(rows)
    row_start = jnp.clip(r - kh // 2, 0, rows - kh)
    key_rows = row_start[:, None] + jnp.arange(kh)[None, :]
    kr = kg[:, :, key_rows]
    vr = vg[:, :, key_rows]
    scale = NA_HEAD_DIM ** -0.5
    s = jnp.einsum('bhrqd,bhrkcd->bhrqkc', qg, kr).astype(jnp.float32) * scale
    c = jnp.arange(GRID_W)
    col_start = jnp.clip(c - NA_WIN_W // 2, 0, GRID_W - NA_WIN_W)
    in_win = (c[None, :] >= col_start[:, None]) & (c[None, :] < col_start[:, None] + NA_WIN_W)
    dr = key_rows - r[:, None]
    dc = jnp.clip(c[None, :] - c[:, None], -(NA_WIN_W - 1), NA_WIN_W - 1)
    bias = rpb[:, (dr + NA_WIN_H - 1)[:, None, :, None], (dc + NA_WIN_W - 1)[None, :, None, :]]
    s = s + bias.astype(jnp.float32)[None]
    s = jnp.where(in_win[None, None, None, :, None, :], s, -jnp.inf)
    p = jax.nn.softmax(s.reshape(B, NA_HEADS, rows, GRID_W, kh * GRID_W), axis=-1)
    p = p.reshape(B, NA_HEADS, rows, GRID_W, kh, GRID_W).astype(v.dtype)
    o = jnp.einsum('bhrqkc,bhrkcd->bhrqd', p, vr)
    return o.transpose(0, 2, 3, 1, 4).reshape(B, L, NA_WIDTH)


def axial_rope(x):
    B, L, H, hd = x.shape
    t = jnp.arange(L)
    pos = jnp.stack([t // GRID_W, t % GRID_W], axis=-1).astype(jnp.float32)
    n_freq = hd // 4
    inv_freq = ROPE_THETA ** (-jnp.arange(n_freq, dtype=jnp.float32) / n_freq)
    ang = pos[:, :, None] * inv_freq
    cos = jnp.cos(ang)[:, None]
    sin = jnp.sin(ang)[:, None]
    xa = x.astype(jnp.float32).reshape(B, L, H, 2, 2, n_freq)
    x1, x2 = xa[..., 0, :], xa[..., 1, :]
    out = jnp.stack([x1 * cos - x2 * sin, x1 * sin + x2 * cos], axis=-2)
    return out.reshape(B, L, H, hd).astype(x.dtype)


def gqa_attention(q, k, v, q_norm, k_norm):
    B, L = q.shape[:2]
    q = axial_rope(rmsnorm(q, q_norm))
    k = axial_rope(rmsnorm(k, k_norm))
    rep = GQA_HEADS // GQA_KV_HEADS
    nb = L // ATTN_BLOCK
    qb = q.reshape(B, nb, ATTN_BLOCK, GQA_KV_HEADS, rep, GQA_HEAD_DIM).transpose(1, 0, 3, 4, 2, 5)
    scale = GQA_HEAD_DIM ** -0.5

    def block(qi):
        s = jnp.einsum('bgrqd,bkgd->bgrqk', qi, k).astype(jnp.float32) * scale
        p = jax.nn.softmax(s, axis=-1).astype(v.dtype)
        return jnp.einsum('bgrqk,bkgd->bgrqd', p, v)

    o = lax.map(block, qb)
    return o.transpose(1, 0, 4, 2, 3, 5).reshape(B, L, GQA_WIDTH)


def ssd_scan(x, dt, a, b, c):
    out_dtype = x.dtype
    Bsz, L = x.shape[:2]
    nc = L // SSD_CHUNK
    R = SSD_HEADS // SSD_GROUPS
    x = x.astype(jnp.float32)
    dt = dt.astype(jnp.float32)
    xd = (x * dt[..., None]).reshape(Bsz, nc, SSD_CHUNK, SSD_GROUPS, R, SSD_HEAD_DIM)
    da = (dt * a).reshape(Bsz, nc, SSD_CHUNK, SSD_GROUPS, R).transpose(0, 3, 4, 1, 2)
    bc = b.astype(jnp.float32).reshape(Bsz, nc, SSD_CHUNK, SSD_GROUPS, SSD_STATE)
    cc = c.astype(jnp.float32).reshape(Bsz, nc, SSD_CHUNK, SSD_GROUPS, SSD_STATE)
    cum = jnp.cumsum(da, axis=-1)
    li = jnp.arange(SSD_CHUNK)
    lower = li[:, None] >= li[None, :]
    seg = cum[..., :, None] - cum[..., None, :]
    lmat = jnp.exp(jnp.where(lower, seg, -jnp.inf))
    cb = jnp.einsum('bclgn,bcsgn->bgcls', cc, bc)
    y_diag = jnp.einsum('bgcls,bgrcls,bcsgrp->bclgrp', cb, lmat, xd)
    decay_states = jnp.exp(cum[..., -1:] - cum)
    states = jnp.einsum('bclgn,bgrcl,bclgrp->bcgrpn', bc, decay_states, xd)
    chunk_decay = jnp.exp(cum[..., -1])

    def step(h, inp):
        s_c, d_c = inp
        return h * d_c[..., None, None] + s_c, h

    h0 = jnp.zeros((Bsz, SSD_GROUPS, R, SSD_HEAD_DIM, SSD_STATE), jnp.float32)
    _, prev = lax.scan(step, h0, (states.transpose(1, 0, 2, 3, 4, 5), chunk_decay.transpose(3, 0, 1, 2)))
    prev = prev.transpose(1, 0, 2, 3, 4, 5)
    y_off = jnp.einsum('bclgn,bcgrpn,bgrcl->bclgrp', cc, prev, jnp.exp(cum))
    return (y_diag + y_off).reshape(Bsz, L, SSD_HEADS, SSD_HEAD_DIM).astype(out_dtype)


def ssd_mixer(z, xbc, dt_raw, conv_w, conv_b, dt_bias, a_log, d_skip, norm_w):
    B, L = z.shape[:2]
    xbc = jax.nn.silu(dwconv_centred(xbc, conv_w, conv_b))
    xs, bm, cm = _split(xbc, (SSD_WIDTH, SSD_GROUPS * SSD_STATE, SSD_GROUPS * SSD_STATE))
    xs = xs.reshape(B, L, SSD_HEADS, SSD_HEAD_DIM)
    bm = bm.reshape(B, L, SSD_GROUPS, SSD_STATE)
    cm = cm.reshape(B, L, SSD_GROUPS, SSD_STATE)
    dt = jax.nn.softplus(dt_raw.astype(jnp.float32).reshape(B, L, 2, SSD_HEADS) + dt_bias.astype(jnp.float32))
    a = -jnp.exp(a_log.astype(jnp.float32))
    flip = lambda t: jnp.flip(t, axis=1)
    y_fwd = ssd_scan(xs, dt[:, :, 0], a[0], bm, cm)
    y_bwd = flip(ssd_scan(flip(xs), flip(dt[:, :, 1]), a[1], flip(bm), flip(cm)))
    y = y_fwd + y_bwd + xs * d_skip[:, None]
    y = y.reshape(B, L, SSD_WIDTH) * jax.nn.silu(z)
    y = rmsnorm(y.reshape(B, L, SSD_GROUPS, SSD_WIDTH // SSD_GROUPS), norm_w.reshape(SSD_GROUPS, SSD_WIDTH // SSD_GROUPS))
    return y.reshape(B, L, SSD_WIDTH)


def layer(x, norm1_w, w_in, na_q_norm, na_k_norm, na_rpb, ssd_conv_w, ssd_conv_b, ssd_dt_bias, ssd_a_log,
          ssd_d, ssd_norm_w, gqa_q_norm, gqa_k_norm, w_branch, w_out, norm2_w, ffn_w_up, ffn_conv_w,
          ffn_conv_b, ffn_w_down):
    B, L, _ = x.shape
    h = rmsnorm(x, norm1_w)
    proj = h @ w_in
    na_qkv, gq, gk, gv, z, xbc, dt_raw, gates = _split(proj, IN_SIZES)
    na_qkv = na_qkv.reshape(B, L, 3, NA_HEADS, NA_HEAD_DIM)
    o_na = neighbourhood_attention(na_qkv[:, :, 0], na_qkv[:, :, 1], na_qkv[:, :, 2], na_rpb, na_q_norm, na_k_norm)
    o_ssd = ssd_mixer(z, xbc, dt_raw, ssd_conv_w, ssd_conv_b, ssd_dt_bias, ssd_a_log, ssd_d, ssd_norm_w)
    o_gqa = gqa_attention(gq.reshape(B, L, GQA_HEADS, GQA_HEAD_DIM), gk.reshape(B, L, GQA_KV_HEADS, GQA_HEAD_DIM),
                          gv.reshape(B, L, GQA_KV_HEADS, GQA_HEAD_DIM), gqa_q_norm, gqa_k_norm)
    g = jax.nn.sigmoid(gates.astype(jnp.float32)).astype(x.dtype).reshape(B, L, N_BRANCH, D_MODEL)
    p_na = o_na @ w_branch[:NA_WIDTH]
    p_ssd = o_ssd @ w_branch[NA_WIDTH:NA_WIDTH + SSD_WIDTH]
    p_gqa = o_gqa @ w_branch[NA_WIDTH + SSD_WIDTH:]
    merged = g[:, :, 0] * p_na + g[:, :, 1] * p_ssd + g[:, :, 2] * p_gqa
    x = x + merged @ w_out
    h2 = rmsnorm(x, norm2_w)
    u_gate, u_val = _split(h2 @ ffn_w_up, (D_FF, D_FF))
    u_gate = dwconv_centred(u_gate, ffn_conv_w, ffn_conv_b)
    return x + (jax.nn.silu(u_gate) * u_val) @ ffn_w_down


def trunk(x, params):
    for i in range(DEPTH):
        x = layer(x, *[p[i] for p in params])
    return x


def setup_inputs(seed: int = 0) -> dict:
    key = jax.random.key(seed)
    ks = jax.random.split(key, 24)
    nrm = lambda k, shape, s: jax.random.normal(k, shape, jnp.float32) * s
    u_dt = jax.random.uniform(ks[9], (DEPTH, 2, SSD_HEADS), jnp.float32)
    dt0 = jnp.exp(u_dt * (math.log(0.1) - math.log(0.001)) + math.log(0.001))
    return {
        "x_prompt": nrm(ks[0], (BATCH, SEQ, D_MODEL), 1.0),
        "x_sample": nrm(ks[1], (DEC_BATCH, DEC_SEQ, D_MODEL), 1.0),
        "norm1_w": 1.0 + nrm(ks[2], (DEPTH, D_MODEL), 0.02),
        "w_in": nrm(ks[3], (DEPTH, D_MODEL, IN_COLS), D_MODEL ** -0.5),
        "na_q_norm": 1.0 + nrm(ks[4], (DEPTH, NA_HEAD_DIM), 0.02),
        "na_k_norm": 1.0 + nrm(ks[5], (DEPTH, NA_HEAD_DIM), 0.02),
        "na_rpb": nrm(ks[6], (DEPTH, NA_HEADS, 2 * NA_WIN_H - 1, 2 * NA_WIN_W - 1), 0.1),
        "ssd_conv_w": nrm(ks[7], (DEPTH, SSD_CONV, SSD_XBC), SSD_CONV ** -0.5),
        "ssd_conv_b": nrm(ks[8], (DEPTH, SSD_XBC), 0.02),
        "ssd_dt_bias": dt0 + jnp.log(-jnp.expm1(-dt0)),
        "ssd_a_log": jnp.log(jax.random.uniform(ks[10], (DEPTH, 2, SSD_HEADS), jnp.float32, 1.0, 16.0)),
        "ssd_d": 1.0 + nrm(ks[11], (DEPTH, SSD_HEADS), 0.1),
        "ssd_norm_w": 1.0 + nrm(ks[12], (DEPTH, SSD_WIDTH), 0.02),
        "gqa_q_norm": 1.0 + nrm(ks[13], (DEPTH, GQA_HEAD_DIM), 0.02),
        "gqa_k_norm": 1.0 + nrm(ks[14], (DEPTH, GQA_HEAD_DIM), 0.02),
        "w_branch": nrm(ks[15], (DEPTH, MIX_WIDTH, D_MODEL), NA_WIDTH ** -0.5),
        "w_out": nrm(ks[16], (DEPTH, D_MODEL, D_MODEL), D_MODEL ** -0.5),
        "norm2_w": 1.0 + nrm(ks[17], (DEPTH, D_MODEL), 0.02),
        "ffn_w_up": nrm(ks[18], (DEPTH, D_MODEL, 2 * D_FF), D_MODEL ** -0.5),
        "ffn_conv_w": nrm(ks[19], (DEPTH, FFN_CONV, D_FF), FFN_CONV ** -0.5),
        "ffn_conv_b": nrm(ks[20], (DEPTH, D_FF), 0.02),
        "ffn_w_down": nrm(ks[21], (DEPTH, D_FF, D_MODEL), D_FF ** -0.5),
    }


def reference(x_prompt, x_sample, norm1_w, w_in, na_q_norm, na_k_norm, na_rpb, ssd_conv_w, ssd_conv_b,
              ssd_dt_bias, ssd_a_log, ssd_d, ssd_norm_w, gqa_q_norm, gqa_k_norm, w_branch, w_out, norm2_w,
              ffn_w_up, ffn_conv_w, ffn_conv_b, ffn_w_down):
    params = (norm1_w, w_in, na_q_norm, na_k_norm, na_rpb, ssd_conv_w, ssd_conv_b, ssd_dt_bias, ssd_a_log,
              ssd_d, ssd_norm_w, gqa_q_norm, gqa_k_norm, w_branch, w_out, norm2_w, ffn_w_up, ffn_conv_w,
              ffn_conv_b, ffn_w_down)
    y_prompt = trunk(x_prompt, params)
    y_sample = trunk(x_sample, params)
    return (y_prompt, y_sample)
```

```python
import functools
import math

import numpy as np
import jax
import jax.numpy as jnp
from jax import lax
from jax.experimental import pallas as pl
from jax.experimental.pallas import tpu as pltpu

F32 = jnp.float32
BF16 = jnp.bfloat16

D_MODEL = 2048
GRID_W = 64
EPS = 1e-6
NA_HEADS = 4
HEAD_DIM = 128
NA_WIDTH = NA_HEADS * HEAD_DIM
NA_WIN_H = 8
NA_WIN_W = 16
SSD_HEADS = 16
SSD_HEAD_DIM = 64
SSD_WIDTH = SSD_HEADS * SSD_HEAD_DIM
SSD_STATE = 128
SSD_GROUPS = 2
SSD_CONV = 5
SSD_CHUNK = 128
SSD_BC = 2 * SSD_GROUPS * SSD_STATE
SSD_XBC = SSD_WIDTH + SSD_BC
GQA_HEADS = 4
GQA_KV_HEADS = 2
GQA_WIDTH = GQA_HEADS * HEAD_DIM
GQA_KV_WIDTH = GQA_KV_HEADS * HEAD_DIM
ROPE_THETA = 10000.0
D_FF = 5632
N_BRANCH = 3
GATES = N_BRANCH * D_MODEL
DT_PAD = 128

OFF_GATES = 0
OFF_XBC = OFF_GATES + GATES
OFF_NA = OFF_XBC + SSD_XBC
OFF_Z = OFF_NA + 3 * NA_WIDTH
OFF_GQ = OFF_Z + SSD_WIDTH
OFF_GK = OFF_GQ + GQA_WIDTH
OFF_GV = OFF_GK + GQA_KV_WIDTH
PROJ_COLS = OFF_GV + GQA_KV_WIDTH

NEG = -1e30
HALO = 8
VMEM_LIMIT = 56 * 2**20


def _cp(*sem):
    return pltpu.CompilerParams(dimension_semantics=sem, vmem_limit_bytes=VMEM_LIMIT)


def _sigmoid(x):
    return 1.0 / (1.0 + jnp.exp(-x))


def _silu(x):
    return x * _sigmoid(x)


def _norm_mm_kernel(*refs, tm, with_dt):
    if with_dt:
        x_ref, nw_ref, w_ref, wdt_ref, o_ref, odt_ref, h_scr = refs
    else:
        x_ref, nw_ref, w_ref, o_ref, h_scr = refs
    rc = 128

    @pl.when(pl.program_id(1) == 0)
    def _():
        def body(r, c):
            rows = pl.ds(pl.multiple_of(r * rc, rc), rc)
            x = x_ref[rows, :]
            ms = jnp.mean(x * x, axis=-1, keepdims=True)
            h_scr[rows, :] = (x * lax.rsqrt(ms + EPS) * nw_ref[...]).astype(BF16)
            return c

        lax.fori_loop(0, tm // rc, body, 0)
        if with_dt:
            odt_ref[...] = jnp.dot(h_scr[...], wdt_ref[...], preferred_element_type=F32)

    o_ref[...] = jnp.dot(h_scr[...], w_ref[...], preferred_element_type=F32).astype(o_ref.dtype)


def _norm_matmul(x, nw, w, wdt=None, *, tm, tn):
    m, d = x.shape
    n = w.shape[1]
    with_dt = wdt is not None
    in_specs = [
        pl.BlockSpec((tm, d), lambda i, j: (i, 0)),
        pl.BlockSpec((1, d), lambda i, j: (0, 0)),
        pl.BlockSpec((d, tn), lambda i, j: (0, j)),
    ]
    out_shape = [jax.ShapeDtypeStruct((m, n), BF16)]
    out_specs = [pl.BlockSpec((tm, tn), lambda i, j: (i, j))]
    args = [x, nw, w]
    if with_dt:
        in_specs.append(pl.BlockSpec((d, DT_PAD), lambda i, j: (0, 0)))
        out_shape.append(jax.ShapeDtypeStruct((m, DT_PAD), F32))
        out_specs.append(pl.BlockSpec((tm, DT_PAD), lambda i, j: (i, 0)))
        args.append(wdt)
    res = pl.pallas_call(
        functools.partial(_norm_mm_kernel, tm=tm, with_dt=with_dt),
        grid=(m // tm, n // tn),
        in_specs=in_specs,
        out_specs=out_specs,
        out_shape=out_shape,
        scratch_shapes=[pltpu.VMEM((tm, d), BF16)],
        compiler_params=_cp("parallel", "arbitrary"),
        name="norm_matmul_dt" if with_dt else "norm_matmul",
    )(*args)
    return res if with_dt else res[0]


NA_QROWS = 8
NA_KROWS = 16
NA_TQ = NA_QROWS * GRID_W
NA_TK = NA_KROWS * GRID_W
NA_KPIECE = 256
NA_NPIECE = NA_TK // NA_KPIECE


def _na_bias_tables(rpb, rows):
    nb = rows // NA_QROWS
    c = np.arange(GRID_W)
    col_start = np.clip(c - NA_WIN_W // 2, 0, GRID_W - NA_WIN_W)
    in_win = (c[None, :] >= col_start[:, None]) & (c[None, :] < col_start[:, None] + NA_WIN_W)
    dc = np.clip(c[None, :] - c[:, None], -(NA_WIN_W - 1), NA_WIN_W - 1) + NA_WIN_W - 1
    tables = []
    for b in (0, min(1, nb - 1), nb - 1):
        r = NA_QROWS * b + np.arange(NA_QROWS)
        ks = int(np.clip(NA_QROWS * b - NA_WIN_H // 2, 0, rows - NA_KROWS))
        krow = ks + np.arange(NA_KROWS)
        rs = np.clip(r - NA_WIN_H // 2, 0, rows - NA_WIN_H)
        valid = (krow[None, :] >= rs[:, None]) & (krow[None, :] < rs[:, None] + NA_WIN_H)
        dr = np.clip(krow[None, :] - r[:, None], -(NA_WIN_H - 1), NA_WIN_H - 1) + NA_WIN_H - 1
        bias = rpb[:, dr[:, None, :, None], dc[None, :, None, :]]
        mask = valid[:, None, :, None] & in_win[None, :, None, :]
        bias = jnp.where(jnp.asarray(mask)[None], bias.astype(F32), NEG)
        tables.append(bias.reshape(NA_HEADS, NA_TQ, NA_TK))
    return jnp.stack(tables)


def _head_rmsnorm(x, w):
    ms = jnp.mean(x * x, axis=-1, keepdims=True)
    return x * lax.rsqrt(ms + EPS) * w


def _na_kernel(q_ref, k0, k1, k2, k3, v0, v1, v2, v3, bias_ref, qw_ref, kw_ref, o_ref, kn_scr, *, nb):
    i = pl.program_id(2)
    kind = jnp.where(i == 0, 0, jnp.where(i == nb - 1, 2, 1))
    scale = HEAD_DIM ** -0.5
    k_refs = (k0, k1, k2, k3)
    v_refs = (v0, v1, v2, v3)
    for t in range(NA_NPIECE):
        kn_scr[t] = _head_rmsnorm(k_refs[t][...].astype(F32), kw_ref[...]).astype(BF16)
    sub = 128
    for qs in range(NA_TQ // sub):
        rows = pl.ds(qs * sub, sub)
        qn = (_head_rmsnorm(q_ref[rows, :].astype(F32), qw_ref[...]) * scale).astype(BF16)
        s = []
        for t in range(NA_NPIECE):
            st = lax.dot_general(qn, kn_scr[t], (((1,), (1,)), ((), ())), preferred_element_type=F32)
            s.append(st + bias_ref[kind, rows, pl.ds(t * NA_KPIECE, NA_KPIECE)])
        m = s[0].max(axis=-1, keepdims=True)
        for t in range(1, NA_NPIECE):
            m = jnp.maximum(m, s[t].max(axis=-1, keepdims=True))
        l = jnp.zeros_like(m)
        acc = jnp.zeros((sub, HEAD_DIM), F32)
        for t in range(NA_NPIECE):
            p = jnp.exp(s[t] - m)
            l = l + p.sum(axis=-1, keepdims=True)
            acc = acc + jnp.dot(p.astype(BF16), v_refs[t][...], preferred_element_type=F32)
        o_ref[rows, :] = (acc / l).astype(o_ref.dtype)


def _na_attention(proj, bias, qw, kw, *, bsz, seq):
    m = proj.shape[0]
    rows = seq // GRID_W
    nb = rows // NA_QROWS
    npieces = seq // NA_KPIECE
    q_col = OFF_NA // HEAD_DIM
    k_col = (OFF_NA + NA_WIDTH) // HEAD_DIM
    v_col = (OFF_NA + 2 * NA_WIDTH) // HEAD_DIM

    def kbase(i):
        return jnp.clip(2 * i - 1, 0, npieces - NA_NPIECE)

    def kv_spec(col, t):
        return pl.BlockSpec((NA_KPIECE, HEAD_DIM), lambda h, b, i: (b * npieces + kbase(i) + t, col + h))

    in_specs = [pl.BlockSpec((NA_TQ, HEAD_DIM), lambda h, b, i: (b * nb + i, q_col + h))]
    in_specs += [kv_spec(k_col, t) for t in range(NA_NPIECE)]
    in_specs += [kv_spec(v_col, t) for t in range(NA_NPIECE)]
    in_specs += [
        pl.BlockSpec((3, None, NA_TQ, NA_TK), lambda h, b, i: (0, h, 0, 0)),
        pl.BlockSpec((1, HEAD_DIM), lambda h, b, i: (0, 0)),
        pl.BlockSpec((1, HEAD_DIM), lambda h, b, i: (0, 0)),
    ]
    return pl.pallas_call(
        functools.partial(_na_kernel, nb=nb),
        grid=(NA_HEADS, bsz, nb),
        in_specs=in_specs,
        out_specs=pl.BlockSpec((NA_TQ, HEAD_DIM), lambda h, b, i: (b * nb + i, h)),
        out_shape=jax.ShapeDtypeStruct((m, NA_WIDTH), BF16),
        scratch_shapes=[pltpu.VMEM((NA_NPIECE, NA_KPIECE, HEAD_DIM), BF16)],
        compiler_params=_cp("parallel", "parallel", "parallel"),
        name="na_attention",
    )(proj, *([proj] * (2 * NA_NPIECE)), bias, qw, kw)


def _rope_tables(seq):
    t = np.arange(seq)
    pos = np.stack([t // GRID_W, t % GRID_W], axis=-1).astype(np.float32)
    n_freq = HEAD_DIM // 4
    inv_freq = jnp.asarray(ROPE_THETA, F32) ** (-jnp.arange(n_freq, dtype=F32) / n_freq)
    ang = jnp.asarray(pos)[:, :, None] * inv_freq
    cos = jnp.cos(ang)
    sin = jnp.sin(ang)
    cos_t = jnp.concatenate([cos[:, 0], cos[:, 0], cos[:, 1], cos[:, 1]], axis=-1)
    sin_t = jnp.concatenate([-sin[:, 0], sin[:, 0], -sin[:, 1], sin[:, 1]], axis=-1)
    return cos_t, sin_t


def _rope(x, cos_t, sin_t, first_half):
    swapped = jnp.where(first_half, pltpu.roll(x, HEAD_DIM - 32, axis=1), pltpu.roll(x, 32, axis=1))
    return x * cos_t + swapped * sin_t


def _gqa_prep_kernel(q_ref, k_ref, cos_ref, sin_ref, qw_ref, kw_ref, qo_ref, ko_ref):
    cos_t = cos_ref[...]
    sin_t = sin_ref[...]
    lane = lax.broadcasted_iota(jnp.int32, cos_t.shape, 1)
    first_half = (lane % 64) < 32
    scale = HEAD_DIM ** -0.5
    for h in range(GQA_HEADS):
        cols = pl.ds(h * HEAD_DIM, HEAD_DIM)
        x = _head_rmsnorm(q_ref[:, cols].astype(F32), qw_ref[...])
        qo_ref[:, cols] = (_rope(x, cos_t, sin_t, first_half) * scale).astype(BF16)
    for h in range(GQA_KV_HEADS):
        cols = pl.ds(h * HEAD_DIM, HEAD_DIM)
        x = _head_rmsnorm(k_ref[:, cols].astype(F32), kw_ref[...])
        ko_ref[:, cols] = _rope(x, cos_t, sin_t, first_half).astype(BF16)


def _gqa_prep(proj, cos_t, sin_t, qw, kw, *, seq, tm):
    m = proj.shape[0]
    tps = seq // tm
    return pl.pallas_call(
        _gqa_prep_kernel,
        grid=(m // tm,),
        in_specs=[
            pl.BlockSpec((tm, GQA_WIDTH), lambda i: (i, OFF_GQ // GQA_WIDTH)),
            pl.BlockSpec((tm, GQA_KV_WIDTH), lambda i: (i, OFF_GK // GQA_KV_WIDTH)),
            pl.BlockSpec((tm, HEAD_DIM), lambda i: (i % tps, 0)),
            pl.BlockSpec((tm, HEAD_DIM), lambda i: (i % tps, 0)),
            pl.BlockSpec((1, HEAD_DIM), lambda i: (0, 0)),
            pl.BlockSpec((1, HEAD_DIM), lambda i: (0, 0)),
        ],
        out_specs=[
            pl.BlockSpec((tm, GQA_WIDTH), lambda i: (i, 0)),
            pl.BlockSpec((tm, GQA_KV_WIDTH), lambda i: (i, 0)),
        ],
        out_shape=[
            jax.ShapeDtypeStruct((m, GQA_WIDTH), BF16),
            jax.ShapeDtypeStruct((m, GQA_KV_WIDTH), BF16),
        ],
        compiler_params=_cp("parallel"),
        name="gqa_prep",
    )(proj, proj, cos_t, sin_t, qw, kw)


def _flash_kernel(q_ref, k_ref, v_ref, o_ref, m_scr, l_scr, acc_scr, *, tq, tk, seq):
    q2 = jnp.concatenate([q_ref[:, :HEAD_DIM], q_ref[:, HEAD_DIM:]], axis=0)
    m_scr[...] = jnp.full(m_scr.shape, NEG, F32)
    l_scr[...] = jnp.zeros(l_scr.shape, F32)
    acc_scr[...] = jnp.zeros(acc_scr.shape, F32)

    def body(kc, c):
        ks = pl.ds(pl.multiple_of(kc * tk, tk), tk)
        s = lax.dot_general(q2, k_ref[ks, :], (((1,), (1,)), ((), ())), preferred_element_type=F32)
        m_prev = m_scr[...]
        m_new = jnp.maximum(m_prev, s.max(axis=-1, keepdims=True))
        alpha = jnp.exp(m_prev - m_new)
        p = jnp.exp(s - m_new)
        l_scr[...] = alpha * l_scr[...] + p.sum(axis=-1, keepdims=True)
        acc_scr[...] = alpha * acc_scr[...] + jnp.dot(p.astype(BF16), v_ref[ks, :], preferred_element_type=F32)
        m_scr[...] = m_new
        return c

    lax.fori_loop(0, seq // tk, body, 0)
    o = acc_scr[...] / l_scr[...]
    o_ref[:, :HEAD_DIM] = o[:tq].astype(o_ref.dtype)
    o_ref[:, HEAD_DIM:] = o[tq:].astype(o_ref.dtype)


def _gqa_attention(qn, kn, proj, *, bsz, seq, tq, tk):
    m = qn.shape[0]
    nq = seq // tq
    rep = GQA_HEADS // GQA_KV_HEADS
    v_col = OFF_GV // HEAD_DIM
    return pl.pallas_call(
        functools.partial(_flash_kernel, tq=tq, tk=tk, seq=seq),
        grid=(bsz, GQA_KV_HEADS, nq),
        in_specs=[
            pl.BlockSpec((tq, rep * HEAD_DIM), lambda b, g, i: (b * nq + i, g)),
            pl.BlockSpec((seq, HEAD_DIM), lambda b, g, i: (b, g)),
            pl.BlockSpec((seq, HEAD_DIM), lambda b, g, i: (b, v_col + g)),
        ],
        out_specs=pl.BlockSpec((tq, rep * HEAD_DIM), lambda b, g, i: (b * nq + i, g)),
        out_shape=jax.ShapeDtypeStruct((m, GQA_WIDTH), BF16),
        scratch_shapes=[
            pltpu.VMEM((rep * tq, 1), F32),
            pltpu.VMEM((rep * tq, 1), F32),
            pltpu.VMEM((rep * tq, HEAD_DIM), F32),
        ],
        compiler_params=_cp("parallel", "parallel", "parallel"),
        name="gqa_flash",
    )(qn, kn, proj)


def _with_halo(x, prev, nxt):
    return jnp.concatenate([x, nxt, prev], axis=0)


def _shift_rows(xcat, s, n):
    return pltpu.roll(xcat, s % xcat.shape[0], axis=0)[:n]


def _dwconv(x, prev, nxt, w_ref, b_ref, width):
    n = x.shape[0]
    xcat = _with_halo(x, prev, nxt)
    pad = width // 2
    out = b_ref[...] + x * w_ref[pad:pad + 1, :]
    for i in range(width):
        if i != pad:
            out = out + _shift_rows(xcat, pad - i, n) * w_ref[i:i + 1, :]
    return out


def _halo_specs(tm, width, col_block, m):
    nblk = m // HALO
    per = tm // HALO
    prev = pl.BlockSpec((HALO, width), lambda i, *_: (jnp.maximum(i * per - 1, 0), col_block))
    nxt = pl.BlockSpec((HALO, width), lambda i, *_: (jnp.minimum((i + 1) * per, nblk - 1), col_block))
    return prev, nxt


def _ssd_prep_kernel(x_ref, prev_ref, next_ref, dtr_ref, cw_ref, cb_ref, dtb_ref, xs_ref, bc_ref, dt_ref, *, tps):
    i = pl.program_id(0)
    first = (i % tps) == 0
    last = (i % tps) == tps - 1
    x = x_ref[...].astype(F32)
    prev = jnp.where(first, 0.0, prev_ref[...].astype(F32))
    nxt = jnp.where(last, 0.0, next_ref[...].astype(F32))
    y = _silu(_dwconv(x, prev, nxt, cw_ref, cb_ref, SSD_CONV))
    xs_ref[...] = y[:, :SSD_WIDTH].astype(BF16)
    bc_ref[...] = y[:, SSD_WIDTH:].astype(BF16)
    v = dtr_ref[...] + dtb_ref[...]
    dt_ref[...] = jnp.maximum(v, 0.0) + jnp.log(1.0 + jnp.exp(-jnp.abs(v)))


def _ssd_prep(proj, dt_raw, conv_w, conv_b, dt_bias, *, seq, tm):
    m = proj.shape[0]
    cb = OFF_XBC // SSD_XBC
    prev, nxt = _halo_specs(tm, SSD_XBC, cb, m)
    return pl.pallas_call(
        functools.partial(_ssd_prep_kernel, tps=seq // tm),
        grid=(m // tm,),
        in_specs=[
            pl.BlockSpec((tm, SSD_XBC), lambda i: (i, cb)),
            prev,
            nxt,
            pl.BlockSpec((tm, DT_PAD), lambda i: (i, 0)),
            pl.BlockSpec((SSD_CONV, SSD_XBC), lambda i: (0, 0)),
            pl.BlockSpec((1, SSD_XBC), lambda i: (0, 0)),
            pl.BlockSpec((1, DT_PAD), lambda i: (0, 0)),
        ],
        out_specs=[
            pl.BlockSpec((tm, SSD_WIDTH), lambda i: (i, 0)),
            pl.BlockSpec((tm, SSD_BC), lambda i: (i, 0)),
            pl.BlockSpec((tm, DT_PAD), lambda i: (i, 0)),
        ],
        out_shape=[
            jax.ShapeDtypeStruct((m, SSD_WIDTH), BF16),
            jax.ShapeDtypeStruct((m, SSD_BC), BF16),
            jax.ShapeDtypeStruct((m, DT_PAD), F32),
        ],
        compiler_params=_cp("parallel"),
        name="ssd_prep",
    )(proj, proj, proj, dt_raw, conv_w, conv_b, dt_bias)


def _ssd_constants():
    q = SSD_CHUNK
    li = np.arange(q)
    tri = np.stack([li[:, None] >= li[None, :], li[:, None] <= li[None, :]]).astype(np.float32)
    hp = np.arange(SSD_WIDTH) // SSD_HEAD_DIM
    expand = np.zeros((2, 2 * DT_PAD, SSD_WIDTH), np.float32)
    for d in range(2):
        expand[d, d * SSD_HEADS + hp, np.arange(SSD_WIDTH)] = 1.0
        expand[d, DT_PAD + d * SSD_HEADS + hp, np.arange(SSD_WIDTH)] = 1.0
    return jnp.asarray(tri, F32), jnp.asarray(expand, BF16)


def _expand_heads(v, e):
    hi = v.astype(BF16)
    lo = (v - hi.astype(F32)).astype(BF16)
    return jnp.dot(jnp.concatenate([hi, lo], axis=1), e, preferred_element_type=F32)


def _ssd_scan_kernel(xs_f, bc_f, dt_f, xs_b, bc_b, dt_b, alog_ref, tri_ref, exp_ref, yf_ref, yb_ref, h_scr):
    @pl.when(pl.program_id(1) == 0)
    def _():
        h_scr[...] = jnp.zeros(h_scr.shape, F32)

    q = SSD_CHUNK
    gw = SSD_WIDTH // SSD_GROUPS
    a_row = -jnp.exp(alog_ref[...])
    li = lax.broadcasted_iota(jnp.int32, (q, q), 0)
    si = lax.broadcasted_iota(jnp.int32, (q, q), 1)
    lane = lax.broadcasted_iota(jnp.int32, (q, HEAD_DIM), 1)
    left = lane < SSD_HEAD_DIM
    dirs = ((xs_f, bc_f, dt_f, yf_ref, q - 1), (xs_b, bc_b, dt_b, yb_ref, 0))
    for d, (xs_ref, bc_ref, dt_ref, y_ref, last) in enumerate(dirs):
        keep = (li >= si) if d == 0 else (li <= si)
        e = exp_ref[d]
        dt = dt_ref[...]
        da = dt * a_row
        cum = jnp.dot(tri_ref[d], da, precision=lax.Precision.HIGHEST, preferred_element_type=F32)
        cum_t = cum.T
        tot = cum[last:last + 1, :]
        dt_e = _expand_heads(dt, e)
        ec_e = _expand_heads(jnp.exp(cum), e)
        w_e = _expand_heads(jnp.exp(tot - cum), e)
        dec_e = ec_e[last:last + 1, :]
        xd = xs_ref[...].astype(F32) * dt_e
        xdb = xd.astype(BF16)
        xw = (xd * w_e).astype(BF16)
        for g in range(SSD_GROUPS):
            gl = slice(g * gw, (g + 1) * gw)
            bg = bc_ref[:, g * SSD_STATE:(g + 1) * SSD_STATE]
            cg = bc_ref[:, (SSD_GROUPS + g) * SSD_STATE:(SSD_GROUPS + g + 1) * SSD_STATE]
            cb = lax.dot_general(cg, bg, (((1,), (1,)), ((), ())), preferred_element_type=F32)
            bg_t = bg.astype(F32).T.astype(BF16)
            h_prev = h_scr[d, g]
            y_off = jnp.dot(cg, h_prev.astype(BF16), preferred_element_type=F32) * ec_e[:, gl]
            states_t = jnp.dot(bg_t, xw[:, gl], preferred_element_type=F32)
            h_scr[d, g] = h_prev * dec_e[:, gl] + states_t
            for j in range(gw // HEAD_DIM):
                mats = []
                for hh in range(2):
                    col = d * SSD_HEADS + g * (SSD_HEADS // SSD_GROUPS) + 2 * j + hh
                    seg = cum[:, col:col + 1] - cum_t[col:col + 1, :]
                    lm = jnp.exp(jnp.where(keep, seg, NEG))
                    mats.append((cb * lm).astype(BF16))
                lo = g * gw + j * HEAD_DIM
                slab = xdb[:, lo:lo + HEAD_DIM]
                zero = jnp.zeros_like(slab)
                wbd = jnp.concatenate([jnp.where(left, slab, zero), jnp.where(left, zero, slab)], axis=0)
                y_diag = jnp.dot(jnp.concatenate(mats, axis=1), wbd, preferred_element_type=F32)
                y_ref[:, lo:lo + HEAD_DIM] = (y_diag + y_off[:, j * HEAD_DIM:(j + 1) * HEAD_DIM]).astype(y_ref.dtype)


def _ssd_scan(xs, bc, dt, a_log, tri, expand, *, bsz, seq):
    m = xs.shape[0]
    nc = seq // SSD_CHUNK
    fwd = lambda b, c: (b * nc + c, 0)
    bwd = lambda b, c: (b * nc + nc - 1 - c, 0)
    return pl.pallas_call(
        _ssd_scan_kernel,
        grid=(bsz, nc),
        in_specs=[
            pl.BlockSpec((SSD_CHUNK, SSD_WIDTH), fwd),
            pl.BlockSpec((SSD_CHUNK, SSD_BC), fwd),
            pl.BlockSpec((SSD_CHUNK, DT_PAD), fwd),
            pl.BlockSpec((SSD_CHUNK, SSD_WIDTH), bwd),
            pl.BlockSpec((SSD_CHUNK, SSD_BC), bwd),
            pl.BlockSpec((SSD_CHUNK, DT_PAD), bwd),
            pl.BlockSpec((1, DT_PAD), lambda b, c: (0, 0)),
            pl.BlockSpec((2, SSD_CHUNK, SSD_CHUNK), lambda b, c: (0, 0, 0)),
            pl.BlockSpec((2, 2 * DT_PAD, SSD_WIDTH), lambda b, c: (0, 0, 0)),
        ],
        out_specs=[
            pl.BlockSpec((SSD_CHUNK, SSD_WIDTH), fwd),
            pl.BlockSpec((SSD_CHUNK, SSD_WIDTH), bwd),
        ],
        out_shape=[
            jax.ShapeDtypeStruct((m, SSD_WIDTH), BF16),
            jax.ShapeDtypeStruct((m, SSD_WIDTH), BF16),
        ],
        scratch_shapes=[pltpu.VMEM((2, SSD_GROUPS, SSD_STATE, SSD_WIDTH // SSD_GROUPS), F32)],
        compiler_params=_cp("parallel", "arbitrary"),
        name="ssd_scan",
    )(xs, bc, dt, xs, bc, dt, a_log, tri, expand)


def _ssd_finish_kernel(yf_ref, yb_ref, xs_ref, z_ref, d_ref, nw_ref, o_ref):
    y = yf_ref[...].astype(F32) + yb_ref[...].astype(F32) + xs_ref[...].astype(F32) * d_ref[...]
    y = y * _silu(z_ref[...].astype(F32))
    gw = SSD_WIDTH // SSD_GROUPS
    for g in range(SSD_GROUPS):
        yg = y[:, g * gw:(g + 1) * gw]
        ms = jnp.mean(yg * yg, axis=-1, keepdims=True)
        o_ref[:, g * gw:(g + 1) * gw] = (yg * lax.rsqrt(ms + EPS) * nw_ref[:, g * gw:(g + 1) * gw]).astype(o_ref.dtype)


def _ssd_finish(yf, yb, xs, proj, d_exp, norm_w, *, tm):
    m = yf.shape[0]
    row = lambda i: (i, 0)
    const = lambda i: (0, 0)
    return pl.pallas_call(
        _ssd_finish_kernel,
        grid=(m // tm,),
        in_specs=[
            pl.BlockSpec((tm, SSD_WIDTH), row),
            pl.BlockSpec((tm, SSD_WIDTH), row),
            pl.BlockSpec((tm, SSD_WIDTH), row),
            pl.BlockSpec((tm, SSD_WIDTH), lambda i: (i, OFF_Z // SSD_WIDTH)),
            pl.BlockSpec((1, SSD_WIDTH), const),
            pl.BlockSpec((1, SSD_WIDTH), const),
        ],
        out_specs=pl.BlockSpec((tm, SSD_WIDTH), row),
        out_shape=jax.ShapeDtypeStruct((m, SSD_WIDTH), BF16),
        compiler_params=_cp("parallel"),
        name="ssd_finish",
    )(yf, yb, xs, proj, d_exp, norm_w)


def _merge_kernel(ona_ref, ossd_ref, ogqa_ref, g_ref, x_ref, wna_ref, wssd_ref, wgqa_ref, wout_ref, o_ref, mg_scr):
    nc = 512
    branches = ((ona_ref, wna_ref), (ossd_ref, wssd_ref), (ogqa_ref, wgqa_ref))
    for c in range(D_MODEL // nc):
        cols = pl.ds(c * nc, nc)
        acc = None
        for b, (o_b, w_b) in enumerate(branches):
            p = jnp.dot(o_b[...], w_b[:, cols], preferred_element_type=F32)
            gate = _sigmoid(g_ref[:, pl.ds(b * D_MODEL + c * nc, nc)].astype(F32))
            acc = gate * p if acc is None else acc + gate * p
        mg_scr[:, cols] = acc.astype(BF16)
    for c in range(D_MODEL // nc):
        cols = pl.ds(c * nc, nc)
        o_ref[:, cols] = x_ref[:, cols] + jnp.dot(mg_scr[...], wout_ref[:, cols], preferred_element_type=F32)


def _merge(o_na, o_ssd, o_gqa, proj, x, w_na, w_ssd, w_gqa, w_out, *, tm):
    m = x.shape[0]
    row = lambda i: (i, 0)
    const = lambda i: (0, 0)
    one = pl.Buffered(1)
    return pl.pallas_call(
        _merge_kernel,
        grid=(m // tm,),
        in_specs=[
            pl.BlockSpec((tm, NA_WIDTH), row),
            pl.BlockSpec((tm, SSD_WIDTH), row),
            pl.BlockSpec((tm, GQA_WIDTH), row),
            pl.BlockSpec((tm, GATES), row),
            pl.BlockSpec((tm, D_MODEL), row),
            pl.BlockSpec((NA_WIDTH, D_MODEL), const, pipeline_mode=one),
            pl.BlockSpec((SSD_WIDTH, D_MODEL), const, pipeline_mode=one),
            pl.BlockSpec((GQA_WIDTH, D_MODEL), const, pipeline_mode=one),
            pl.BlockSpec((D_MODEL, D_MODEL), const, pipeline_mode=one),
        ],
        out_specs=pl.BlockSpec((tm, D_MODEL), row),
        out_shape=jax.ShapeDtypeStruct((m, D_MODEL), F32),
        scratch_shapes=[pltpu.VMEM((tm, D_MODEL), BF16)],
        compiler_params=_cp("parallel"),
        name="merge",
    )(o_na, o_ssd, o_gqa, proj, x, w_na, w_ssd, w_gqa, w_out)


def _ffn_down_kernel(ug_ref, prev_ref, next_ref, uv_ref, cw_ref, cb_ref, wd_ref, x_ref, o_ref, acc_scr, *, tps, nk):
    i = pl.program_id(0)
    j = pl.program_id(1)
    first = (i % tps) == 0
    last = (i % tps) == tps - 1
    g = ug_ref[...].astype(F32)
    prev = jnp.where(first, 0.0, prev_ref[...].astype(F32))
    nxt = jnp.where(last, 0.0, next_ref[...].astype(F32))
    a = _dwconv(g, prev, nxt, cw_ref, cb_ref, 3)
    hid = (_silu(a) * uv_ref[...].astype(F32)).astype(BF16)
    part = jnp.dot(hid, wd_ref[...], preferred_element_type=F32)

    @pl.when(j == 0)
    def _():
        acc_scr[...] = x_ref[...] + part

    @pl.when(j > 0)
    def _():
        acc_scr[...] += part

    @pl.when(j == nk - 1)
    def _():
        o_ref[...] = acc_scr[...]


def _ffn_down(u, conv_w, conv_b, w_down, x, *, seq, tm, tk):
    m = x.shape[0]
    nk = D_FF // tk
    nblk = m // HALO
    per = tm // HALO
    return pl.pallas_call(
        functools.partial(_ffn_down_kernel, tps=seq // tm, nk=nk),
        grid=(m // tm, nk),
        in_specs=[
            pl.BlockSpec((tm, tk), lambda i, j: (i, j)),
            pl.BlockSpec((HALO, tk), lambda i, j: (jnp.maximum(i * per - 1, 0), j)),
            pl.BlockSpec((HALO, tk), lambda i, j: (jnp.minimum((i + 1) * per, nblk - 1), j)),
            pl.BlockSpec((tm, tk), lambda i, j: (i, nk + j)),
            pl.BlockSpec((3, tk), lambda i, j: (0, j)),
            pl.BlockSpec((1, tk), lambda i, j: (0, j)),
            pl.BlockSpec((tk, D_MODEL), lambda i, j: (j, 0)),
            pl.BlockSpec((tm, D_MODEL), lambda i, j: (i, 0)),
        ],
        out_specs=pl.BlockSpec((tm, D_MODEL), lambda i, j: (i, 0)),
        out_shape=jax.ShapeDtypeStruct((m, D_MODEL), F32),
        scratch_shapes=[pltpu.VMEM((tm, D_MODEL), F32)],
        compiler_params=_cp("parallel", "arbitrary"),
        name="ffn_down",
    )(u, u, u, u, conv_w, conv_b, w_down, x)


def _tiles(seq):
    return dict(tm_mm=min(seq, 1024), tm_ew=min(seq, 512), tm_merge=min(seq, 512), tq=min(seq, 512), tk=min(seq, 512))


def _prep_layer_params(p):
    (norm1_w, w_in, na_q_norm, na_k_norm, na_rpb, ssd_conv_w, ssd_conv_b, ssd_dt_bias, ssd_a_log, ssd_d,
     ssd_norm_w, gqa_q_norm, gqa_k_norm, w_branch, w_out, norm2_w, ffn_w_up, ffn_conv_w, ffn_conv_b, ffn_w_down) = p
    sizes = (3 * NA_WIDTH, GQA_WIDTH, GQA_KV_WIDTH, GQA_KV_WIDTH, SSD_WIDTH, SSD_XBC, 2 * SSD_HEADS, GATES)
    offs = np.concatenate([[0], np.cumsum(sizes)])
    piece = lambda k: w_in[:, offs[k]:offs[k + 1]]
    na, gq, gk, gv, z, xbc, dtw, gates = (piece(k) for k in range(8))
    w_main = jnp.concatenate([gates, xbc, na, z, gq, gk, gv], axis=1).astype(BF16)
    w_dt = jnp.pad(dtw, ((0, 0), (0, DT_PAD - 2 * SSD_HEADS))).astype(BF16)
    pad_dt = lambda v: jnp.pad(v.reshape(1, 2 * SSD_HEADS), ((0, 0), (0, DT_PAD - 2 * SSD_HEADS)))
    row = lambda v: v.reshape(1, -1)
    wb = w_branch.astype(BF16)
    return dict(
        norm1_w=row(norm1_w), w_main=w_main, w_dt=w_dt,
        na_q_norm=row(na_q_norm), na_k_norm=row(na_k_norm), na_rpb=na_rpb,
        conv_w=ssd_conv_w, conv_b=row(ssd_conv_b), dt_bias=pad_dt(ssd_dt_bias), a_log=pad_dt(ssd_a_log),
        d_exp=row(jnp.repeat(ssd_d, SSD_HEAD_DIM)), ssd_norm_w=row(ssd_norm_w),
        gqa_q_norm=row(gqa_q_norm), gqa_k_norm=row(gqa_k_norm),
        w_na=wb[:NA_WIDTH], w_ssd=wb[NA_WIDTH:NA_WIDTH + SSD_WIDTH], w_gqa=wb[NA_WIDTH + SSD_WIDTH:],
        w_out=w_out.astype(BF16), norm2_w=row(norm2_w), w_up=ffn_w_up.astype(BF16),
        ffn_conv_w=ffn_conv_w, ffn_conv_b=row(ffn_conv_b), w_down=ffn_w_down.astype(BF16),
    )


def _layer(x, lp, consts, *, bsz, seq):
    t = _tiles(seq)
    cos_t, sin_t, tri, expand = consts
    proj, dt_raw = _norm_matmul(x, lp["norm1_w"], lp["w_main"], lp["w_dt"], tm=t["tm_mm"], tn=1024)
    bias = _na_bias_tables(lp["na_rpb"], seq // GRID_W)
    o_na = _na_attention(proj, bias, lp["na_q_norm"], lp["na_k_norm"], bsz=bsz, seq=seq)
    qn, kn = _gqa_prep(proj, cos_t, sin_t, lp["gqa_q_norm"], lp["gqa_k_norm"], seq=seq, tm=t["tm_ew"])
    o_gqa = _gqa_attention(qn, kn, proj, bsz=bsz, seq=seq, tq=t["tq"], tk=t["tk"])
    xs, bc, dt = _ssd_prep(proj, dt_raw, lp["conv_w"], lp["conv_b"], lp["dt_bias"], seq=seq, tm=t["tm_ew"])
    yf, yb = _ssd_scan(xs, bc, dt, lp["a_log"], tri, expand, bsz=bsz, seq=seq)
    o_ssd = _ssd_finish(yf, yb, xs, proj, lp["d_exp"], lp["ssd_norm_w"], tm=t["tm_ew"])
    x = _merge(o_na, o_ssd, o_gqa, proj, x, lp["w_na"], lp["w_ssd"], lp["w_gqa"], lp["w_out"], tm=t["tm_merge"])
    u = _norm_matmul(x, lp["norm2_w"], lp["w_up"], tm=t["tm_mm"], tn=1024)
    return _ffn_down(u, lp["ffn_conv_w"], lp["ffn_conv_b"], lp["w_down"], x, seq=seq, tm=t["tm_mm"], tk=512)


def _trunk(x, layer_params):
    bsz, seq, d = x.shape
    assert d == D_MODEL and seq % 1024 == 0 and seq // GRID_W >= NA_KROWS
    consts = _rope_tables(seq) + _ssd_constants()
    h = x.reshape(bsz * seq, d)
    for lp in layer_params:
        h = _layer(h, lp, consts, bsz=bsz, seq=seq)
    return h.reshape(bsz, seq, d)


def kernel(x_prompt, x_sample, norm1_w, w_in, na_q_norm, na_k_norm, na_rpb, ssd_conv_w, ssd_conv_b, ssd_dt_bias, ssd_a_log, ssd_d, ssd_norm_w, gqa_q_norm, gqa_k_norm, w_branch, w_out, norm2_w, ffn_w_up, ffn_conv_w, ffn_conv_b, ffn_w_down):
    params = (norm1_w, w_in, na_q_norm, na_k_norm, na_rpb, ssd_conv_w, ssd_conv_b, ssd_dt_bias, ssd_a_log,
              ssd_d, ssd_norm_w, gqa_q_norm, gqa_k_norm, w_branch, w_out, norm2_w, ffn_w_up, ffn_conv_w,
              ffn_conv_b, ffn_w_down)
    depth = norm1_w.shape[0]
    layer_params = [_prep_layer_params([p[i] for p in params]) for i in range(depth)]
    return (_trunk(x_prompt, layer_params), _trunk(x_sample, layer_params))
```

```python
import functools
import math

import numpy as np
import jax
import jax.numpy as jnp
from jax import lax
from jax.experimental import pallas as pl
from jax.experimental.pallas import tpu as pltpu

F32 = jnp.float32
BF16 = jnp.bfloat16

D_MODEL = 2048
GRID_W = 64
EPS = 1e-6
NA_HEADS = 4
HEAD_DIM = 128
NA_WIDTH = NA_HEADS * HEAD_DIM
NA_WIN_H = 8
NA_WIN_W = 16
SSD_HEADS = 16
SSD_HEAD_DIM = 64
SSD_WIDTH = SSD_HEADS * SSD_HEAD_DIM
SSD_STATE = 128
SSD_GROUPS = 2
SSD_CONV = 5
SSD_CHUNK = 128
SSD_BC = 2 * SSD_GROUPS * SSD_STATE
SSD_XBC = SSD_WIDTH + SSD_BC
GQA_HEADS = 4
GQA_KV_HEADS = 2
GQA_WIDTH = GQA_HEADS * HEAD_DIM
GQA_KV_WIDTH = GQA_KV_HEADS * HEAD_DIM
ROPE_THETA = 10000.0
D_FF = 5632
N_BRANCH = 3
GATES = N_BRANCH * D_MODEL
DT_PAD = 128

OFF_GATES = 0
OFF_XBC = OFF_GATES + GATES
OFF_NA = OFF_XBC + SSD_XBC
OFF_Z = OFF_NA + 3 * NA_WIDTH
OFF_GQ = OFF_Z + SSD_WIDTH
OFF_GK = OFF_GQ + GQA_WIDTH
OFF_GV = OFF_GK + GQA_KV_WIDTH
PROJ_COLS = OFF_GV + GQA_KV_WIDTH

NEG = -1e30
HALO = 8
VMEM_LIMIT = 56 * 2**20


def _cp(*sem):
    return pltpu.CompilerParams(dimension_semantics=sem, vmem_limit_bytes=VMEM_LIMIT)


def _sigmoid(x):
    return 1.0 / (1.0 + jnp.exp(-x))


def _silu(x):
    return x * _sigmoid(x)


def _norm_mm_kernel(*refs, tm, with_dt):
    if with_dt:
        x_ref, nw_ref, w_ref, wdt_ref, o_ref, odt_ref, h_scr = refs
    else:
        x_ref, nw_ref, w_ref, o_ref, h_scr = refs
    rc = 128

    @pl.when(pl.program_id(1) == 0)
    def _():
        def body(r, c):
            rows = pl.ds(pl.multiple_of(r * rc, rc), rc)
            x = x_ref[rows, :]
            ms = jnp.mean(x * x, axis=-1, keepdims=True)
            h_scr[rows, :] = (x * lax.rsqrt(ms + EPS) * nw_ref[...]).astype(BF16)
            return c

        lax.fori_loop(0, tm // rc, body, 0)
        if with_dt:
            odt_ref[...] = jnp.dot(h_scr[...], wdt_ref[...], preferred_element_type=F32)

    o_ref[...] = jnp.dot(h_scr[...], w_ref[...], preferred_element_type=F32).astype(o_ref.dtype)


def _norm_matmul(x, nw, w, wdt=None, *, tm, tn):
    m, d = x.shape
    n = w.shape[1]
    with_dt = wdt is not None
    in_specs = [
        pl.BlockSpec((tm, d), lambda i, j: (i, 0)),
        pl.BlockSpec((1, d), lambda i, j: (0, 0)),
        pl.BlockSpec((d, tn), lambda i, j: (0, j)),
    ]
    out_shape = [jax.ShapeDtypeStruct((m, n), BF16)]
    out_specs = [pl.BlockSpec((tm, tn), lambda i, j: (i, j))]
    args = [x, nw, w]
    if with_dt:
        in_specs.append(pl.BlockSpec((d, DT_PAD), lambda i, j: (0, 0)))
        out_shape.append(jax.ShapeDtypeStruct((m, DT_PAD), F32))
        out_specs.append(pl.BlockSpec((tm, DT_PAD), lambda i, j: (i, 0)))
        args.append(wdt)
    res = pl.pallas_call(
        functools.partial(_norm_mm_kernel, tm=tm, with_dt=with_dt),
        grid=(m // tm, n // tn),
        in_specs=in_specs,
        out_specs=out_specs,
        out_shape=out_shape,
        scratch_shapes=[pltpu.VMEM((tm, d), BF16)],
        compiler_params=_cp("parallel", "arbitrary"),
        name="norm_matmul_dt" if with_dt else "norm_matmul",
    )(*args)
    return res if with_dt else res[0]


NA_QROWS = 8
NA_KROWS = 16
NA_TQ = NA_QROWS * GRID_W
NA_TK = NA_KROWS * GRID_W
NA_KPIECE = 256
NA_NPIECE = NA_TK // NA_KPIECE


def _na_bias_tables(rpb):
    rows = 4 * NA_QROWS
    nb = rows // NA_QROWS
    depth = rpb.shape[0]
    c = np.arange(GRID_W)
    col_start = np.clip(c - NA_WIN_W // 2, 0, GRID_W - NA_WIN_W)
    in_win = (c[None, :] >= col_start[:, None]) & (c[None, :] < col_start[:, None] + NA_WIN_W)
    dc = np.clip(c[None, :] - c[:, None], -(NA_WIN_W - 1), NA_WIN_W - 1) + NA_WIN_W - 1
    onehot_c = (dc[:, :, None] == np.arange(2 * NA_WIN_W - 1)).astype(np.float32)
    by_col = jnp.einsum("lhab,qcb->lhaqc", rpb.astype(F32), jnp.asarray(onehot_c), precision=lax.Precision.HIGHEST)
    tables = []
    for b in (0, 1, nb - 1):
        r = NA_QROWS * b + np.arange(NA_QROWS)
        ks = int(np.clip(NA_QROWS * b - NA_WIN_H // 2, 0, rows - NA_KROWS))
        krow = ks + np.arange(NA_KROWS)
        rs = np.clip(r - NA_WIN_H // 2, 0, rows - NA_WIN_H)
        valid = (krow[None, :] >= rs[:, None]) & (krow[None, :] < rs[:, None] + NA_WIN_H)
        dr = np.clip(krow[None, :] - r[:, None], -(NA_WIN_H - 1), NA_WIN_H - 1) + NA_WIN_H - 1
        bias = jnp.take(by_col, jnp.asarray(dr.reshape(-1)), axis=2)
        bias = bias.reshape(depth, NA_HEADS, NA_QROWS, NA_KROWS, GRID_W, GRID_W).transpose(0, 1, 2, 4, 3, 5)
        mask = valid[:, None, :, None] & in_win[None, :, None, :]
        bias = jnp.where(jnp.asarray(mask)[None, None], bias, NEG)
        tables.append(bias.reshape(depth, NA_HEADS, NA_TQ, NA_TK))
    return jnp.stack(tables, axis=1)


def _head_rmsnorm(x, w):
    ms = jnp.mean(x * x, axis=-1, keepdims=True)
    return x * lax.rsqrt(ms + EPS) * w


def _na_kernel(q_ref, k0, k1, k2, k3, v0, v1, v2, v3, bias_ref, qw_ref, kw_ref, o_ref, kn_scr, *, nb):
    i = pl.program_id(2)
    kind = jnp.where(i == 0, 0, jnp.where(i == nb - 1, 2, 1))
    scale = HEAD_DIM ** -0.5
    k_refs = (k0, k1, k2, k3)
    v_refs = (v0, v1, v2, v3)
    for t in range(NA_NPIECE):
        kn_scr[t] = _head_rmsnorm(k_refs[t][...].astype(F32), kw_ref[...]).astype(BF16)
    sub = 128
    for qs in range(NA_TQ // sub):
        rows = pl.ds(qs * sub, sub)
        qn = (_head_rmsnorm(q_ref[rows, :].astype(F32), qw_ref[...]) * scale).astype(BF16)
        s = []
        for t in range(NA_NPIECE):
            st = lax.dot_general(qn, kn_scr[t], (((1,), (1,)), ((), ())), preferred_element_type=F32)
            s.append(st + bias_ref[kind, rows, pl.ds(t * NA_KPIECE, NA_KPIECE)])
        m = s[0].max(axis=-1, keepdims=True)
        for t in range(1, NA_NPIECE):
            m = jnp.maximum(m, s[t].max(axis=-1, keepdims=True))
        l = jnp.zeros_like(m)
        acc = jnp.zeros((sub, HEAD_DIM), F32)
        for t in range(NA_NPIECE):
            p = jnp.exp(s[t] - m)
            l = l + p.sum(axis=-1, keepdims=True)
            acc = acc + jnp.dot(p.astype(BF16), v_refs[t][...], preferred_element_type=F32)
        o_ref[rows, :] = (acc / l).astype(o_ref.dtype)


def _na_attention(proj, bias, qw, kw, *, bsz, seq):
    m = proj.shape[0]
    rows = seq // GRID_W
    nb = rows // NA_QROWS
    npieces = seq // NA_KPIECE
    q_col = OFF_NA // HEAD_DIM
    k_col = (OFF_NA + NA_WIDTH) // HEAD_DIM
    v_col = (OFF_NA + 2 * NA_WIDTH) // HEAD_DIM

    def kbase(i):
        return jnp.clip(2 * i - 1, 0, npieces - NA_NPIECE)

    def kv_spec(col, t):
        return pl.BlockSpec((NA_KPIECE, HEAD_DIM), lambda h, b, i: (b * npieces + kbase(i) + t, col + h))

    in_specs = [pl.BlockSpec((NA_TQ, HEAD_DIM), lambda h, b, i: (b * nb + i, q_col + h))]
    in_specs += [kv_spec(k_col, t) for t in range(NA_NPIECE)]
    in_specs += [kv_spec(v_col, t) for t in range(NA_NPIECE)]
    in_specs += [
        pl.BlockSpec((3, None, NA_TQ, NA_TK), lambda h, b, i: (0, h, 0, 0)),
        pl.BlockSpec((1, HEAD_DIM), lambda h, b, i: (0, 0)),
        pl.BlockSpec((1, HEAD_DIM), lambda h, b, i: (0, 0)),
    ]
    return pl.pallas_call(
        functools.partial(_na_kernel, nb=nb),
        grid=(NA_HEADS, bsz, nb),
        in_specs=in_specs,
        out_specs=pl.BlockSpec((NA_TQ, HEAD_DIM), lambda h, b, i: (b * nb + i, h)),
        out_shape=jax.ShapeDtypeStruct((m, NA_WIDTH), BF16),
        scratch_shapes=[pltpu.VMEM((NA_NPIECE, NA_KPIECE, HEAD_DIM), BF16)],
        compiler_params=_cp("parallel", "parallel", "parallel"),
        name="na_attention",
    )(proj, *([proj] * (2 * NA_NPIECE)), bias, qw, kw)


def _rope_tables(seq):
    t = np.arange(seq)
    pos = np.stack([t // GRID_W, t % GRID_W], axis=-1).astype(np.float32)
    n_freq = HEAD_DIM // 4
    inv_freq = jnp.asarray(ROPE_THETA, F32) ** (-jnp.arange(n_freq, dtype=F32) / n_freq)
    ang = jnp.asarray(pos)[:, :, None] * inv_freq
    cos = jnp.cos(ang)
    sin = jnp.sin(ang)
    cos_t = jnp.concatenate([cos[:, 0], cos[:, 0], cos[:, 1], cos[:, 1]], axis=-1)
    sin_t = jnp.concatenate([-sin[:, 0], sin[:, 0], -sin[:, 1], sin[:, 1]], axis=-1)
    return cos_t, sin_t


def _rope(x, cos_t, sin_t, first_half):
    swapped = jnp.where(first_half, pltpu.roll(x, HEAD_DIM - 32, axis=1), pltpu.roll(x, 32, axis=1))
    return x * cos_t + swapped * sin_t


def _gqa_prep_kernel(q_ref, k_ref, v_ref, cos_ref, sin_ref, qw_ref, kw_ref, qo_ref, ko_ref, vo_ref):
    cos_t = cos_ref[...]
    sin_t = sin_ref[...]
    lane = lax.broadcasted_iota(jnp.int32, cos_t.shape, 1)
    first_half = (lane % 64) < 32
    scale = HEAD_DIM ** -0.5 * math.log2(math.e)
    for h in range(GQA_HEADS):
        cols = pl.ds(h * HEAD_DIM, HEAD_DIM)
        x = _head_rmsnorm(q_ref[:, cols].astype(F32), qw_ref[...])
        qo_ref[:, cols] = (_rope(x, cos_t, sin_t, first_half) * scale).astype(BF16)
    ones = jnp.ones(cos_t.shape, BF16)
    for h in range(GQA_KV_HEADS):
        cols = pl.ds(h * HEAD_DIM, HEAD_DIM)
        x = _head_rmsnorm(k_ref[:, cols].astype(F32), kw_ref[...])
        ko_ref[:, cols] = _rope(x, cos_t, sin_t, first_half).astype(BF16)
        vo_ref[:, pl.ds(2 * h * HEAD_DIM, HEAD_DIM)] = v_ref[:, cols]
        vo_ref[:, pl.ds((2 * h + 1) * HEAD_DIM, HEAD_DIM)] = ones


def _gqa_prep(proj, cos_t, sin_t, qw, kw, *, seq, tm):
    m = proj.shape[0]
    tps = seq // tm
    return pl.pallas_call(
        _gqa_prep_kernel,
        grid=(m // tm,),
        in_specs=[
            pl.BlockSpec((tm, GQA_WIDTH), lambda i: (i, OFF_GQ // GQA_WIDTH)),
            pl.BlockSpec((tm, GQA_KV_WIDTH), lambda i: (i, OFF_GK // GQA_KV_WIDTH)),
            pl.BlockSpec((tm, GQA_KV_WIDTH), lambda i: (i, OFF_GV // GQA_KV_WIDTH)),
            pl.BlockSpec((tm, HEAD_DIM), lambda i: (i % tps, 0)),
            pl.BlockSpec((tm, HEAD_DIM), lambda i: (i % tps, 0)),
            pl.BlockSpec((1, HEAD_DIM), lambda i: (0, 0)),
            pl.BlockSpec((1, HEAD_DIM), lambda i: (0, 0)),
        ],
        out_specs=[
            pl.BlockSpec((tm, GQA_WIDTH), lambda i: (i, 0)),
            pl.BlockSpec((tm, GQA_KV_WIDTH), lambda i: (i, 0)),
            pl.BlockSpec((tm, 2 * GQA_KV_WIDTH), lambda i: (i, 0)),
        ],
        out_shape=[
            jax.ShapeDtypeStruct((m, GQA_WIDTH), BF16),
            jax.ShapeDtypeStruct((m, GQA_KV_WIDTH), BF16),
            jax.ShapeDtypeStruct((m, 2 * GQA_KV_WIDTH), BF16),
        ],
        compiler_params=_cp("parallel"),
        name="gqa_prep",
    )(proj, proj, proj, cos_t, sin_t, qw, kw)


def _flash_kernel(q_ref, k_ref, v_ref, o_ref, s_scr, m_scr, acc_scr, *, tq, tk, seq):
    q2 = jnp.concatenate([q_ref[:, :HEAD_DIM], q_ref[:, HEAD_DIM:]], axis=0)
    m_scr[...] = jnp.full(m_scr.shape, NEG, F32)
    acc_scr[...] = jnp.zeros(acc_scr.shape, F32)
    nblk = tk // HEAD_DIM
    n = seq // tk

    def chunk(kc):
        return pl.ds(pl.multiple_of(kc * tk, tk), tk)

    def scores(kc):
        return lax.dot_general(q2, k_ref[chunk(kc), :], (((1,), (1,)), ((), ())), preferred_element_type=F32)

    def softmax_pv(kc, slot):
        blocks = [s_scr[slot, :, j * HEAD_DIM:(j + 1) * HEAD_DIM] for j in range(nblk)]
        part = blocks[0]
        for blk in blocks[1:]:
            part = jnp.maximum(part, blk)
        m_prev = m_scr[...]
        m_new = jnp.maximum(m_prev, jnp.max(part, axis=-1, keepdims=True))
        alpha = jnp.exp2(m_prev - m_new)
        p = jnp.concatenate([jnp.exp2(blk - m_new).astype(BF16) for blk in blocks], axis=1)
        pv = jnp.dot(p, v_ref[chunk(kc), :], preferred_element_type=F32)
        acc_scr[...] = jnp.concatenate([alpha, alpha], axis=1) * acc_scr[...] + pv
        m_scr[...] = m_new

    s_scr[0] = scores(0)

    def pair(t, c):
        s_scr[1] = scores(2 * t + 1)
        softmax_pv(2 * t, 0)
        s_scr[0] = scores(2 * t + 2)
        softmax_pv(2 * t + 1, 1)
        return c

    npairs = (n - 1) // 2
    lax.fori_loop(0, npairs, pair, 0)
    base = 2 * npairs
    if n - base == 2:
        s_scr[1] = scores(base + 1)
        softmax_pv(base, 0)
        softmax_pv(base + 1, 1)
    else:
        softmax_pv(base, 0)
    o = acc_scr[:, :HEAD_DIM] / acc_scr[:, HEAD_DIM:]
    o_ref[:, :HEAD_DIM] = o[:tq].astype(o_ref.dtype)
    o_ref[:, HEAD_DIM:] = o[tq:].astype(o_ref.dtype)


def _gqa_attention(qn, kn, vaug, *, bsz, seq, tq, tk):
    m = qn.shape[0]
    nq = seq // tq
    rep = GQA_HEADS // GQA_KV_HEADS
    return pl.pallas_call(
        functools.partial(_flash_kernel, tq=tq, tk=tk, seq=seq),
        grid=(bsz, GQA_KV_HEADS, nq),
        in_specs=[
            pl.BlockSpec((tq, rep * HEAD_DIM), lambda b, g, i: (b * nq + i, g)),
            pl.BlockSpec((seq, HEAD_DIM), lambda b, g, i: (b, g)),
            pl.BlockSpec((seq, 2 * HEAD_DIM), lambda b, g, i: (b, g)),
        ],
        out_specs=pl.BlockSpec((tq, rep * HEAD_DIM), lambda b, g, i: (b * nq + i, g)),
        out_shape=jax.ShapeDtypeStruct((m, GQA_WIDTH), BF16),
        scratch_shapes=[
            pltpu.VMEM((2, rep * tq, tk), F32),
            pltpu.VMEM((rep * tq, HEAD_DIM), F32),
            pltpu.VMEM((rep * tq, 2 * HEAD_DIM), F32),
        ],
        compiler_params=_cp("parallel", "parallel", "parallel"),
        name="gqa_flash",
    )(qn, kn, vaug)


def _with_halo(x, prev, nxt):
    return jnp.concatenate([x, nxt, prev], axis=0)


def _shift_rows(xcat, s, n):
    return pltpu.roll(xcat, s % xcat.shape[0], axis=0)[:n]


def _dwconv(x, prev, nxt, w, b, width):
    n = x.shape[0]
    xcat = _with_halo(x, prev, nxt)
    pad = width // 2
    out = b + x * w[pad:pad + 1, :]
    for i in range(width):
        if i != pad:
            out = out + _shift_rows(xcat, pad - i, n) * w[i:i + 1, :]
    return out


def _halo_specs(tm, width, col_block, m):
    nblk = m // HALO
    per = tm // HALO
    prev = pl.BlockSpec((HALO, width), lambda i, *_: (jnp.maximum(i * per - 1, 0), col_block))
    nxt = pl.BlockSpec((HALO, width), lambda i, *_: (jnp.minimum((i + 1) * per, nblk - 1), col_block))
    return prev, nxt


def _ssd_prep_kernel(x_ref, prev_ref, next_ref, dtr_ref, cw_ref, cb_ref, dtb_ref, xs_ref, bc_ref, dt_ref, *, tps):
    i = pl.program_id(0)
    first = (i % tps) == 0
    last = (i % tps) == tps - 1
    x = x_ref[...].astype(F32)
    prev = jnp.where(first, 0.0, prev_ref[...].astype(F32))
    nxt = jnp.where(last, 0.0, next_ref[...].astype(F32))
    y = _silu(_dwconv(x, prev, nxt, cw_ref[...], cb_ref[...], SSD_CONV))
    xs_ref[...] = y[:, :SSD_WIDTH].astype(BF16)
    bc_ref[...] = y[:, SSD_WIDTH:].astype(BF16)
    v = dtr_ref[...] + dtb_ref[...]
    dt_ref[...] = jnp.maximum(v, 0.0) + jnp.log(1.0 + jnp.exp(-jnp.abs(v)))


def _ssd_prep(proj, dt_raw, conv_w, conv_b, dt_bias, *, seq, tm):
    m = proj.shape[0]
    cb = OFF_XBC // SSD_XBC
    prev, nxt = _halo_specs(tm, SSD_XBC, cb, m)
    return pl.pallas_call(
        functools.partial(_ssd_prep_kernel, tps=seq // tm),
        grid=(m // tm,),
        in_specs=[
            pl.BlockSpec((tm, SSD_XBC), lambda i: (i, cb)),
            prev,
            nxt,
            pl.BlockSpec((tm, DT_PAD), lambda i: (i, 0)),
            pl.BlockSpec((SSD_CONV, SSD_XBC), lambda i: (0, 0)),
            pl.BlockSpec((1, SSD_XBC), lambda i: (0, 0)),
            pl.BlockSpec((1, DT_PAD), lambda i: (0, 0)),
        ],
        out_specs=[
            pl.BlockSpec((tm, SSD_WIDTH), lambda i: (i, 0)),
            pl.BlockSpec((tm, SSD_BC), lambda i: (i, 0)),
            pl.BlockSpec((tm, DT_PAD), lambda i: (i, 0)),
        ],
        out_shape=[
            jax.ShapeDtypeStruct((m, SSD_WIDTH), BF16),
            jax.ShapeDtypeStruct((m, SSD_BC), BF16),
            jax.ShapeDtypeStruct((m, DT_PAD), F32),
        ],
        compiler_params=_cp("parallel"),
        name="ssd_prep",
    )(proj, proj, proj, dt_raw, conv_w, conv_b, dt_bias)


def _ssd_constants():
    q = SSD_CHUNK
    li = np.arange(q)
    tri = np.stack([li[:, None] >= li[None, :], li[:, None] <= li[None, :]]).astype(np.float32)
    hp = np.arange(SSD_WIDTH) // SSD_HEAD_DIM
    expand = np.zeros((2, 2 * DT_PAD, SSD_WIDTH), np.float32)
    for d in range(2):
        expand[d, d * SSD_HEADS + hp, np.arange(SSD_WIDTH)] = 1.0
        expand[d, DT_PAD + d * SSD_HEADS + hp, np.arange(SSD_WIDTH)] = 1.0
    return jnp.asarray(tri, F32), jnp.asarray(expand, BF16)


def _expand_heads(v, e):
    hi = v.astype(BF16)
    lo = (v - hi.astype(F32)).astype(BF16)
    return jnp.dot(jnp.concatenate([hi, lo], axis=1), e, preferred_element_type=F32)


def _ssd_scan_kernel(xs_f, bc_f, dt_f, xs_b, bc_b, dt_b, alog_ref, tri_ref, exp_ref, yf_ref, yb_ref, h_scr):
    @pl.when(pl.program_id(1) == 0)
    def _():
        h_scr[...] = jnp.zeros(h_scr.shape, F32)

    q = SSD_CHUNK
    gw = SSD_WIDTH // SSD_GROUPS
    a_row = -jnp.exp(alog_ref[...])
    li = lax.broadcasted_iota(jnp.int32, (q, q), 0)
    si = lax.broadcasted_iota(jnp.int32, (q, q), 1)
    lane = lax.broadcasted_iota(jnp.int32, (q, HEAD_DIM), 1)
    left = lane < SSD_HEAD_DIM
    dirs = ((xs_f, bc_f, dt_f, yf_ref, q - 1), (xs_b, bc_b, dt_b, yb_ref, 0))
    for d, (xs_ref, bc_ref, dt_ref, y_ref, last) in enumerate(dirs):
        keep = (li >= si) if d == 0 else (li <= si)
        e = exp_ref[d]
        dt = dt_ref[...]
        da = dt * a_row
        cum = jnp.dot(tri_ref[d], da, precision=lax.Precision.HIGHEST, preferred_element_type=F32)
        cum_t = cum.T
        tot = cum[last:last + 1, :]
        dt_e = _expand_heads(dt, e)
        ec_e = _expand_heads(jnp.exp(cum), e)
        w_e = _expand_heads(jnp.exp(tot - cum), e)
        dec_e = ec_e[last:last + 1, :]
        xd = xs_ref[...].astype(F32) * dt_e
        xdb = xd.astype(BF16)
        xw = (xd * w_e).astype(BF16)
        for g in range(SSD_GROUPS):
            gl = slice(g * gw, (g + 1) * gw)
            bg = bc_ref[:, g * SSD_STATE:(g + 1) * SSD_STATE]
            cg = bc_ref[:, (SSD_GROUPS + g) * SSD_STATE:(SSD_GROUPS + g + 1) * SSD_STATE]
            cb = lax.dot_general(cg, bg, (((1,), (1,)), ((), ())), preferred_element_type=F32)
            bg_t = bg.astype(F32).T.astype(BF16)
            h_prev = h_scr[d, g]
            y_off = jnp.dot(cg, h_prev.astype(BF16), preferred_element_type=F32) * ec_e[:, gl]
            states_t = jnp.dot(bg_t, xw[:, gl], preferred_element_type=F32)
            h_scr[d, g] = h_prev * dec_e[:, gl] + states_t
            for j in range(gw // HEAD_DIM):
                mats = []
                for hh in range(2):
                    col = d * SSD_HEADS + g * (SSD_HEADS // SSD_GROUPS) + 2 * j + hh
                    seg = cum[:, col:col + 1] - cum_t[col:col + 1, :]
                    lm = jnp.exp(jnp.where(keep, seg, NEG))
                    mats.append((cb * lm).astype(BF16))
                lo = g * gw + j * HEAD_DIM
                slab = xdb[:, lo:lo + HEAD_DIM]
                zero = jnp.zeros_like(slab)
                wbd = jnp.concatenate([jnp.where(left, slab, zero), jnp.where(left, zero, slab)], axis=0)
                y_diag = jnp.dot(jnp.concatenate(mats, axis=1), wbd, preferred_element_type=F32)
                y_ref[:, lo:lo + HEAD_DIM] = (y_diag + y_off[:, j * HEAD_DIM:(j + 1) * HEAD_DIM]).astype(y_ref.dtype)


def _ssd_scan(xs, bc, dt, a_log, tri, expand, *, bsz, seq):
    m = xs.shape[0]
    nc = seq // SSD_CHUNK
    fwd = lambda b, c: (b * nc + c, 0)
    bwd = lambda b, c: (b * nc + nc - 1 - c, 0)
    return pl.pallas_call(
        _ssd_scan_kernel,
        grid=(bsz, nc),
        in_specs=[
            pl.BlockSpec((SSD_CHUNK, SSD_WIDTH), fwd),
            pl.BlockSpec((SSD_CHUNK, SSD_BC), fwd),
            pl.BlockSpec((SSD_CHUNK, DT_PAD), fwd),
            pl.BlockSpec((SSD_CHUNK, SSD_WIDTH), bwd),
            pl.BlockSpec((SSD_CHUNK, SSD_BC), bwd),
            pl.BlockSpec((SSD_CHUNK, DT_PAD), bwd),
            pl.BlockSpec((1, DT_PAD), lambda b, c: (0, 0)),
            pl.BlockSpec((2, SSD_CHUNK, SSD_CHUNK), lambda b, c: (0, 0, 0)),
            pl.BlockSpec((2, 2 * DT_PAD, SSD_WIDTH), lambda b, c: (0, 0, 0)),
        ],
        out_specs=[
            pl.BlockSpec((SSD_CHUNK, SSD_WIDTH), fwd),
            pl.BlockSpec((SSD_CHUNK, SSD_WIDTH), bwd),
        ],
        out_shape=[
            jax.ShapeDtypeStruct((m, SSD_WIDTH), BF16),
            jax.ShapeDtypeStruct((m, SSD_WIDTH), BF16),
        ],
        scratch_shapes=[pltpu.VMEM((2, SSD_GROUPS, SSD_STATE, SSD_WIDTH // SSD_GROUPS), F32)],
        compiler_params=_cp("parallel", "arbitrary"),
        name="ssd_scan",
    )(xs, bc, dt, xs, bc, dt, a_log, tri, expand)


def _ssd_finish_kernel(yf_ref, yb_ref, xs_ref, z_ref, d_ref, nw_ref, o_ref):
    y = yf_ref[...].astype(F32) + yb_ref[...].astype(F32) + xs_ref[...].astype(F32) * d_ref[...]
    y = y * _silu(z_ref[...].astype(F32))
    gw = SSD_WIDTH // SSD_GROUPS
    for g in range(SSD_GROUPS):
        yg = y[:, g * gw:(g + 1) * gw]
        ms = jnp.mean(yg * yg, axis=-1, keepdims=True)
        o_ref[:, g * gw:(g + 1) * gw] = (yg * lax.rsqrt(ms + EPS) * nw_ref[:, g * gw:(g + 1) * gw]).astype(o_ref.dtype)


def _ssd_finish(yf, yb, xs, proj, d_exp, norm_w, *, tm):
    m = yf.shape[0]
    row = lambda i: (i, 0)
    const = lambda i: (0, 0)
    return pl.pallas_call(
        _ssd_finish_kernel,
        grid=(m // tm,),
        in_specs=[
            pl.BlockSpec((tm, SSD_WIDTH), row),
            pl.BlockSpec((tm, SSD_WIDTH), row),
            pl.BlockSpec((tm, SSD_WIDTH), row),
            pl.BlockSpec((tm, SSD_WIDTH), lambda i: (i, OFF_Z // SSD_WIDTH)),
            pl.BlockSpec((1, SSD_WIDTH), const),
            pl.BlockSpec((1, SSD_WIDTH), const),
        ],
        out_specs=pl.BlockSpec((tm, SSD_WIDTH), row),
        out_shape=jax.ShapeDtypeStruct((m, SSD_WIDTH), BF16),
        compiler_params=_cp("parallel"),
        name="ssd_finish",
    )(yf, yb, xs, proj, d_exp, norm_w)


def _merge_kernel(ona_ref, ossd_ref, ogqa_ref, g_ref, x_ref, wna_ref, wssd_ref, wgqa_ref, wout_ref, o_ref, mg_scr):
    nc = 512
    branches = ((ona_ref, wna_ref), (ossd_ref, wssd_ref), (ogqa_ref, wgqa_ref))
    for c in range(D_MODEL // nc):
        cols = pl.ds(c * nc, nc)
        acc = None
        for b, (o_b, w_b) in enumerate(branches):
            p = jnp.dot(o_b[...], w_b[:, cols], preferred_element_type=F32)
            gate = _sigmoid(g_ref[:, pl.ds(b * D_MODEL + c * nc, nc)].astype(F32))
            acc = gate * p if acc is None else acc + gate * p
        mg_scr[:, cols] = acc.astype(BF16)
    for c in range(D_MODEL // nc):
        cols = pl.ds(c * nc, nc)
        o_ref[:, cols] = x_ref[:, cols] + jnp.dot(mg_scr[...], wout_ref[:, cols], preferred_element_type=F32)


def _merge(o_na, o_ssd, o_gqa, proj, x, w_na, w_ssd, w_gqa, w_out, *, tm):
    m = x.shape[0]
    row = lambda i: (i, 0)
    const = lambda i: (0, 0)
    one = pl.Buffered(1)
    return pl.pallas_call(
        _merge_kernel,
        grid=(m // tm,),
        in_specs=[
            pl.BlockSpec((tm, NA_WIDTH), row),
            pl.BlockSpec((tm, SSD_WIDTH), row),
            pl.BlockSpec((tm, GQA_WIDTH), row),
            pl.BlockSpec((tm, GATES), row),
            pl.BlockSpec((tm, D_MODEL), row),
            pl.BlockSpec((NA_WIDTH, D_MODEL), const, pipeline_mode=one),
            pl.BlockSpec((SSD_WIDTH, D_MODEL), const, pipeline_mode=one),
            pl.BlockSpec((GQA_WIDTH, D_MODEL), const, pipeline_mode=one),
            pl.BlockSpec((D_MODEL, D_MODEL), const, pipeline_mode=one),
        ],
        out_specs=pl.BlockSpec((tm, D_MODEL), row),
        out_shape=jax.ShapeDtypeStruct((m, D_MODEL), F32),
        scratch_shapes=[pltpu.VMEM((tm, D_MODEL), BF16)],
        compiler_params=_cp("parallel"),
        name="merge",
    )(o_na, o_ssd, o_gqa, proj, x, w_na, w_ssd, w_gqa, w_out)


FFN_EPI_COLS = 512


def _ffn_down_kernel(ug_ref, prev_ref, next_ref, uv_ref, cw_ref, cb_ref, wd_ref, x_ref, o_ref, hid_scr, *, tps, tm):
    i = pl.program_id(0)
    first = (i % tps) == 0
    last = (i % tps) == tps - 1
    half = tm // 2
    for r in range(2):
        rows = pl.ds(r * half, half)
        for c in range(D_FF // FFN_EPI_COLS):
            cols = pl.ds(c * FFN_EPI_COLS, FFN_EPI_COLS)
            g = ug_ref[rows, cols].astype(F32)
            if r == 0:
                prev = jnp.where(first, 0.0, prev_ref[:, cols].astype(F32))
                nxt = ug_ref[pl.ds(half, HALO), cols].astype(F32)
            else:
                prev = ug_ref[pl.ds(half - HALO, HALO), cols].astype(F32)
                nxt = jnp.where(last, 0.0, next_ref[:, cols].astype(F32))
            a = _dwconv(g, prev, nxt, cw_ref[:, cols], cb_ref[:, cols], 3)
            hid_scr[rows, cols] = (_silu(a) * uv_ref[rows, cols].astype(F32)).astype(BF16)
        o_ref[rows, :] = x_ref[rows, :] + jnp.dot(hid_scr[rows, :], wd_ref[...], preferred_element_type=F32)


def _ffn_down(u, conv_w, conv_b, w_down, x, *, seq, tm):
    m = x.shape[0]
    nblk = m // HALO
    per = tm // HALO
    const = lambda i: (0, 0)
    return pl.pallas_call(
        functools.partial(_ffn_down_kernel, tps=seq // tm, tm=tm),
        grid=(m // tm,),
        in_specs=[
            pl.BlockSpec((tm, D_FF), lambda i: (i, 0)),
            pl.BlockSpec((HALO, D_FF), lambda i: (jnp.maximum(i * per - 1, 0), 0)),
            pl.BlockSpec((HALO, D_FF), lambda i: (jnp.minimum((i + 1) * per, nblk - 1), 0)),
            pl.BlockSpec((tm, D_FF), lambda i: (i, 1)),
            pl.BlockSpec((3, D_FF), const),
            pl.BlockSpec((1, D_FF), const),
            pl.BlockSpec((D_FF, D_MODEL), const, pipeline_mode=pl.Buffered(1)),
            pl.BlockSpec((tm, D_MODEL), lambda i: (i, 0)),
        ],
        out_specs=pl.BlockSpec((tm, D_MODEL), lambda i: (i, 0)),
        out_shape=jax.ShapeDtypeStruct((m, D_MODEL), F32),
        scratch_shapes=[pltpu.VMEM((tm, D_FF), BF16)],
        compiler_params=_cp("parallel"),
        name="ffn_down",
    )(u, u, u, u, conv_w, conv_b, w_down, x)


def _tiles(seq):
    return dict(tm_mm=min(seq, 1024), tm_ew=min(seq, 512), tm_merge=min(seq, 512), tm_down=min(seq, 256),
                tq=min(seq, 512), tk=min(seq, 1024))


def _prep_layer_params(p, na_bias):
    (norm1_w, w_in, na_q_norm, na_k_norm, _, ssd_conv_w, ssd_conv_b, ssd_dt_bias, ssd_a_log, ssd_d,
     ssd_norm_w, gqa_q_norm, gqa_k_norm, w_branch, w_out, norm2_w, ffn_w_up, ffn_conv_w, ffn_conv_b, ffn_w_down) = p
    sizes = (3 * NA_WIDTH, GQA_WIDTH, GQA_KV_WIDTH, GQA_KV_WIDTH, SSD_WIDTH, SSD_XBC, 2 * SSD_HEADS, GATES)
    offs = np.concatenate([[0], np.cumsum(sizes)])
    piece = lambda k: w_in[:, offs[k]:offs[k + 1]]
    na, gq, gk, gv, z, xbc, dtw, gates = (piece(k) for k in range(8))
    w_main = jnp.concatenate([gates, xbc, na, z, gq, gk, gv], axis=1).astype(BF16)
    w_dt = jnp.pad(dtw, ((0, 0), (0, DT_PAD - 2 * SSD_HEADS))).astype(BF16)
    pad_dt = lambda v: jnp.pad(v.reshape(1, 2 * SSD_HEADS), ((0, 0), (0, DT_PAD - 2 * SSD_HEADS)))
    row = lambda v: v.reshape(1, -1)
    wb = w_branch.astype(BF16)
    return dict(
        norm1_w=row(norm1_w), w_main=w_main, w_dt=w_dt,
        na_q_norm=row(na_q_norm), na_k_norm=row(na_k_norm), na_bias=na_bias,
        conv_w=ssd_conv_w, conv_b=row(ssd_conv_b), dt_bias=pad_dt(ssd_dt_bias), a_log=pad_dt(ssd_a_log),
        d_exp=row(jnp.repeat(ssd_d, SSD_HEAD_DIM)), ssd_norm_w=row(ssd_norm_w),
        gqa_q_norm=row(gqa_q_norm), gqa_k_norm=row(gqa_k_norm),
        w_na=wb[:NA_WIDTH], w_ssd=wb[NA_WIDTH:NA_WIDTH + SSD_WIDTH], w_gqa=wb[NA_WIDTH + SSD_WIDTH:],
        w_out=w_out.astype(BF16), norm2_w=row(norm2_w), w_up=ffn_w_up.astype(BF16),
        ffn_conv_w=ffn_conv_w, ffn_conv_b=row(ffn_conv_b), w_down=ffn_w_down.astype(BF16),
    )


def _prep_params(params):
    depth = params[0].shape[0]
    na_bias = _na_bias_tables(params[4])
    return [_prep_layer_params([p[i] for p in params], na_bias[i]) for i in range(depth)]


def _layer(x, lp, consts, *, bsz, seq):
    t = _tiles(seq)
    cos_t, sin_t, tri, expand = consts
    proj, dt_raw = _norm_matmul(x, lp["norm1_w"], lp["w_main"], lp["w_dt"], tm=t["tm_mm"], tn=1024)
    o_na = _na_attention(proj, lp["na_bias"], lp["na_q_norm"], lp["na_k_norm"], bsz=bsz, seq=seq)
    qn, kn, vaug = _gqa_prep(proj, cos_t, sin_t, lp["gqa_q_norm"], lp["gqa_k_norm"], seq=seq, tm=t["tm_ew"])
    o_gqa = _gqa_attention(qn, kn, vaug, bsz=bsz, seq=seq, tq=t["tq"], tk=t["tk"])
    xs, bc, dt = _ssd_prep(proj, dt_raw, lp["conv_w"], lp["conv_b"], lp["dt_bias"], seq=seq, tm=t["tm_ew"])
    yf, yb = _ssd_scan(xs, bc, dt, lp["a_log"], tri, expand, bsz=bsz, seq=seq)
    o_ssd = _ssd_finish(yf, yb, xs, proj, lp["d_exp"], lp["ssd_norm_w"], tm=t["tm_ew"])
    x = _merge(o_na, o_ssd, o_gqa, proj, x, lp["w_na"], lp["w_ssd"], lp["w_gqa"], lp["w_out"], tm=t["tm_merge"])
    u = _norm_matmul(x, lp["norm2_w"], lp["w_up"], tm=t["tm_mm"], tn=1024)
    return _ffn_down(u, lp["ffn_conv_w"], lp["ffn_conv_b"], lp["w_down"], x, seq=seq, tm=t["tm_down"])


def _trunk(x, layer_params):
    bsz, seq, d = x.shape
    assert d == D_MODEL and seq % 1024 == 0 and seq // GRID_W >= NA_KROWS
    consts = _rope_tables(seq) + _ssd_constants()
    h = x.reshape(bsz * seq, d)
    for lp in layer_params:
        h = _layer(h, lp, consts, bsz=bsz, seq=seq)
    return h.reshape(bsz, seq, d)


def kernel(x_prompt, x_sample, norm1_w, w_in, na_q_norm, na_k_norm, na_rpb, ssd_conv_w, ssd_conv_b, ssd_dt_bias, ssd_a_log, ssd_d, ssd_norm_w, gqa_q_norm, gqa_k_norm, w_branch, w_out, norm2_w, ffn_w_up, ffn_conv_w, ffn_conv_b, ffn_w_down):
    params = (norm1_w, w_in, na_q_norm, na_k_norm, na_rpb, ssd_conv_w, ssd_conv_b, ssd_dt_bias, ssd_a_log,
              ssd_d, ssd_norm_w, gqa_q_norm, gqa_k_norm, w_branch, w_out, norm2_w, ffn_w_up, ffn_conv_w,
              ffn_conv_b, ffn_w_down)
    layer_params = _prep_params(params)
    return (_trunk(x_prompt, layer_params), _trunk(x_sample, layer_params))
```

```python
import functools
import math

import numpy as np
import jax
import jax.numpy as jnp
from jax import lax
from jax.experimental import pallas as pl
from jax.experimental.pallas import tpu as pltpu

F32 = jnp.float32
BF16 = jnp.bfloat16

D_MODEL = 2048
GRID_W = 64
EPS = 1e-6
NA_HEADS = 4
HEAD_DIM = 128
NA_WIDTH = NA_HEADS * HEAD_DIM
NA_WIN_H = 8
NA_WIN_W = 16
SSD_HEADS = 16
SSD_HEAD_DIM = 64
SSD_WIDTH = SSD_HEADS * SSD_HEAD_DIM
SSD_STATE = 128
SSD_GROUPS = 2
SSD_CONV = 5
SSD_CHUNK = 128
SSD_BC = 2 * SSD_GROUPS * SSD_STATE
SSD_XBC = SSD_WIDTH + SSD_BC
GQA_HEADS = 4
GQA_KV_HEADS = 2
GQA_WIDTH = GQA_HEADS * HEAD_DIM
GQA_KV_WIDTH = GQA_KV_HEADS * HEAD_DIM
ROPE_THETA = 10000.0
D_FF = 5632
N_BRANCH = 3
GATES = N_BRANCH * D_MODEL
DT_PAD = 128

OFF_GATES = 0
OFF_XBC = OFF_GATES + GATES
OFF_NA = OFF_XBC + SSD_XBC
OFF_Z = OFF_NA + 3 * NA_WIDTH
OFF_GQ = OFF_Z + SSD_WIDTH
OFF_GK = OFF_GQ + GQA_WIDTH
OFF_GV = OFF_GK + GQA_KV_WIDTH
PROJ_COLS = OFF_GV + GQA_KV_WIDTH

NEG = -1e30
LOG2E = math.log2(math.e)
HALO = 8
VMEM_LIMIT = 56 * 2**20


def _cp(*sem, flags=None):
    return pltpu.CompilerParams(dimension_semantics=sem, vmem_limit_bytes=VMEM_LIMIT, flags=flags)


def _sigmoid(x):
    return 1.0 / (1.0 + jnp.exp(-x))


def _silu(x):
    return x * _sigmoid(x)


def _norm_mm_kernel(*refs, tm, with_dt):
    if with_dt:
        x_ref, nw_ref, w_ref, wdt_ref, o_ref, odt_ref, h_scr = refs
    else:
        x_ref, nw_ref, w_ref, o_ref, h_scr = refs
    rc = 128

    @pl.when(pl.program_id(1) == 0)
    def _():
        def body(r, c):
            rows = pl.ds(pl.multiple_of(r * rc, rc), rc)
            x = x_ref[rows, :]
            ms = jnp.mean(x * x, axis=-1, keepdims=True)
            h_scr[rows, :] = (x * lax.rsqrt(ms + EPS) * nw_ref[...]).astype(BF16)
            return c

        lax.fori_loop(0, tm // rc, body, 0)
        if with_dt:
            odt_ref[...] = jnp.dot(h_scr[...], wdt_ref[...], preferred_element_type=F32)

    o_ref[...] = jnp.dot(h_scr[...], w_ref[...], preferred_element_type=F32).astype(o_ref.dtype)


def _norm_matmul(x, nw, w, wdt=None, *, tm, tn):
    m, d = x.shape
    n = w.shape[1]
    with_dt = wdt is not None
    in_specs = [
        pl.BlockSpec((tm, d), lambda i, j: (i, 0)),
        pl.BlockSpec((1, d), lambda i, j: (0, 0)),
        pl.BlockSpec((d, tn), lambda i, j: (0, j)),
    ]
    out_shape = [jax.ShapeDtypeStruct((m, n), BF16)]
    out_specs = [pl.BlockSpec((tm, tn), lambda i, j: (i, j))]
    args = [x, nw, w]
    if with_dt:
        in_specs.append(pl.BlockSpec((d, DT_PAD), lambda i, j: (0, 0)))
        out_shape.append(jax.ShapeDtypeStruct((m, DT_PAD), F32))
        out_specs.append(pl.BlockSpec((tm, DT_PAD), lambda i, j: (i, 0)))
        args.append(wdt)
    res = pl.pallas_call(
        functools.partial(_norm_mm_kernel, tm=tm, with_dt=with_dt),
        grid=(m // tm, n // tn),
        in_specs=in_specs,
        out_specs=out_specs,
        out_shape=out_shape,
        scratch_shapes=[pltpu.VMEM((tm, d), BF16)],
        compiler_params=_cp("parallel", "arbitrary"),
        name="norm_matmul_dt" if with_dt else "norm_matmul",
    )(*args)
    return res if with_dt else res[0]


NA_QROWS = 8
NA_KROWS = 16
NA_TQ = NA_QROWS * GRID_W
NA_TK = NA_KROWS * GRID_W
NA_KPIECE = 256
NA_NPIECE = NA_TK // NA_KPIECE


def _na_bias_tables(rpb):
    rows = 4 * NA_QROWS
    nb = rows // NA_QROWS
    depth = rpb.shape[0]
    c = np.arange(GRID_W)
    col_start = np.clip(c - NA_WIN_W // 2, 0, GRID_W - NA_WIN_W)
    in_win = (c[None, :] >= col_start[:, None]) & (c[None, :] < col_start[:, None] + NA_WIN_W)
    dc = np.clip(c[None, :] - c[:, None], -(NA_WIN_W - 1), NA_WIN_W - 1) + NA_WIN_W - 1
    onehot_c = (dc[:, :, None] == np.arange(2 * NA_WIN_W - 1)).astype(np.float32)
    by_col = jnp.einsum("lhab,qcb->lhaqc", rpb.astype(F32), jnp.asarray(onehot_c), precision=lax.Precision.HIGHEST)
    tables = []
    for b in (0, 1, nb - 1):
        r = NA_QROWS * b + np.arange(NA_QROWS)
        ks = int(np.clip(NA_QROWS * b - NA_WIN_H // 2, 0, rows - NA_KROWS))
        krow = ks + np.arange(NA_KROWS)
        rs = np.clip(r - NA_WIN_H // 2, 0, rows - NA_WIN_H)
        valid = (krow[None, :] >= rs[:, None]) & (krow[None, :] < rs[:, None] + NA_WIN_H)
        dr = np.clip(krow[None, :] - r[:, None], -(NA_WIN_H - 1), NA_WIN_H - 1) + NA_WIN_H - 1
        bias = jnp.take(by_col, jnp.asarray(dr.reshape(-1)), axis=2)
        bias = bias.reshape(depth, NA_HEADS, NA_QROWS, NA_KROWS, GRID_W, GRID_W).transpose(0, 1, 2, 4, 3, 5)
        mask = valid[:, None, :, None] & in_win[None, :, None, :]
        bias = jnp.where(jnp.asarray(mask)[None, None], bias * LOG2E, NEG)
        tables.append(bias.reshape(depth, NA_HEADS, NA_TQ, NA_TK))
    return jnp.stack(tables, axis=1)


def _head_rmsnorm(x, w):
    ms = jnp.mean(x * x, axis=-1, keepdims=True)
    return x * lax.rsqrt(ms + EPS) * w


NA_SUB = 128


def _na_kernel(q_ref, k0, k1, k2, k3, v0, v1, v2, v3, bias_ref, qw_ref, kw_ref, o_ref, kn_scr, va_scr, s_scr, *, nb):
    i = pl.program_id(2)
    kind = jnp.where(i == 0, 0, jnp.where(i == nb - 1, 2, 1))
    scale = HEAD_DIM ** -0.5 * LOG2E
    ones = jnp.ones((NA_KPIECE, HEAD_DIM), BF16)
    for t, (k_ref, v_ref) in enumerate(zip((k0, k1, k2, k3), (v0, v1, v2, v3))):
        piece = pl.ds(t * NA_KPIECE, NA_KPIECE)
        kn_scr[piece, :] = _head_rmsnorm(k_ref[...].astype(F32), kw_ref[...]).astype(BF16)
        va_scr[piece, :HEAD_DIM] = v_ref[...]
        va_scr[piece, HEAD_DIM:] = ones
    nlane = NA_TK // HEAD_DIM

    def scores(qs):
        rows = pl.ds(qs * NA_SUB, NA_SUB)
        qn = (_head_rmsnorm(q_ref[rows, :].astype(F32), qw_ref[...]) * scale).astype(BF16)
        s = lax.dot_general(qn, kn_scr[...], (((1,), (1,)), ((), ())), preferred_element_type=F32)
        return s + bias_ref[kind, rows, :]

    def finish(qs, slot):
        blocks = [s_scr[slot, :, j * HEAD_DIM:(j + 1) * HEAD_DIM] for j in range(nlane)]
        part = blocks[0]
        for blk in blocks[1:]:
            part = jnp.maximum(part, blk)
        m = jnp.broadcast_to(jnp.max(part, axis=-1, keepdims=True), part.shape)
        p = jnp.concatenate([jnp.exp2(blk - m).astype(BF16) for blk in blocks], axis=1)
        pv = jnp.dot(p, va_scr[...], preferred_element_type=F32)
        o_ref[pl.ds(qs * NA_SUB, NA_SUB), :] = (pv[:, :HEAD_DIM] / pv[:, HEAD_DIM:]).astype(o_ref.dtype)

    nsub = NA_TQ // NA_SUB
    s_scr[0] = scores(0)
    for qs in range(nsub):
        if qs + 1 < nsub:
            s_scr[(qs + 1) % 2] = scores(qs + 1)
        finish(qs, qs % 2)


def _na_attention(proj, bias, qw, kw, *, bsz, seq):
    m = proj.shape[0]
    rows = seq // GRID_W
    nb = rows // NA_QROWS
    npieces = seq // NA_KPIECE
    q_col = OFF_NA // HEAD_DIM
    k_col = (OFF_NA + NA_WIDTH) // HEAD_DIM
    v_col = (OFF_NA + 2 * NA_WIDTH) // HEAD_DIM

    def kbase(i):
        return jnp.clip(2 * i - 1, 0, npieces - NA_NPIECE)

    def kv_spec(col, t):
        return pl.BlockSpec((NA_KPIECE, HEAD_DIM), lambda h, b, i: (b * npieces + kbase(i) + t, col + h))

    in_specs = [pl.BlockSpec((NA_TQ, HEAD_DIM), lambda h, b, i: (b * nb + i, q_col + h))]
    in_specs += [kv_spec(k_col, t) for t in range(NA_NPIECE)]
    in_specs += [kv_spec(v_col, t) for t in range(NA_NPIECE)]
    in_specs += [
        pl.BlockSpec((3, None, NA_TQ, NA_TK), lambda h, b, i: (0, h, 0, 0)),
        pl.BlockSpec((1, HEAD_DIM), lambda h, b, i: (0, 0)),
        pl.BlockSpec((1, HEAD_DIM), lambda h, b, i: (0, 0)),
    ]
    return pl.pallas_call(
        functools.partial(_na_kernel, nb=nb),
        grid=(NA_HEADS, bsz, nb),
        in_specs=in_specs,
        out_specs=pl.BlockSpec((NA_TQ, HEAD_DIM), lambda h, b, i: (b * nb + i, h)),
        out_shape=jax.ShapeDtypeStruct((m, NA_WIDTH), BF16),
        scratch_shapes=[
            pltpu.VMEM((NA_TK, HEAD_DIM), BF16),
            pltpu.VMEM((NA_TK, 2 * HEAD_DIM), BF16),
            pltpu.VMEM((2, NA_SUB, NA_TK), F32),
        ],
        compiler_params=_cp("parallel", "parallel", "parallel"),
        name="na_attention",
    )(proj, *([proj] * (2 * NA_NPIECE)), bias, qw, kw)


def _rope_tables(seq):
    t = np.arange(seq)
    pos = np.stack([t // GRID_W, t % GRID_W], axis=-1).astype(np.float32)
    n_freq = HEAD_DIM // 4
    inv_freq = jnp.asarray(ROPE_THETA, F32) ** (-jnp.arange(n_freq, dtype=F32) / n_freq)
    ang = jnp.asarray(pos)[:, :, None] * inv_freq
    cos = jnp.cos(ang)
    sin = jnp.sin(ang)
    cos_t = jnp.concatenate([cos[:, 0], cos[:, 0], cos[:, 1], cos[:, 1]], axis=-1)
    sin_t = jnp.concatenate([-sin[:, 0], sin[:, 0], -sin[:, 1], sin[:, 1]], axis=-1)
    return cos_t, sin_t


def _rope(x, cos_t, sin_t, first_half):
    swapped = jnp.where(first_half, pltpu.roll(x, HEAD_DIM - 32, axis=1), pltpu.roll(x, 32, axis=1))
    return x * cos_t + swapped * sin_t


def _gqa_prep_kernel(q_ref, k_ref, v_ref, cos_ref, sin_ref, qw_ref, kw_ref, qo_ref, ko_ref, vo_ref):
    cos_t = cos_ref[...]
    sin_t = sin_ref[...]
    lane = lax.broadcasted_iota(jnp.int32, cos_t.shape, 1)
    first_half = (lane % 64) < 32
    scale = HEAD_DIM ** -0.5 * math.log2(math.e)
    for h in range(GQA_HEADS):
        cols = pl.ds(h * HEAD_DIM, HEAD_DIM)
        x = _head_rmsnorm(q_ref[:, cols].astype(F32), qw_ref[...])
        qo_ref[:, cols] = (_rope(x, cos_t, sin_t, first_half) * scale).astype(BF16)
    ones = jnp.ones(cos_t.shape, BF16)
    for h in range(GQA_KV_HEADS):
        cols = pl.ds(h * HEAD_DIM, HEAD_DIM)
        x = _head_rmsnorm(k_ref[:, cols].astype(F32), kw_ref[...])
        ko_ref[:, cols] = _rope(x, cos_t, sin_t, first_half).astype(BF16)
        vo_ref[:, pl.ds(2 * h * HEAD_DIM, HEAD_DIM)] = v_ref[:, cols]
        vo_ref[:, pl.ds((2 * h + 1) * HEAD_DIM, HEAD_DIM)] = ones


def _gqa_prep(proj, cos_t, sin_t, qw, kw, *, seq, tm):
    m = proj.shape[0]
    tps = seq // tm
    return pl.pallas_call(
        _gqa_prep_kernel,
        grid=(m // tm,),
        in_specs=[
            pl.BlockSpec((tm, GQA_WIDTH), lambda i: (i, OFF_GQ // GQA_WIDTH)),
            pl.BlockSpec((tm, GQA_KV_WIDTH), lambda i: (i, OFF_GK // GQA_KV_WIDTH)),
            pl.BlockSpec((tm, GQA_KV_WIDTH), lambda i: (i, OFF_GV // GQA_KV_WIDTH)),
            pl.BlockSpec((tm, HEAD_DIM), lambda i: (i % tps, 0)),
            pl.BlockSpec((tm, HEAD_DIM), lambda i: (i % tps, 0)),
            pl.BlockSpec((1, HEAD_DIM), lambda i: (0, 0)),
            pl.BlockSpec((1, HEAD_DIM), lambda i: (0, 0)),
        ],
        out_specs=[
            pl.BlockSpec((tm, GQA_WIDTH), lambda i: (i, 0)),
            pl.BlockSpec((tm, GQA_KV_WIDTH), lambda i: (i, 0)),
            pl.BlockSpec((tm, 2 * GQA_KV_WIDTH), lambda i: (i, 0)),
        ],
        out_shape=[
            jax.ShapeDtypeStruct((m, GQA_WIDTH), BF16),
            jax.ShapeDtypeStruct((m, GQA_KV_WIDTH), BF16),
            jax.ShapeDtypeStruct((m, 2 * GQA_KV_WIDTH), BF16),
        ],
        compiler_params=_cp("parallel"),
        name="gqa_prep",
    )(proj, proj, proj, cos_t, sin_t, qw, kw)


def _flash_kernel(q_ref, k_ref, v_ref, o_ref, s_scr, m_scr, acc_scr, *, tq, tk, seq):
    q2 = jnp.concatenate([q_ref[:, :HEAD_DIM], q_ref[:, HEAD_DIM:]], axis=0)
    m_scr[...] = jnp.full(m_scr.shape, NEG, F32)
    acc_scr[...] = jnp.zeros(acc_scr.shape, F32)
    nblk = tk // HEAD_DIM
    n = seq // tk

    def chunk(kc):
        return pl.ds(pl.multiple_of(kc * tk, tk), tk)

    def scores(kc):
        return lax.dot_general(q2, k_ref[chunk(kc), :], (((1,), (1,)), ((), ())), preferred_element_type=F32)

    def softmax_pv(kc, slot):
        blocks = [s_scr[slot, :, j * HEAD_DIM:(j + 1) * HEAD_DIM] for j in range(nblk)]
        part = blocks[0]
        for blk in blocks[1:]:
            part = jnp.maximum(part, blk)
        m_prev = m_scr[...]
        m_new = jnp.maximum(m_prev, jnp.max(part, axis=-1, keepdims=True))
        alpha = jnp.exp2(m_prev - m_new)
        p = jnp.concatenate([jnp.exp2(blk - m_new).astype(BF16) for blk in blocks], axis=1)
        pv = jnp.dot(p, v_ref[chunk(kc), :], preferred_element_type=F32)
        acc_scr[...] = jnp.concatenate([alpha, alpha], axis=1) * acc_scr[...] + pv
        m_scr[...] = m_new

    s_scr[0] = scores(0)

    def pair(t, c):
        s_scr[1] = scores(2 * t + 1)
        softmax_pv(2 * t, 0)
        s_scr[0] = scores(2 * t + 2)
        softmax_pv(2 * t + 1, 1)
        return c

    npairs = (n - 1) // 2
    lax.fori_loop(0, npairs, pair, 0)
    base = 2 * npairs
    if n - base == 2:
        s_scr[1] = scores(base + 1)
        softmax_pv(base, 0)
        softmax_pv(base + 1, 1)
    else:
        softmax_pv(base, 0)
    o = acc_scr[:, :HEAD_DIM] / acc_scr[:, HEAD_DIM:]
    o_ref[:, :HEAD_DIM] = o[:tq].astype(o_ref.dtype)
    o_ref[:, HEAD_DIM:] = o[tq:].astype(o_ref.dtype)


def _gqa_attention(qn, kn, vaug, *, bsz, seq, tq, tk):
    m = qn.shape[0]
    nq = seq // tq
    rep = GQA_HEADS // GQA_KV_HEADS
    return pl.pallas_call(
        functools.partial(_flash_kernel, tq=tq, tk=tk, seq=seq),
        grid=(bsz, GQA_KV_HEADS, nq),
        in_specs=[
            pl.BlockSpec((tq, rep * HEAD_DIM), lambda b, g, i: (b * nq + i, g)),
            pl.BlockSpec((seq, HEAD_DIM), lambda b, g, i: (b, g)),
            pl.BlockSpec((seq, 2 * HEAD_DIM), lambda b, g, i: (b, g)),
        ],
        out_specs=pl.BlockSpec((tq, rep * HEAD_DIM), lambda b, g, i: (b * nq + i, g)),
        out_shape=jax.ShapeDtypeStruct((m, GQA_WIDTH), BF16),
        scratch_shapes=[
            pltpu.VMEM((2, rep * tq, tk), F32),
            pltpu.VMEM((rep * tq, HEAD_DIM), F32),
            pltpu.VMEM((rep * tq, 2 * HEAD_DIM), F32),
        ],
        compiler_params=_cp("parallel", "parallel", "parallel"),
        name="gqa_flash",
    )(qn, kn, vaug)


def _with_halo(x, prev, nxt):
    return jnp.concatenate([x, nxt, prev], axis=0)


def _shift_rows(xcat, s, n):
    return pltpu.roll(xcat, s % xcat.shape[0], axis=0)[:n]


def _dwconv(x, prev, nxt, w, b, width):
    n = x.shape[0]
    xcat = _with_halo(x, prev, nxt)
    pad = width // 2
    out = b + x * w[pad:pad + 1, :]
    for i in range(width):
        if i != pad:
            out = out + _shift_rows(xcat, pad - i, n) * w[i:i + 1, :]
    return out


def _halo_specs(tm, width, col_block, m):
    nblk = m // HALO
    per = tm // HALO
    prev = pl.BlockSpec((HALO, width), lambda i, *_: (jnp.maximum(i * per - 1, 0), col_block))
    nxt = pl.BlockSpec((HALO, width), lambda i, *_: (jnp.minimum((i + 1) * per, nblk - 1), col_block))
    return prev, nxt


def _ssd_prep_kernel(x_ref, prev_ref, next_ref, dtr_ref, cw_ref, cb_ref, dtb_ref, xs_ref, bc_ref, dt_ref, *, tps):
    i = pl.program_id(0)
    first = (i % tps) == 0
    last = (i % tps) == tps - 1
    x = x_ref[...].astype(F32)
    prev = jnp.where(first, 0.0, prev_ref[...].astype(F32))
    nxt = jnp.where(last, 0.0, next_ref[...].astype(F32))
    y = _silu(_dwconv(x, prev, nxt, cw_ref[...], cb_ref[...], SSD_CONV))
    xs_ref[...] = y[:, :SSD_WIDTH].astype(BF16)
    bc_ref[...] = y[:, SSD_WIDTH:].astype(BF16)
    v = dtr_ref[...] + dtb_ref[...]
    dt_ref[...] = jnp.maximum(v, 0.0) + jnp.log(1.0 + jnp.exp(-jnp.abs(v)))


def _ssd_prep(proj, dt_raw, conv_w, conv_b, dt_bias, *, seq, tm):
    m = proj.shape[0]
    cb = OFF_XBC // SSD_XBC
    prev, nxt = _halo_specs(tm, SSD_XBC, cb, m)
    return pl.pallas_call(
        functools.partial(_ssd_prep_kernel, tps=seq // tm),
        grid=(m // tm,),
        in_specs=[
            pl.BlockSpec((tm, SSD_XBC), lambda i: (i, cb)),
            prev,
            nxt,
            pl.BlockSpec((tm, DT_PAD), lambda i: (i, 0)),
            pl.BlockSpec((SSD_CONV, SSD_XBC), lambda i: (0, 0)),
            pl.BlockSpec((1, SSD_XBC), lambda i: (0, 0)),
            pl.BlockSpec((1, DT_PAD), lambda i: (0, 0)),
        ],
        out_specs=[
            pl.BlockSpec((tm, SSD_WIDTH), lambda i: (i, 0)),
            pl.BlockSpec((tm, SSD_BC), lambda i: (i, 0)),
            pl.BlockSpec((tm, DT_PAD), lambda i: (i, 0)),
        ],
        out_shape=[
            jax.ShapeDtypeStruct((m, SSD_WIDTH), BF16),
            jax.ShapeDtypeStruct((m, SSD_BC), BF16),
            jax.ShapeDtypeStruct((m, DT_PAD), F32),
        ],
        compiler_params=_cp("parallel"),
        name="ssd_prep",
    )(proj, proj, proj, dt_raw, conv_w, conv_b, dt_bias)


def _ssd_constants():
    q = SSD_CHUNK
    li = np.arange(q)
    tri = np.stack([li[:, None] >= li[None, :], li[:, None] <= li[None, :]]).astype(np.float32)
    hp = np.arange(SSD_WIDTH) // SSD_HEAD_DIM
    expand = np.zeros((2, 2 * DT_PAD, SSD_WIDTH), np.float32)
    for d in range(2):
        expand[d, d * SSD_HEADS + hp, np.arange(SSD_WIDTH)] = 1.0
        expand[d, DT_PAD + d * SSD_HEADS + hp, np.arange(SSD_WIDTH)] = 1.0
    return jnp.asarray(tri, F32), jnp.asarray(expand, BF16)


def _expand_heads(v, e):
    hi = v.astype(BF16)
    lo = (v - hi.astype(F32)).astype(BF16)
    return jnp.dot(jnp.concatenate([hi, lo], axis=1), e, preferred_element_type=F32)


def _ssd_scan_kernel(xs_f, bc_f, dt_f, xs_b, bc_b, dt_b, alog_ref, tri_ref, exp_ref, yf_ref, yb_ref, h_scr):
    @pl.when(pl.program_id(1) == 0)
    def _():
        h_scr[...] = jnp.zeros(h_scr.shape, F32)

    q = SSD_CHUNK
    gw = SSD_WIDTH // SSD_GROUPS
    a_row = -jnp.exp(alog_ref[...])
    li = lax.broadcasted_iota(jnp.int32, (q, q), 0)
    si = lax.broadcasted_iota(jnp.int32, (q, q), 1)
    lane = lax.broadcasted_iota(jnp.int32, (q, HEAD_DIM), 1)
    left = lane < SSD_HEAD_DIM
    dirs = ((xs_f, bc_f, dt_f, yf_ref, q - 1), (xs_b, bc_b, dt_b, yb_ref, 0))
    for d, (xs_ref, bc_ref, dt_ref, y_ref, last) in enumerate(dirs):
        keep = (li >= si) if d == 0 else (li <= si)
        e = exp_ref[d]
        dt = dt_ref[...]
        da = dt * a_row
        cum = jnp.dot(tri_ref[d], da, precision=lax.Precision.HIGHEST, preferred_element_type=F32)
        cum_t = cum.T
        tot = cum[last:last + 1, :]
        stack_e = _expand_heads(jnp.concatenate([dt, jnp.exp(cum), jnp.exp(tot - cum)], axis=0), e)
        dt_e, ec_e, w_e = stack_e[:q], stack_e[q:2 * q], stack_e[2 * q:]
        dec_e = ec_e[last:last + 1, :]
        xd = xs_ref[...].astype(F32) * dt_e
        xdb = xd.astype(BF16)
        xw = (xd * w_e).astype(BF16)
        for g in range(SSD_GROUPS):
            gl = slice(g * gw, (g + 1) * gw)
            bg = bc_ref[:, g * SSD_STATE:(g + 1) * SSD_STATE]
            cg = bc_ref[:, (SSD_GROUPS + g) * SSD_STATE:(SSD_GROUPS + g + 1) * SSD_STATE]
            cb = lax.dot_general(cg, bg, (((1,), (1,)), ((), ())), preferred_element_type=F32)
            bg_t = bg.astype(F32).T.astype(BF16)
            h_prev = h_scr[d, g]
            y_off = jnp.dot(cg, h_prev.astype(BF16), preferred_element_type=F32) * ec_e[:, gl]
            states_t = jnp.dot(bg_t, xw[:, gl], preferred_element_type=F32)
            h_scr[d, g] = h_prev * dec_e[:, gl] + states_t
            for j in range(gw // HEAD_DIM):
                mats = []
                for hh in range(2):
                    col = d * SSD_HEADS + g * (SSD_HEADS // SSD_GROUPS) + 2 * j + hh
                    seg = cum[:, col:col + 1] - cum_t[col:col + 1, :]
                    lm = jnp.exp(jnp.where(keep, seg, NEG))
                    mats.append((cb * lm).astype(BF16))
                lo = g * gw + j * HEAD_DIM
                slab = xdb[:, lo:lo + HEAD_DIM]
                zero = jnp.zeros_like(slab)
                wbd = jnp.concatenate([jnp.where(left, slab, zero), jnp.where(left, zero, slab)], axis=0)
                y_diag = jnp.dot(jnp.concatenate(mats, axis=1), wbd, preferred_element_type=F32)
                y_ref[:, lo:lo + HEAD_DIM] = (y_diag + y_off[:, j * HEAD_DIM:(j + 1) * HEAD_DIM]).astype(y_ref.dtype)


def _ssd_scan(xs, bc, dt, a_log, tri, expand, *, bsz, seq):
    m = xs.shape[0]
    nc = seq // SSD_CHUNK
    fwd = lambda b, c: (b * nc + c, 0)
    bwd = lambda b, c: (b * nc + nc - 1 - c, 0)
    return pl.pallas_call(
        _ssd_scan_kernel,
        grid=(bsz, nc),
        in_specs=[
            pl.BlockSpec((SSD_CHUNK, SSD_WIDTH), fwd),
            pl.BlockSpec((SSD_CHUNK, SSD_BC), fwd),
            pl.BlockSpec((SSD_CHUNK, DT_PAD), fwd),
            pl.BlockSpec((SSD_CHUNK, SSD_WIDTH), bwd),
            pl.BlockSpec((SSD_CHUNK, SSD_BC), bwd),
            pl.BlockSpec((SSD_CHUNK, DT_PAD), bwd),
            pl.BlockSpec((1, DT_PAD), lambda b, c: (0, 0)),
            pl.BlockSpec((2, SSD_CHUNK, SSD_CHUNK), lambda b, c: (0, 0, 0)),
            pl.BlockSpec((2, 2 * DT_PAD, SSD_WIDTH), lambda b, c: (0, 0, 0)),
        ],
        out_specs=[
            pl.BlockSpec((SSD_CHUNK, SSD_WIDTH), fwd),
            pl.BlockSpec((SSD_CHUNK, SSD_WIDTH), bwd),
        ],
        out_shape=[
            jax.ShapeDtypeStruct((m, SSD_WIDTH), BF16),
            jax.ShapeDtypeStruct((m, SSD_WIDTH), BF16),
        ],
        scratch_shapes=[pltpu.VMEM((2, SSD_GROUPS, SSD_STATE, SSD_WIDTH // SSD_GROUPS), F32)],
        compiler_params=_cp("parallel", "arbitrary"),
        name="ssd_scan",
    )(xs, bc, dt, xs, bc, dt, a_log, tri, expand)


def _ssd_finish_kernel(yf_ref, yb_ref, xs_ref, z_ref, d_ref, nw_ref, o_ref):
    y = yf_ref[...].astype(F32) + yb_ref[...].astype(F32) + xs_ref[...].astype(F32) * d_ref[...]
    y = y * _silu(z_ref[...].astype(F32))
    gw = SSD_WIDTH // SSD_GROUPS
    for g in range(SSD_GROUPS):
        yg = y[:, g * gw:(g + 1) * gw]
        ms = jnp.mean(yg * yg, axis=-1, keepdims=True)
        o_ref[:, g * gw:(g + 1) * gw] = (yg * lax.rsqrt(ms + EPS) * nw_ref[:, g * gw:(g + 1) * gw]).astype(o_ref.dtype)


def _ssd_finish(yf, yb, xs, proj, d_exp, norm_w, *, tm):
    m = yf.shape[0]
    row = lambda i: (i, 0)
    const = lambda i: (0, 0)
    return pl.pallas_call(
        _ssd_finish_kernel,
        grid=(m // tm,),
        in_specs=[
            pl.BlockSpec((tm, SSD_WIDTH), row),
            pl.BlockSpec((tm, SSD_WIDTH), row),
            pl.BlockSpec((tm, SSD_WIDTH), row),
            pl.BlockSpec((tm, SSD_WIDTH), lambda i: (i, OFF_Z // SSD_WIDTH)),
            pl.BlockSpec((1, SSD_WIDTH), const),
            pl.BlockSpec((1, SSD_WIDTH), const),
        ],
        out_specs=pl.BlockSpec((tm, SSD_WIDTH), row),
        out_shape=jax.ShapeDtypeStruct((m, SSD_WIDTH), BF16),
        compiler_params=_cp("parallel"),
        name="ssd_finish",
    )(yf, yb, xs, proj, d_exp, norm_w)


def _merge_kernel(ona_ref, ossd_ref, ogqa_ref, g_ref, x_ref, wna_ref, wssd_ref, wgqa_ref, wout_ref, o_ref, mg_scr):
    nc = 512
    branches = ((ona_ref, wna_ref), (ossd_ref, wssd_ref), (ogqa_ref, wgqa_ref))
    for c in range(D_MODEL // nc):
        cols = pl.ds(c * nc, nc)
        acc = None
        for b, (o_b, w_b) in enumerate(branches):
            p = jnp.dot(o_b[...], w_b[:, cols], preferred_element_type=F32)
            gate = _sigmoid(g_ref[:, pl.ds(b * D_MODEL + c * nc, nc)].astype(F32))
            acc = gate * p if acc is None else acc + gate * p
        mg_scr[:, cols] = acc.astype(BF16)
    for c in range(D_MODEL // nc):
        cols = pl.ds(c * nc, nc)
        o_ref[:, cols] = x_ref[:, cols] + jnp.dot(mg_scr[...], wout_ref[:, cols], preferred_element_type=F32)


def _merge(o_na, o_ssd, o_gqa, proj, x, w_na, w_ssd, w_gqa, w_out, *, tm):
    m = x.shape[0]
    row = lambda i: (i, 0)
    const = lambda i: (0, 0)
    one = pl.Buffered(1)
    return pl.pallas_call(
        _merge_kernel,
        grid=(m // tm,),
        in_specs=[
            pl.BlockSpec((tm, NA_WIDTH), row),
            pl.BlockSpec((tm, SSD_WIDTH), row),
            pl.BlockSpec((tm, GQA_WIDTH), row),
            pl.BlockSpec((tm, GATES), row),
            pl.BlockSpec((tm, D_MODEL), row),
            pl.BlockSpec((NA_WIDTH, D_MODEL), const, pipeline_mode=one),
            pl.BlockSpec((SSD_WIDTH, D_MODEL), const, pipeline_mode=one),
            pl.BlockSpec((GQA_WIDTH, D_MODEL), const, pipeline_mode=one),
            pl.BlockSpec((D_MODEL, D_MODEL), const, pipeline_mode=one),
        ],
        out_specs=pl.BlockSpec((tm, D_MODEL), row),
        out_shape=jax.ShapeDtypeStruct((m, D_MODEL), F32),
        scratch_shapes=[pltpu.VMEM((tm, D_MODEL), BF16)],
        compiler_params=_cp("parallel"),
        name="merge",
    )(o_na, o_ssd, o_gqa, proj, x, w_na, w_ssd, w_gqa, w_out)


FFN_EPI_COLS = 512


def _ffn_down_kernel(ug_ref, prev_ref, next_ref, uv_ref, cw_ref, cb_ref, wd_ref, x_ref, o_ref, hid_scr, *, tps):
    i = pl.program_id(0)
    first = (i % tps) == 0
    last = (i % tps) == tps - 1
    nchunk = D_FF // FFN_EPI_COLS

    def gated(c):
        cols = pl.ds(c * FFN_EPI_COLS, FFN_EPI_COLS)
        g = ug_ref[:, cols].astype(F32)
        prev = jnp.where(first, 0.0, prev_ref[:, cols].astype(F32))
        nxt = jnp.where(last, 0.0, next_ref[:, cols].astype(F32))
        a = _dwconv(g, prev, nxt, cw_ref[:, cols], cb_ref[:, cols], 3)
        return (_silu(a) * uv_ref[:, cols].astype(F32)).astype(BF16)

    hid_scr[0] = gated(0)
    for c in range(nchunk):
        if c + 1 < nchunk:
            hid_scr[(c + 1) % 2] = gated(c + 1)
        part = jnp.dot(hid_scr[c % 2], wd_ref[pl.ds(c * FFN_EPI_COLS, FFN_EPI_COLS), :], preferred_element_type=F32)
        if c == 0:
            o_ref[...] = x_ref[...] + part
        else:
            o_ref[...] += part


def _ffn_down(u, conv_w, conv_b, w_down, x, *, seq, tm):
    m = x.shape[0]
    nblk = m // HALO
    per = tm // HALO
    const = lambda i: (0, 0)
    return pl.pallas_call(
        functools.partial(_ffn_down_kernel, tps=seq // tm),
        grid=(m // tm,),
        in_specs=[
            pl.BlockSpec((tm, D_FF), lambda i: (i, 0)),
            pl.BlockSpec((HALO, D_FF), lambda i: (jnp.maximum(i * per - 1, 0), 0)),
            pl.BlockSpec((HALO, D_FF), lambda i: (jnp.minimum((i + 1) * per, nblk - 1), 0)),
            pl.BlockSpec((tm, D_FF), lambda i: (i, 1)),
            pl.BlockSpec((3, D_FF), const),
            pl.BlockSpec((1, D_FF), const),
            pl.BlockSpec((D_FF, D_MODEL), const, pipeline_mode=pl.Buffered(1)),
            pl.BlockSpec((tm, D_MODEL), lambda i: (i, 0)),
        ],
        out_specs=pl.BlockSpec((tm, D_MODEL), lambda i: (i, 0)),
        out_shape=jax.ShapeDtypeStruct((m, D_MODEL), F32),
        scratch_shapes=[pltpu.VMEM((2, tm, FFN_EPI_COLS), BF16)],
        compiler_params=_cp("parallel"),
        name="ffn_down",
    )(u, u, u, u, conv_w, conv_b, w_down, x)


def _tiles(seq):
    return dict(tm_mm=min(seq, 1024), tm_ew=min(seq, 512), tm_merge=min(seq, 512), tm_down=min(seq, 256),
                tq=min(seq, 512), tk=min(seq, 1024))


def _prep_layer_params(p, na_bias):
    (norm1_w, w_in, na_q_norm, na_k_norm, _, ssd_conv_w, ssd_conv_b, ssd_dt_bias, ssd_a_log, ssd_d,
     ssd_norm_w, gqa_q_norm, gqa_k_norm, w_branch, w_out, norm2_w, ffn_w_up, ffn_conv_w, ffn_conv_b, ffn_w_down) = p
    sizes = (3 * NA_WIDTH, GQA_WIDTH, GQA_KV_WIDTH, GQA_KV_WIDTH, SSD_WIDTH, SSD_XBC, 2 * SSD_HEADS, GATES)
    offs = np.concatenate([[0], np.cumsum(sizes)])
    piece = lambda k: w_in[:, offs[k]:offs[k + 1]]
    na, gq, gk, gv, z, xbc, dtw, gates = (piece(k) for k in range(8))
    w_main = jnp.concatenate([gates, xbc, na, z, gq, gk, gv], axis=1).astype(BF16)
    w_dt = jnp.pad(dtw, ((0, 0), (0, DT_PAD - 2 * SSD_HEADS))).astype(BF16)
    pad_dt = lambda v: jnp.pad(v.reshape(1, 2 * SSD_HEADS), ((0, 0), (0, DT_PAD - 2 * SSD_HEADS)))
    row = lambda v: v.reshape(1, -1)
    wb = w_branch.astype(BF16)
    return dict(
        norm1_w=row(norm1_w), w_main=w_main, w_dt=w_dt,
        na_q_norm=row(na_q_norm), na_k_norm=row(na_k_norm), na_bias=na_bias,
        conv_w=ssd_conv_w, conv_b=row(ssd_conv_b), dt_bias=pad_dt(ssd_dt_bias), a_log=pad_dt(ssd_a_log),
        d_exp=row(jnp.repeat(ssd_d, SSD_HEAD_DIM)), ssd_norm_w=row(ssd_norm_w),
        gqa_q_norm=row(gqa_q_norm), gqa_k_norm=row(gqa_k_norm),
        w_na=wb[:NA_WIDTH], w_ssd=wb[NA_WIDTH:NA_WIDTH + SSD_WIDTH], w_gqa=wb[NA_WIDTH + SSD_WIDTH:],
        w_out=w_out.astype(BF16), norm2_w=row(norm2_w), w_up=ffn_w_up.astype(BF16),
        ffn_conv_w=ffn_conv_w, ffn_conv_b=row(ffn_conv_b), w_down=ffn_w_down.astype(BF16),
    )


def _prep_params(params):
    depth = params[0].shape[0]
    na_bias = _na_bias_tables(params[4])
    return [_prep_layer_params([p[i] for p in params], na_bias[i]) for i in range(depth)]


def _layer(x, lp, consts, *, bsz, seq):
    t = _tiles(seq)
    cos_t, sin_t, tri, expand = consts
    proj, dt_raw = _norm_matmul(x, lp["norm1_w"], lp["w_main"], lp["w_dt"], tm=t["tm_mm"], tn=1024)
    o_na = _na_attention(proj, lp["na_bias"], lp["na_q_norm"], lp["na_k_norm"], bsz=bsz, seq=seq)
    qn, kn, vaug = _gqa_prep(proj, cos_t, sin_t, lp["gqa_q_norm"], lp["gqa_k_norm"], seq=seq, tm=t["tm_ew"])
    o_gqa = _gqa_attention(qn, kn, vaug, bsz=bsz, seq=seq, tq=t["tq"], tk=t["tk"])
    xs, bc, dt = _ssd_prep(proj, dt_raw, lp["conv_w"], lp["conv_b"], lp["dt_bias"], seq=seq, tm=t["tm_ew"])
    yf, yb = _ssd_scan(xs, bc, dt, lp["a_log"], tri, expand, bsz=bsz, seq=seq)
    o_ssd = _ssd_finish(yf, yb, xs, proj, lp["d_exp"], lp["ssd_norm_w"], tm=t["tm_ew"])
    x = _merge(o_na, o_ssd, o_gqa, proj, x, lp["w_na"], lp["w_ssd"], lp["w_gqa"], lp["w_out"], tm=t["tm_merge"])
    u = _norm_matmul(x, lp["norm2_w"], lp["w_up"], tm=t["tm_mm"], tn=1024)
    return _ffn_down(u, lp["ffn_conv_w"], lp["ffn_conv_b"], lp["w_down"], x, seq=seq, tm=t["tm_down"])


def _trunk(x, layer_params):
    bsz, seq, d = x.shape
    assert d == D_MODEL and seq % 1024 == 0 and seq // GRID_W >= NA_KROWS
    consts = _rope_tables(seq) + _ssd_constants()
    h = x.reshape(bsz * seq, d)
    for lp in layer_params:
        h = _layer(h, lp, consts, bsz=bsz, seq=seq)
    return h.reshape(bsz, seq, d)


def kernel(x_prompt, x_sample, norm1_w, w_in, na_q_norm, na_k_norm, na_rpb, ssd_conv_w, ssd_conv_b, ssd_dt_bias, ssd_a_log, ssd_d, ssd_norm_w, gqa_q_norm, gqa_k_norm, w_branch, w_out, norm2_w, ffn_w_up, ffn_conv_w, ffn_conv_b, ffn_w_down):
    params = (norm1_w, w_in, na_q_norm, na_k_norm, na_rpb, ssd_conv_w, ssd_conv_b, ssd_dt_bias, ssd_a_log,
              ssd_d, ssd_norm_w, gqa_q_norm, gqa_k_norm, w_branch, w_out, norm2_w, ffn_w_up, ffn_conv_w,
              ffn_conv_b, ffn_w_down)
    layer_params = _prep_params(params)
    return (_trunk(x_prompt, layer_params), _trunk(x_sample, layer_params))
```

```python
import functools
import math

import numpy as np
import jax
import jax.numpy as jnp
from jax import lax
from jax.experimental import pallas as pl
from jax.experimental.pallas import tpu as pltpu

F32 = jnp.float32
BF16 = jnp.bfloat16

D_MODEL = 2048
GRID_W = 64
EPS = 1e-6
NA_HEADS = 4
HEAD_DIM = 128
NA_WIDTH = NA_HEADS * HEAD_DIM
NA_WIN_H = 8
NA_WIN_W = 16
SSD_HEADS = 16
SSD_HEAD_DIM = 64
SSD_WIDTH = SSD_HEADS * SSD_HEAD_DIM
SSD_STATE = 128
SSD_GROUPS = 2
SSD_CONV = 5
SSD_CHUNK = 128
SSD_BC = 2 * SSD_GROUPS * SSD_STATE
SSD_XBC = SSD_WIDTH + SSD_BC
GQA_HEADS = 4
GQA_KV_HEADS = 2
GQA_WIDTH = GQA_HEADS * HEAD_DIM
GQA_KV_WIDTH = GQA_KV_HEADS * HEAD_DIM
ROPE_THETA = 10000.0
D_FF = 5632
N_BRANCH = 3
GATES = N_BRANCH * D_MODEL
DT_PAD = 128

OFF_GATES = 0
OFF_XBC = OFF_GATES + GATES
OFF_NA = OFF_XBC + SSD_XBC
OFF_Z = OFF_NA + 3 * NA_WIDTH
OFF_GQ = OFF_Z + SSD_WIDTH
OFF_GK = OFF_GQ + GQA_WIDTH
OFF_GV = OFF_GK + GQA_KV_WIDTH
PROJ_COLS = OFF_GV + GQA_KV_WIDTH

NEG = -1e30
LOG2E = math.log2(math.e)
HALO = 8
VMEM_LIMIT = 56 * 2**20


def _cp(*sem):
    return pltpu.CompilerParams(dimension_semantics=sem, vmem_limit_bytes=VMEM_LIMIT)


def _sigmoid(x):
    return 1.0 / (1.0 + jnp.exp(-x))


def _silu(x):
    return x * _sigmoid(x)


NORM_ROWS = 128


def _rmsnorm_rows(x_ref, nw_ref, h_ref, n_rows):
    def body(r, c):
        rows = pl.ds(pl.multiple_of(r * NORM_ROWS, NORM_ROWS), NORM_ROWS)
        x = x_ref[rows, :]
        ms = jnp.mean(x * x, axis=-1, keepdims=True)
        h_ref[rows, :] = (x * lax.rsqrt(ms + EPS) * nw_ref[...]).astype(BF16)
        return c

    lax.fori_loop(0, n_rows // NORM_ROWS, body, 0)


def _rmsnorm_kernel(x_ref, nw_ref, h_ref, *, tm):
    _rmsnorm_rows(x_ref, nw_ref, h_ref, tm)


def _rmsnorm(x, nw, *, tm):
    m, d = x.shape
    return pl.pallas_call(
        functools.partial(_rmsnorm_kernel, tm=tm),
        grid=(m // tm,),
        in_specs=[pl.BlockSpec((tm, d), lambda i: (i, 0)), pl.BlockSpec((1, d), lambda i: (0, 0))],
        out_specs=pl.BlockSpec((tm, d), lambda i: (i, 0)),
        out_shape=jax.ShapeDtypeStruct((m, d), BF16),
        compiler_params=_cp("parallel"),
        name="rmsnorm",
    )(x, nw)


def _mm_kernel(*refs, with_dt):
    if with_dt:
        h_ref, w_ref, wdt_ref, o_ref, odt_ref = refs

        @pl.when(pl.program_id(1) == 0)
        def _():
            odt_ref[...] = jnp.dot(h_ref[...], wdt_ref[...], preferred_element_type=F32)
    else:
        h_ref, w_ref, o_ref = refs
    o_ref[...] = jnp.dot(h_ref[...], w_ref[...], preferred_element_type=F32).astype(o_ref.dtype)


def _matmul(h, w, wdt=None, *, tm, tn):
    m, d = h.shape
    n = w.shape[1]
    with_dt = wdt is not None
    in_specs = [
        pl.BlockSpec((tm, d), lambda i, j: (i, 0)),
        pl.BlockSpec((d, tn), lambda i, j: (0, j)),
    ]
    out_shape = [jax.ShapeDtypeStruct((m, n), BF16)]
    out_specs = [pl.BlockSpec((tm, tn), lambda i, j: (i, j))]
    args = [h, w]
    if with_dt:
        in_specs.append(pl.BlockSpec((d, DT_PAD), lambda i, j: (0, 0)))
        out_shape.append(jax.ShapeDtypeStruct((m, DT_PAD), F32))
        out_specs.append(pl.BlockSpec((tm, DT_PAD), lambda i, j: (i, 0)))
        args.append(wdt)
    res = pl.pallas_call(
        functools.partial(_mm_kernel, with_dt=with_dt),
        grid=(m // tm, n // tn),
        in_specs=in_specs,
        out_specs=out_specs,
        out_shape=out_shape,
        compiler_params=_cp("parallel", "arbitrary"),
        name="matmul_dt" if with_dt else "matmul",
    )(*args)
    return res if with_dt else res[0]


NA_QROWS = 8
NA_KROWS = 16
NA_TQ = NA_QROWS * GRID_W
NA_TK = NA_KROWS * GRID_W
NA_KPIECE = 256
NA_NPIECE = NA_TK // NA_KPIECE


def _na_bias_tables(rpb):
    rows = 4 * NA_QROWS
    nb = rows // NA_QROWS
    depth = rpb.shape[0]
    c = np.arange(GRID_W)
    col_start = np.clip(c - NA_WIN_W // 2, 0, GRID_W - NA_WIN_W)
    in_win = (c[None, :] >= col_start[:, None]) & (c[None, :] < col_start[:, None] + NA_WIN_W)
    dc = np.clip(c[None, :] - c[:, None], -(NA_WIN_W - 1), NA_WIN_W - 1) + NA_WIN_W - 1
    onehot_c = (dc[:, :, None] == np.arange(2 * NA_WIN_W - 1)).astype(np.float32)
    by_col = jnp.einsum("lhab,qcb->lhaqc", rpb.astype(F32), jnp.asarray(onehot_c), precision=lax.Precision.HIGHEST)
    tables = []
    for b in (0, 1, nb - 1):
        r = NA_QROWS * b + np.arange(NA_QROWS)
        ks = int(np.clip(NA_QROWS * b - NA_WIN_H // 2, 0, rows - NA_KROWS))
        krow = ks + np.arange(NA_KROWS)
        rs = np.clip(r - NA_WIN_H // 2, 0, rows - NA_WIN_H)
        valid = (krow[None, :] >= rs[:, None]) & (krow[None, :] < rs[:, None] + NA_WIN_H)
        dr = np.clip(krow[None, :] - r[:, None], -(NA_WIN_H - 1), NA_WIN_H - 1) + NA_WIN_H - 1
        bias = jnp.take(by_col, jnp.asarray(dr.reshape(-1)), axis=2)
        bias = bias.reshape(depth, NA_HEADS, NA_QROWS, NA_KROWS, GRID_W, GRID_W).transpose(0, 1, 2, 4, 3, 5)
        mask = valid[:, None, :, None] & in_win[None, :, None, :]
        bias = jnp.where(jnp.asarray(mask)[None, None], bias * LOG2E, NEG)
        tables.append(bias.reshape(depth, NA_HEADS, NA_TQ, NA_TK))
    return jnp.stack(tables, axis=1)


def _head_rmsnorm(x, w):
    ms = jnp.mean(x * x, axis=-1, keepdims=True)
    return x * lax.rsqrt(ms + EPS) * w


NA_SUB = 128


def _na_kernel(q_ref, k0, k1, k2, k3, v0, v1, v2, v3, bias_ref, qw_ref, kw_ref, o_ref, kn_scr, va_scr, s_scr, *, nb):
    i = pl.program_id(2)
    kind = jnp.where(i == 0, 0, jnp.where(i == nb - 1, 2, 1))
    scale = HEAD_DIM ** -0.5 * LOG2E
    ones = jnp.ones((NA_KPIECE, HEAD_DIM), BF16)
    for t, (k_ref, v_ref) in enumerate(zip((k0, k1, k2, k3), (v0, v1, v2, v3))):
        piece = pl.ds(t * NA_KPIECE, NA_KPIECE)
        kn_scr[piece, :] = _head_rmsnorm(k_ref[...].astype(F32), kw_ref[...]).astype(BF16)
        va_scr[piece, :HEAD_DIM] = v_ref[...]
        va_scr[piece, HEAD_DIM:] = ones
    nlane = NA_TK // HEAD_DIM

    def scores(qs):
        rows = pl.ds(qs * NA_SUB, NA_SUB)
        qn = (_head_rmsnorm(q_ref[rows, :].astype(F32), qw_ref[...]) * scale).astype(BF16)
        s = lax.dot_general(qn, kn_scr[...], (((1,), (1,)), ((), ())), preferred_element_type=F32)
        return s + bias_ref[kind, rows, :]

    def finish(qs, slot):
        blocks = [s_scr[slot, :, j * HEAD_DIM:(j + 1) * HEAD_DIM] for j in range(nlane)]
        part = blocks[0]
        for blk in blocks[1:]:
            part = jnp.maximum(part, blk)
        m = jnp.broadcast_to(jnp.max(part, axis=-1, keepdims=True), part.shape)
        p = jnp.concatenate([jnp.exp2(blk - m).astype(BF16) for blk in blocks], axis=1)
        pv = jnp.dot(p, va_scr[...], preferred_element_type=F32)
        o_ref[pl.ds(qs * NA_SUB, NA_SUB), :] = (pv[:, :HEAD_DIM] / pv[:, HEAD_DIM:]).astype(o_ref.dtype)

    nsub = NA_TQ // NA_SUB
    s_scr[0] = scores(0)
    for qs in range(nsub):
        if qs + 1 < nsub:
            s_scr[(qs + 1) % 2] = scores(qs + 1)
        finish(qs, qs % 2)


def _na_attention(proj, bias, qw, kw, *, bsz, seq):
    m = proj.shape[0]
    rows = seq // GRID_W
    nb = rows // NA_QROWS
    npieces = seq // NA_KPIECE
    q_col = OFF_NA // HEAD_DIM
    k_col = (OFF_NA + NA_WIDTH) // HEAD_DIM
    v_col = (OFF_NA + 2 * NA_WIDTH) // HEAD_DIM

    def kbase(i):
        return jnp.clip(2 * i - 1, 0, npieces - NA_NPIECE)

    def kv_spec(col, t):
        return pl.BlockSpec((NA_KPIECE, HEAD_DIM), lambda h, b, i: (b * npieces + kbase(i) + t, col + h))

    in_specs = [pl.BlockSpec((NA_TQ, HEAD_DIM), lambda h, b, i: (b * nb + i, q_col + h))]
    in_specs += [kv_spec(k_col, t) for t in range(NA_NPIECE)]
    in_specs += [kv_spec(v_col, t) for t in range(NA_NPIECE)]
    in_specs += [
        pl.BlockSpec((3, None, NA_TQ, NA_TK), lambda h, b, i: (0, h, 0, 0)),
        pl.BlockSpec((1, HEAD_DIM), lambda h, b, i: (0, 0)),
        pl.BlockSpec((1, HEAD_DIM), lambda h, b, i: (0, 0)),
    ]
    return pl.pallas_call(
        functools.partial(_na_kernel, nb=nb),
        grid=(NA_HEADS, bsz, nb),
        in_specs=in_specs,
        out_specs=pl.BlockSpec((NA_TQ, HEAD_DIM), lambda h, b, i: (b * nb + i, h)),
        out_shape=jax.ShapeDtypeStruct((m, NA_WIDTH), BF16),
        scratch_shapes=[
            pltpu.VMEM((NA_TK, HEAD_DIM), BF16),
            pltpu.VMEM((NA_TK, 2 * HEAD_DIM), BF16),
            pltpu.VMEM((2, NA_SUB, NA_TK), F32),
        ],
        compiler_params=_cp("parallel", "parallel", "parallel"),
        name="na_attention",
    )(proj, *([proj] * (2 * NA_NPIECE)), bias, qw, kw)


def _rope_tables(seq):
    t = np.arange(seq)
    pos = np.stack([t // GRID_W, t % GRID_W], axis=-1).astype(np.float32)
    n_freq = HEAD_DIM // 4
    inv_freq = jnp.asarray(ROPE_THETA, F32) ** (-jnp.arange(n_freq, dtype=F32) / n_freq)
    ang = jnp.asarray(pos)[:, :, None] * inv_freq
    cos = jnp.cos(ang)
    sin = jnp.sin(ang)
    cos_t = jnp.concatenate([cos[:, 0], cos[:, 0], cos[:, 1], cos[:, 1]], axis=-1)
    sin_t = jnp.concatenate([-sin[:, 0], sin[:, 0], -sin[:, 1], sin[:, 1]], axis=-1)
    return cos_t, sin_t


def _rope(x, cos_t, sin_t, first_half):
    swapped = jnp.where(first_half, pltpu.roll(x, HEAD_DIM - 32, axis=1), pltpu.roll(x, 32, axis=1))
    return x * cos_t + swapped * sin_t


def _gqa_prep_kernel(q_ref, k_ref, v_ref, cos_ref, sin_ref, qw_ref, kw_ref, qo_ref, ko_ref, vo_ref):
    cos_t = cos_ref[...]
    sin_t = sin_ref[...]
    lane = lax.broadcasted_iota(jnp.int32, cos_t.shape, 1)
    first_half = (lane % 64) < 32
    scale = HEAD_DIM ** -0.5 * LOG2E
    for h in range(GQA_HEADS):
        cols = pl.ds(h * HEAD_DIM, HEAD_DIM)
        x = _head_rmsnorm(q_ref[:, cols].astype(F32), qw_ref[...])
        qo_ref[:, cols] = (_rope(x, cos_t, sin_t, first_half) * scale).astype(BF16)
    ones = jnp.ones(cos_t.shape, BF16)
    for h in range(GQA_KV_HEADS):
        cols = pl.ds(h * HEAD_DIM, HEAD_DIM)
        x = _head_rmsnorm(k_ref[:, cols].astype(F32), kw_ref[...])
        ko_ref[:, cols] = _rope(x, cos_t, sin_t, first_half).astype(BF16)
        vo_ref[:, pl.ds(2 * h * HEAD_DIM, HEAD_DIM)] = v_ref[:, cols]
        vo_ref[:, pl.ds((2 * h + 1) * HEAD_DIM, HEAD_DIM)] = ones


def _gqa_prep(proj, cos_t, sin_t, qw, kw, *, seq, tm):
    m = proj.shape[0]
    tps = seq // tm
    return pl.pallas_call(
        _gqa_prep_kernel,
        grid=(m // tm,),
        in_specs=[
            pl.BlockSpec((tm, GQA_WIDTH), lambda i: (i, OFF_GQ // GQA_WIDTH)),
            pl.BlockSpec((tm, GQA_KV_WIDTH), lambda i: (i, OFF_GK // GQA_KV_WIDTH)),
            pl.BlockSpec((tm, GQA_KV_WIDTH), lambda i: (i, OFF_GV // GQA_KV_WIDTH)),
            pl.BlockSpec((tm, HEAD_DIM), lambda i: (i % tps, 0)),
            pl.BlockSpec((tm, HEAD_DIM), lambda i: (i % tps, 0)),
            pl.BlockSpec((1, HEAD_DIM), lambda i: (0, 0)),
            pl.BlockSpec((1, HEAD_DIM), lambda i: (0, 0)),
        ],
        out_specs=[
            pl.BlockSpec((tm, GQA_WIDTH), lambda i: (i, 0)),
            pl.BlockSpec((tm, GQA_KV_WIDTH), lambda i: (i, 0)),
            pl.BlockSpec((tm, 2 * GQA_KV_WIDTH), lambda i: (i, 0)),
        ],
        out_shape=[
            jax.ShapeDtypeStruct((m, GQA_WIDTH), BF16),
            jax.ShapeDtypeStruct((m, GQA_KV_WIDTH), BF16),
            jax.ShapeDtypeStruct((m, 2 * GQA_KV_WIDTH), BF16),
        ],
        compiler_params=_cp("parallel"),
        name="gqa_prep",
    )(proj, proj, proj, cos_t, sin_t, qw, kw)


def _flash_kernel(q_ref, k_ref, v_ref, o_ref, s_scr, m_scr, acc_scr, *, tq, tk, seq):
    q2 = jnp.concatenate([q_ref[:, :HEAD_DIM], q_ref[:, HEAD_DIM:]], axis=0)
    m_scr[...] = jnp.full(m_scr.shape, NEG, F32)
    acc_scr[...] = jnp.zeros(acc_scr.shape, F32)
    nblk = tk // HEAD_DIM
    n = seq // tk

    def chunk(kc):
        return pl.ds(pl.multiple_of(kc * tk, tk), tk)

    def scores(kc):
        return lax.dot_general(q2, k_ref[chunk(kc), :], (((1,), (1,)), ((), ())), preferred_element_type=F32)

    def softmax_pv(kc, slot):
        blocks = [s_scr[slot, :, j * HEAD_DIM:(j + 1) * HEAD_DIM] for j in range(nblk)]
        part = blocks[0]
        for blk in blocks[1:]:
            part = jnp.maximum(part, blk)
        m_prev = m_scr[...]
        m_new = jnp.maximum(m_prev, jnp.max(part, axis=-1, keepdims=True))
        alpha = jnp.exp2(m_prev - m_new)
        p = jnp.concatenate([jnp.exp2(blk - m_new).astype(BF16) for blk in blocks], axis=1)
        pv = jnp.dot(p, v_ref[chunk(kc), :], preferred_element_type=F32)
        acc_scr[...] = jnp.concatenate([alpha, alpha], axis=1) * acc_scr[...] + pv
        m_scr[...] = m_new

    s_scr[0] = scores(0)

    def pair(t, c):
        s_scr[1] = scores(2 * t + 1)
        softmax_pv(2 * t, 0)
        s_scr[0] = scores(2 * t + 2)
        softmax_pv(2 * t + 1, 1)
        return c

    npairs = (n - 1) // 2
    lax.fori_loop(0, npairs, pair, 0)
    base = 2 * npairs
    if n - base == 2:
        s_scr[1] = scores(base + 1)
        softmax_pv(base, 0)
        softmax_pv(base + 1, 1)
    else:
        softmax_pv(base, 0)
    o = acc_scr[:, :HEAD_DIM] / acc_scr[:, HEAD_DIM:]
    o_ref[:, :HEAD_DIM] = o[:tq].astype(o_ref.dtype)
    o_ref[:, HEAD_DIM:] = o[tq:].astype(o_ref.dtype)


def _gqa_attention(qn, kn, vaug, *, bsz, seq, tq, tk):
    m = qn.shape[0]
    nq = seq // tq
    rep = GQA_HEADS // GQA_KV_HEADS
    return pl.pallas_call(
        functools.partial(_flash_kernel, tq=tq, tk=tk, seq=seq),
        grid=(bsz, GQA_KV_HEADS, nq),
        in_specs=[
            pl.BlockSpec((tq, rep * HEAD_DIM), lambda b, g, i: (b * nq + i, g)),
            pl.BlockSpec((seq, HEAD_DIM), lambda b, g, i: (b, g)),
            pl.BlockSpec((seq, 2 * HEAD_DIM), lambda b, g, i: (b, g)),
        ],
        out_specs=pl.BlockSpec((tq, rep * HEAD_DIM), lambda b, g, i: (b * nq + i, g)),
        out_shape=jax.ShapeDtypeStruct((m, GQA_WIDTH), BF16),
        scratch_shapes=[
            pltpu.VMEM((2, rep * tq, tk), F32),
            pltpu.VMEM((rep * tq, HEAD_DIM), F32),
            pltpu.VMEM((rep * tq, 2 * HEAD_DIM), F32),
        ],
        compiler_params=_cp("parallel", "parallel", "parallel"),
        name="gqa_flash",
    )(qn, kn, vaug)


def _with_halo(x, prev, nxt):
    return jnp.concatenate([x, nxt, prev], axis=0)


def _shift_rows(xcat, s, n):
    return pltpu.roll(xcat, s % xcat.shape[0], axis=0)[:n]


def _dwconv(x, prev, nxt, w, b, width):
    n = x.shape[0]
    xcat = _with_halo(x, prev, nxt)
    pad = width // 2
    out = b + x * w[pad:pad + 1, :]
    for i in range(width):
        if i != pad:
            out = out + _shift_rows(xcat, pad - i, n) * w[i:i + 1, :]
    return out


def _halo_specs(tm, width, col_block, m):
    nblk = m // HALO
    per = tm // HALO
    prev = pl.BlockSpec((HALO, width), lambda i, *_: (jnp.maximum(i * per - 1, 0), col_block))
    nxt = pl.BlockSpec((HALO, width), lambda i, *_: (jnp.minimum((i + 1) * per, nblk - 1), col_block))
    return prev, nxt


def _ssd_prep_kernel(x_ref, prev_ref, next_ref, dtr_ref, cw_ref, cb_ref, dtb_ref, xs_ref, bc_ref, dt_ref, *, tps):
    i = pl.program_id(0)
    first = (i % tps) == 0
    last = (i % tps) == tps - 1
    x = x_ref[...].astype(F32)
    prev = jnp.where(first, 0.0, prev_ref[...].astype(F32))
    nxt = jnp.where(last, 0.0, next_ref[...].astype(F32))
    y = _silu(_dwconv(x, prev, nxt, cw_ref[...], cb_ref[...], SSD_CONV))
    xs_ref[...] = y[:, :SSD_WIDTH].astype(BF16)
    bc_ref[...] = y[:, SSD_WIDTH:].astype(BF16)
    v = dtr_ref[...] + dtb_ref[...]
    dt_ref[...] = jnp.maximum(v, 0.0) + jnp.log(1.0 + jnp.exp(-jnp.abs(v)))


def _ssd_prep(proj, dt_raw, conv_w, conv_b, dt_bias, *, seq, tm):
    m = proj.shape[0]
    cb = OFF_XBC // SSD_XBC
    prev, nxt = _halo_specs(tm, SSD_XBC, cb, m)
    return pl.pallas_call(
        functools.partial(_ssd_prep_kernel, tps=seq // tm),
        grid=(m // tm,),
        in_specs=[
            pl.BlockSpec((tm, SSD_XBC), lambda i: (i, cb)),
            prev,
            nxt,
            pl.BlockSpec((tm, DT_PAD), lambda i: (i, 0)),
            pl.BlockSpec((SSD_CONV, SSD_XBC), lambda i: (0, 0)),
            pl.BlockSpec((1, SSD_XBC), lambda i: (0, 0)),
            pl.BlockSpec((1, DT_PAD), lambda i: (0, 0)),
        ],
        out_specs=[
            pl.BlockSpec((tm, SSD_WIDTH), lambda i: (i, 0)),
            pl.BlockSpec((tm, SSD_BC), lambda i: (i, 0)),
            pl.BlockSpec((tm, DT_PAD), lambda i: (i, 0)),
        ],
        out_shape=[
            jax.ShapeDtypeStruct((m, SSD_WIDTH), BF16),
            jax.ShapeDtypeStruct((m, SSD_BC), BF16),
            jax.ShapeDtypeStruct((m, DT_PAD), F32),
        ],
        compiler_params=_cp("parallel"),
        name="ssd_prep",
    )(proj, proj, proj, dt_raw, conv_w, conv_b, dt_bias)


def _ssd_constants():
    q = SSD_CHUNK
    li = np.arange(q)
    tri = np.stack([li[:, None] >= li[None, :], li[:, None] <= li[None, :]]).astype(np.float32)
    hp = np.arange(SSD_WIDTH) // SSD_HEAD_DIM
    expand = np.zeros((2, 2 * DT_PAD, SSD_WIDTH), np.float32)
    for d in range(2):
        expand[d, d * SSD_HEADS + hp, np.arange(SSD_WIDTH)] = 1.0
        expand[d, DT_PAD + d * SSD_HEADS + hp, np.arange(SSD_WIDTH)] = 1.0
    return jnp.asarray(tri, F32), jnp.asarray(expand, BF16)


def _expand_heads(v, e):
    hi = v.astype(BF16)
    lo = (v - hi.astype(F32)).astype(BF16)
    return jnp.dot(jnp.concatenate([hi, lo], axis=1), e, preferred_element_type=F32)


def _ssd_scan_kernel(xs_f, bc_f, dt_f, xs_b, bc_b, dt_b, alog_ref, tri_ref, exp_ref, yf_ref, yb_ref, h_scr):
    @pl.when(pl.program_id(1) == 0)
    def _():
        h_scr[...] = jnp.zeros(h_scr.shape, F32)

    q = SSD_CHUNK
    gw = SSD_WIDTH // SSD_GROUPS
    a_row = -jnp.exp(alog_ref[...])
    li = lax.broadcasted_iota(jnp.int32, (q, q), 0)
    si = lax.broadcasted_iota(jnp.int32, (q, q), 1)
    lane = lax.broadcasted_iota(jnp.int32, (q, HEAD_DIM), 1)
    left = lane < SSD_HEAD_DIM
    dirs = ((xs_f, bc_f, dt_f, yf_ref, q - 1), (xs_b, bc_b, dt_b, yb_ref, 0))
    for d, (xs_ref, bc_ref, dt_ref, y_ref, last) in enumerate(dirs):
        keep = (li >= si) if d == 0 else (li <= si)
        e = exp_ref[d]
        dt = dt_ref[...]
        da = dt * a_row
        cum = jnp.dot(tri_ref[d], da, precision=lax.Precision.HIGHEST, preferred_element_type=F32)
        cum_t = cum.T
        tot = cum[last:last + 1, :]
        stack_e = _expand_heads(jnp.concatenate([dt, jnp.exp(cum), jnp.exp(tot - cum)], axis=0), e)
        dt_e, ec_e, w_e = stack_e[:q], stack_e[q:2 * q], stack_e[2 * q:]
        dec_e = ec_e[last:last + 1, :]
        xd = xs_ref[...].astype(F32) * dt_e
        xdb = xd.astype(BF16)
        xw = (xd * w_e).astype(BF16)
        for g in range(SSD_GROUPS):
            gl = slice(g * gw, (g + 1) * gw)
            bg = bc_ref[:, g * SSD_STATE:(g + 1) * SSD_STATE]
            cg = bc_ref[:, (SSD_GROUPS + g) * SSD_STATE:(SSD_GROUPS + g + 1) * SSD_STATE]
            cb = lax.dot_general(cg, bg, (((1,), (1,)), ((), ())), preferred_element_type=F32)
            bg_t = bg.astype(F32).T.astype(BF16)
            h_prev = h_scr[d, g]
            y_off = jnp.dot(cg, h_prev.astype(BF16), preferred_element_type=F32) * ec_e[:, gl]
            states_t = jnp.dot(bg_t, xw[:, gl], preferred_element_type=F32)
            h_scr[d, g] = h_prev * dec_e[:, gl] + states_t
            for j in range(gw // HEAD_DIM):
                mats = []
                for hh in range(2):
                    col = d * SSD_HEADS + g * (SSD_HEADS // SSD_GROUPS) + 2 * j + hh
                    seg = cum[:, col:col + 1] - cum_t[col:col + 1, :]
                    lm = jnp.exp(jnp.where(keep, seg, NEG))
                    mats.append((cb * lm).astype(BF16))
                lo = g * gw + j * HEAD_DIM
                slab = xdb[:, lo:lo + HEAD_DIM]
                zero = jnp.zeros_like(slab)
                wbd = jnp.concatenate([jnp.where(left, slab, zero), jnp.where(left, zero, slab)], axis=0)
                y_diag = jnp.dot(jnp.concatenate(mats, axis=1), wbd, preferred_element_type=F32)
                y_ref[:, lo:lo + HEAD_DIM] = (y_diag + y_off[:, j * HEAD_DIM:(j + 1) * HEAD_DIM]).astype(y_ref.dtype)


def _ssd_scan(xs, bc, dt, a_log, tri, expand, *, bsz, seq):
    m = xs.shape[0]
    nc = seq // SSD_CHUNK
    fwd = lambda b, c: (b * nc + c, 0)
    bwd = lambda b, c: (b * nc + nc - 1 - c, 0)
    return pl.pallas_call(
        _ssd_scan_kernel,
        grid=(bsz, nc),
        in_specs=[
            pl.BlockSpec((SSD_CHUNK, SSD_WIDTH), fwd),
            pl.BlockSpec((SSD_CHUNK, SSD_BC), fwd),
            pl.BlockSpec((SSD_CHUNK, DT_PAD), fwd),
            pl.BlockSpec((SSD_CHUNK, SSD_WIDTH), bwd),
            pl.BlockSpec((SSD_CHUNK, SSD_BC), bwd),
            pl.BlockSpec((SSD_CHUNK, DT_PAD), bwd),
            pl.BlockSpec((1, DT_PAD), lambda b, c: (0, 0)),
            pl.BlockSpec((2, SSD_CHUNK, SSD_CHUNK), lambda b, c: (0, 0, 0)),
            pl.BlockSpec((2, 2 * DT_PAD, SSD_WIDTH), lambda b, c: (0, 0, 0)),
        ],
        out_specs=[
            pl.BlockSpec((SSD_CHUNK, SSD_WIDTH), fwd),
            pl.BlockSpec((SSD_CHUNK, SSD_WIDTH), bwd),
        ],
        out_shape=[
            jax.ShapeDtypeStruct((m, SSD_WIDTH), BF16),
            jax.ShapeDtypeStruct((m, SSD_WIDTH), BF16),
        ],
        scratch_shapes=[pltpu.VMEM((2, SSD_GROUPS, SSD_STATE, SSD_WIDTH // SSD_GROUPS), F32)],
        compiler_params=_cp("parallel", "arbitrary"),
        name="ssd_scan",
    )(xs, bc, dt, xs, bc, dt, a_log, tri, expand)


def _ssd_finish_kernel(yf_ref, yb_ref, xs_ref, z_ref, d_ref, nw_ref, o_ref):
    y = yf_ref[...].astype(F32) + yb_ref[...].astype(F32) + xs_ref[...].astype(F32) * d_ref[...]
    y = y * _silu(z_ref[...].astype(F32))
    gw = SSD_WIDTH // SSD_GROUPS
    for g in range(SSD_GROUPS):
        yg = y[:, g * gw:(g + 1) * gw]
        ms = jnp.mean(yg * yg, axis=-1, keepdims=True)
        o_ref[:, g * gw:(g + 1) * gw] = (yg * lax.rsqrt(ms + EPS) * nw_ref[:, g * gw:(g + 1) * gw]).astype(o_ref.dtype)


def _ssd_finish(yf, yb, xs, proj, d_exp, norm_w, *, tm):
    m = yf.shape[0]
    row = lambda i: (i, 0)
    const = lambda i: (0, 0)
    return pl.pallas_call(
        _ssd_finish_kernel,
        grid=(m // tm,),
        in_specs=[
            pl.BlockSpec((tm, SSD_WIDTH), row),
            pl.BlockSpec((tm, SSD_WIDTH), row),
            pl.BlockSpec((tm, SSD_WIDTH), row),
            pl.BlockSpec((tm, SSD_WIDTH), lambda i: (i, OFF_Z // SSD_WIDTH)),
            pl.BlockSpec((1, SSD_WIDTH), const),
            pl.BlockSpec((1, SSD_WIDTH), const),
        ],
        out_specs=pl.BlockSpec((tm, SSD_WIDTH), row),
        out_shape=jax.ShapeDtypeStruct((m, SSD_WIDTH), BF16),
        compiler_params=_cp("parallel"),
        name="ssd_finish",
    )(yf, yb, xs, proj, d_exp, norm_w)


def _merge_kernel(ona_ref, ossd_ref, ogqa_ref, g_ref, x_ref, wna_ref, wssd_ref, wgqa_ref, wout_ref, nw_ref,
                  o_ref, h_ref, mg_scr, *, tm):
    nc = 512
    branches = ((ona_ref, wna_ref), (ossd_ref, wssd_ref), (ogqa_ref, wgqa_ref))
    for c in range(D_MODEL // nc):
        cols = pl.ds(c * nc, nc)
        acc = None
        for b, (o_b, w_b) in enumerate(branches):
            p = jnp.dot(o_b[...], w_b[:, cols], preferred_element_type=F32)
            gate = _sigmoid(g_ref[:, pl.ds(b * D_MODEL + c * nc, nc)].astype(F32))
            acc = gate * p if acc is None else acc + gate * p
        mg_scr[:, cols] = acc.astype(BF16)
    for c in range(D_MODEL // nc):
        cols = pl.ds(c * nc, nc)
        o_ref[:, cols] = x_ref[:, cols] + jnp.dot(mg_scr[...], wout_ref[:, cols], preferred_element_type=F32)
    _rmsnorm_rows(o_ref, nw_ref, h_ref, tm)


def _merge(o_na, o_ssd, o_gqa, proj, x, w_na, w_ssd, w_gqa, w_out, norm_w, *, tm):
    m = x.shape[0]
    row = lambda i: (i, 0)
    const = lambda i: (0, 0)
    one = pl.Buffered(1)
    return pl.pallas_call(
        functools.partial(_merge_kernel, tm=tm),
        grid=(m // tm,),
        in_specs=[
            pl.BlockSpec((tm, NA_WIDTH), row),
            pl.BlockSpec((tm, SSD_WIDTH), row),
            pl.BlockSpec((tm, GQA_WIDTH), row),
            pl.BlockSpec((tm, GATES), row),
            pl.BlockSpec((tm, D_MODEL), row),
            pl.BlockSpec((NA_WIDTH, D_MODEL), const, pipeline_mode=one),
            pl.BlockSpec((SSD_WIDTH, D_MODEL), const, pipeline_mode=one),
            pl.BlockSpec((GQA_WIDTH, D_MODEL), const, pipeline_mode=one),
            pl.BlockSpec((D_MODEL, D_MODEL), const, pipeline_mode=one),
            pl.BlockSpec((1, D_MODEL), const),
        ],
        out_specs=[pl.BlockSpec((tm, D_MODEL), row), pl.BlockSpec((tm, D_MODEL), row)],
        out_shape=[jax.ShapeDtypeStruct((m, D_MODEL), F32), jax.ShapeDtypeStruct((m, D_MODEL), BF16)],
        scratch_shapes=[pltpu.VMEM((tm, D_MODEL), BF16)],
        compiler_params=_cp("parallel"),
        name="merge",
    )(o_na, o_ssd, o_gqa, proj, x, w_na, w_ssd, w_gqa, w_out, norm_w)


FFN_EPI_COLS = 512


def _ffn_down_kernel(*refs, tps, tm, with_norm):
    if with_norm:
        ug_ref, prev_ref, next_ref, uv_ref, cw_ref, cb_ref, wd_ref, x_ref, nw_ref, o_ref, h_ref, hid_scr = refs
    else:
        ug_ref, prev_ref, next_ref, uv_ref, cw_ref, cb_ref, wd_ref, x_ref, o_ref, hid_scr = refs
    i = pl.program_id(0)
    first = (i % tps) == 0
    last = (i % tps) == tps - 1
    nchunk = D_FF // FFN_EPI_COLS

    def gated(c):
        cols = pl.ds(c * FFN_EPI_COLS, FFN_EPI_COLS)
        g = ug_ref[:, cols].astype(F32)
        prev = jnp.where(first, 0.0, prev_ref[:, cols].astype(F32))
        nxt = jnp.where(last, 0.0, next_ref[:, cols].astype(F32))
        a = _dwconv(g, prev, nxt, cw_ref[:, cols], cb_ref[:, cols], 3)
        return (_silu(a) * uv_ref[:, cols].astype(F32)).astype(BF16)

    hid_scr[0] = gated(0)
    for c in range(nchunk):
        if c + 1 < nchunk:
            hid_scr[(c + 1) % 2] = gated(c + 1)
        part = jnp.dot(hid_scr[c % 2], wd_ref[pl.ds(c * FFN_EPI_COLS, FFN_EPI_COLS), :], preferred_element_type=F32)
        if c == 0:
            o_ref[...] = x_ref[...] + part
        else:
            o_ref[...] += part
    if with_norm:
        _rmsnorm_rows(o_ref, nw_ref, h_ref, tm)


def _ffn_down(u, conv_w, conv_b, w_down, x, next_norm_w=None, *, seq, tm):
    m = x.shape[0]
    nblk = m // HALO
    per = tm // HALO
    const = lambda i: (0, 0)
    row = lambda i: (i, 0)
    with_norm = next_norm_w is not None
    in_specs = [
        pl.BlockSpec((tm, D_FF), lambda i: (i, 0)),
        pl.BlockSpec((HALO, D_FF), lambda i: (jnp.maximum(i * per - 1, 0), 0)),
        pl.BlockSpec((HALO, D_FF), lambda i: (jnp.minimum((i + 1) * per, nblk - 1), 0)),
        pl.BlockSpec((tm, D_FF), lambda i: (i, 1)),
        pl.BlockSpec((3, D_FF), const),
        pl.BlockSpec((1, D_FF), const),
        pl.BlockSpec((D_FF, D_MODEL), const, pipeline_mode=pl.Buffered(1)),
        pl.BlockSpec((tm, D_MODEL), row),
    ]
    out_specs = [pl.BlockSpec((tm, D_MODEL), row)]
    out_shape = [jax.ShapeDtypeStruct((m, D_MODEL), F32)]
    args = [u, u, u, u, conv_w, conv_b, w_down, x]
    if with_norm:
        in_specs.append(pl.BlockSpec((1, D_MODEL), const))
        out_specs.append(pl.BlockSpec((tm, D_MODEL), row))
        out_shape.append(jax.ShapeDtypeStruct((m, D_MODEL), BF16))
        args.append(next_norm_w)
    res = pl.pallas_call(
        functools.partial(_ffn_down_kernel, tps=seq // tm, tm=tm, with_norm=with_norm),
        grid=(m // tm,),
        in_specs=in_specs,
        out_specs=out_specs,
        out_shape=out_shape,
        scratch_shapes=[pltpu.VMEM((2, tm, FFN_EPI_COLS), BF16)],
        compiler_params=_cp("parallel"),
        name="ffn_down",
    )(*args)
    return res if with_norm else (res[0], None)


MM_TN = 1024


def _tiles(bsz, seq):
    m = bsz * seq
    return dict(tm_mm=2048 if m % 2048 == 0 else 1024, tm_ew=min(seq, 512), tm_merge=min(seq, 256),
                tm_down=min(seq, 256), tq=min(seq, 512), tk=min(seq, 1024))


def _prep_layer_params(p, na_bias):
    (norm1_w, w_in, na_q_norm, na_k_norm, _, ssd_conv_w, ssd_conv_b, ssd_dt_bias, ssd_a_log, ssd_d,
     ssd_norm_w, gqa_q_norm, gqa_k_norm, w_branch, w_out, norm2_w, ffn_w_up, ffn_conv_w, ffn_conv_b, ffn_w_down) = p
    sizes = (3 * NA_WIDTH, GQA_WIDTH, GQA_KV_WIDTH, GQA_KV_WIDTH, SSD_WIDTH, SSD_XBC, 2 * SSD_HEADS, GATES)
    offs = np.concatenate([[0], np.cumsum(sizes)])
    piece = lambda k: w_in[:, offs[k]:offs[k + 1]]
    na, gq, gk, gv, z, xbc, dtw, gates = (piece(k) for k in range(8))
    w_main = jnp.concatenate([gates, xbc, na, z, gq, gk, gv], axis=1).astype(BF16)
    w_dt = jnp.pad(dtw, ((0, 0), (0, DT_PAD - 2 * SSD_HEADS))).astype(BF16)
    pad_dt = lambda v: jnp.pad(v.reshape(1, 2 * SSD_HEADS), ((0, 0), (0, DT_PAD - 2 * SSD_HEADS)))
    row = lambda v: v.reshape(1, -1)
    wb = w_branch.astype(BF16)
    return dict(
        norm1_w=row(norm1_w), w_main=w_main, w_dt=w_dt,
        na_q_norm=row(na_q_norm), na_k_norm=row(na_k_norm), na_bias=na_bias,
        conv_w=ssd_conv_w, conv_b=row(ssd_conv_b), dt_bias=pad_dt(ssd_dt_bias), a_log=pad_dt(ssd_a_log),
        d_exp=row(jnp.repeat(ssd_d, SSD_HEAD_DIM)), ssd_norm_w=row(ssd_norm_w),
        gqa_q_norm=row(gqa_q_norm), gqa_k_norm=row(gqa_k_norm),
        w_na=wb[:NA_WIDTH], w_ssd=wb[NA_WIDTH:NA_WIDTH + SSD_WIDTH], w_gqa=wb[NA_WIDTH + SSD_WIDTH:],
        w_out=w_out.astype(BF16), norm2_w=row(norm2_w), w_up=ffn_w_up.astype(BF16),
        ffn_conv_w=ffn_conv_w, ffn_conv_b=row(ffn_conv_b), w_down=ffn_w_down.astype(BF16),
    )


def _prep_params(params):
    depth = params[0].shape[0]
    na_bias = _na_bias_tables(params[4])
    return [_prep_layer_params([p[i] for p in params], na_bias[i]) for i in range(depth)]


def _layer(x, h, lp, next_norm_w, consts, *, bsz, seq):
    t = _tiles(bsz, seq)
    cos_t, sin_t, tri, expand = consts
    proj, dt_raw = _matmul(h, lp["w_main"], lp["w_dt"], tm=t["tm_mm"], tn=MM_TN)
    o_na = _na_attention(proj, lp["na_bias"], lp["na_q_norm"], lp["na_k_norm"], bsz=bsz, seq=seq)
    qn, kn, vaug = _gqa_prep(proj, cos_t, sin_t, lp["gqa_q_norm"], lp["gqa_k_norm"], seq=seq, tm=t["tm_ew"])
    o_gqa = _gqa_attention(qn, kn, vaug, bsz=bsz, seq=seq, tq=t["tq"], tk=t["tk"])
    xs, bc, dt = _ssd_prep(proj, dt_raw, lp["conv_w"], lp["conv_b"], lp["dt_bias"], seq=seq, tm=t["tm_ew"])
    yf, yb = _ssd_scan(xs, bc, dt, lp["a_log"], tri, expand, bsz=bsz, seq=seq)
    o_ssd = _ssd_finish(yf, yb, xs, proj, lp["d_exp"], lp["ssd_norm_w"], tm=t["tm_ew"])
    x, h2 = _merge(o_na, o_ssd, o_gqa, proj, x, lp["w_na"], lp["w_ssd"], lp["w_gqa"], lp["w_out"], lp["norm2_w"],
                   tm=t["tm_merge"])
    u = _matmul(h2, lp["w_up"], tm=t["tm_mm"], tn=MM_TN)
    return _ffn_down(u, lp["ffn_conv_w"], lp["ffn_conv_b"], lp["w_down"], x, next_norm_w, seq=seq, tm=t["tm_down"])


def _trunk(x, layer_params):
    bsz, seq, d = x.shape
    assert d == D_MODEL and seq % 1024 == 0 and seq // GRID_W >= NA_KROWS
    consts = _rope_tables(seq) + _ssd_constants()
    x = x.reshape(bsz * seq, d)
    h = _rmsnorm(x, layer_params[0]["norm1_w"], tm=_tiles(bsz, seq)["tm_ew"])
    for li, lp in enumerate(layer_params):
        nxt = layer_params[li + 1]["norm1_w"] if li + 1 < len(layer_params) else None
        x, h = _layer(x, h, lp, nxt, consts, bsz=bsz, seq=seq)
    return x.reshape(bsz, seq, d)


def kernel(x_prompt, x_sample, norm1_w, w_in, na_q_norm, na_k_norm, na_rpb, ssd_conv_w, ssd_conv_b, ssd_dt_bias, ssd_a_log, ssd_d, ssd_norm_w, gqa_q_norm, gqa_k_norm, w_branch, w_out, norm2_w, ffn_w_up, ffn_conv_w, ffn_conv_b, ffn_w_down):
    params = (norm1_w, w_in, na_q_norm, na_k_norm, na_rpb, ssd_conv_w, ssd_conv_b, ssd_dt_bias, ssd_a_log,
              ssd_d, ssd_norm_w, gqa_q_norm, gqa_k_norm, w_branch, w_out, norm2_w, ffn_w_up, ffn_conv_w,
              ffn_conv_b, ffn_w_down)
    layer_params = _prep_params(params)
    return (_trunk(x_prompt, layer_params), _trunk(x_sample, layer_params))
```

```python
import functools
import math

import numpy as np
import jax
import jax.numpy as jnp
from jax import lax
from jax.experimental import pallas as pl
from jax.experimental.pallas import tpu as pltpu

F32 = jnp.float32
BF16 = jnp.bfloat16

D_MODEL = 2048
GRID_W = 64
EPS = 1e-6
NA_HEADS = 4
HEAD_DIM = 128
NA_WIDTH = NA_HEADS * HEAD_DIM
NA_WIN_H = 8
NA_WIN_W = 16
SSD_HEADS = 16
SSD_HEAD_DIM = 64
SSD_WIDTH = SSD_HEADS * SSD_HEAD_DIM
SSD_STATE = 128
SSD_GROUPS = 2
SSD_CONV = 5
SSD_CHUNK = 128
SSD_CPS = 4
SSD_BC = 2 * SSD_GROUPS * SSD_STATE
SSD_XBC = SSD_WIDTH + SSD_BC
GQA_HEADS = 4
GQA_KV_HEADS = 2
GQA_WIDTH = GQA_HEADS * HEAD_DIM
GQA_KV_WIDTH = GQA_KV_HEADS * HEAD_DIM
ROPE_THETA = 10000.0
D_FF = 5632
N_BRANCH = 3
GATES = N_BRANCH * D_MODEL
DT_PAD = 128

OFF_GATES = 0
OFF_XBC = OFF_GATES + GATES
OFF_NA = OFF_XBC + SSD_XBC
OFF_Z = OFF_NA + 3 * NA_WIDTH
OFF_GQ = OFF_Z + SSD_WIDTH
OFF_GK = OFF_GQ + GQA_WIDTH
OFF_GV = OFF_GK + GQA_KV_WIDTH
PROJ_COLS = OFF_GV + GQA_KV_WIDTH

NEG = -1e30
LOG2E = math.log2(math.e)
HALO = 8
VMEM_LIMIT = 56 * 2**20


def _cp(*sem):
    return pltpu.CompilerParams(dimension_semantics=sem, vmem_limit_bytes=VMEM_LIMIT)


def _sigmoid(x):
    return 1.0 / (1.0 + jnp.exp(-x))


def _silu(x):
    return x * _sigmoid(x)


NORM_ROWS = 128


def _rmsnorm_rows(x_ref, nw_ref, h_ref, n_rows):
    def body(r, c):
        rows = pl.ds(pl.multiple_of(r * NORM_ROWS, NORM_ROWS), NORM_ROWS)
        x = x_ref[rows, :]
        ms = jnp.mean(x * x, axis=-1, keepdims=True)
        h_ref[rows, :] = (x * lax.rsqrt(ms + EPS) * nw_ref[...]).astype(BF16)
        return c

    lax.fori_loop(0, n_rows // NORM_ROWS, body, 0)


def _rmsnorm_kernel(x_ref, nw_ref, h_ref, *, tm):
    _rmsnorm_rows(x_ref, nw_ref, h_ref, tm)


def _rmsnorm(x, nw, *, tm):
    m, d = x.shape
    return pl.pallas_call(
        functools.partial(_rmsnorm_kernel, tm=tm),
        grid=(m // tm,),
        in_specs=[pl.BlockSpec((tm, d), lambda i: (i, 0)), pl.BlockSpec((1, d), lambda i: (0, 0))],
        out_specs=pl.BlockSpec((tm, d), lambda i: (i, 0)),
        out_shape=jax.ShapeDtypeStruct((m, d), BF16),
        compiler_params=_cp("parallel"),
        name="rmsnorm",
    )(x, nw)


def _mm_kernel(*refs, with_dt):
    if with_dt:
        h_ref, w_ref, wdt_ref, o_ref, odt_ref = refs

        @pl.when(pl.program_id(1) == 0)
        def _():
            odt_ref[...] = jnp.dot(h_ref[...], wdt_ref[...], preferred_element_type=F32)
    else:
        h_ref, w_ref, o_ref = refs
    o_ref[...] = jnp.dot(h_ref[...], w_ref[...], preferred_element_type=F32).astype(o_ref.dtype)


def _matmul(h, w, wdt=None, *, tm, tn):
    m, d = h.shape
    n = w.shape[1]
    with_dt = wdt is not None
    in_specs = [
        pl.BlockSpec((tm, d), lambda i, j: (i, 0)),
        pl.BlockSpec((d, tn), lambda i, j: (0, j)),
    ]
    out_shape = [jax.ShapeDtypeStruct((m, n), BF16)]
    out_specs = [pl.BlockSpec((tm, tn), lambda i, j: (i, j))]
    args = [h, w]
    if with_dt:
        in_specs.append(pl.BlockSpec((d, DT_PAD), lambda i, j: (0, 0)))
        out_shape.append(jax.ShapeDtypeStruct((m, DT_PAD), F32))
        out_specs.append(pl.BlockSpec((tm, DT_PAD), lambda i, j: (i, 0)))
        args.append(wdt)
    res = pl.pallas_call(
        functools.partial(_mm_kernel, with_dt=with_dt),
        grid=(m // tm, n // tn),
        in_specs=in_specs,
        out_specs=out_specs,
        out_shape=out_shape,
        compiler_params=_cp("parallel", "arbitrary"),
        name="matmul_dt" if with_dt else "matmul",
    )(*args)
    return res if with_dt else res[0]


NA_QROWS = 8
NA_KROWS = 16
NA_TQ = NA_QROWS * GRID_W
NA_TK = NA_KROWS * GRID_W
NA_KPIECE = 256
NA_NPIECE = NA_TK // NA_KPIECE


def _na_bias_tables(rpb):
    rows = 4 * NA_QROWS
    nb = rows // NA_QROWS
    depth = rpb.shape[0]
    c = np.arange(GRID_W)
    col_start = np.clip(c - NA_WIN_W // 2, 0, GRID_W - NA_WIN_W)
    in_win = (c[None, :] >= col_start[:, None]) & (c[None, :] < col_start[:, None] + NA_WIN_W)
    dc = np.clip(c[None, :] - c[:, None], -(NA_WIN_W - 1), NA_WIN_W - 1) + NA_WIN_W - 1
    onehot_c = (dc[:, :, None] == np.arange(2 * NA_WIN_W - 1)).astype(np.float32)
    by_col = jnp.einsum("lhab,qcb->lhaqc", rpb.astype(F32), jnp.asarray(onehot_c), precision=lax.Precision.HIGHEST)
    tables = []
    for b in (0, 1, nb - 1):
        r = NA_QROWS * b + np.arange(NA_QROWS)
        ks = int(np.clip(NA_QROWS * b - NA_WIN_H // 2, 0, rows - NA_KROWS))
        krow = ks + np.arange(NA_KROWS)
        rs = np.clip(r - NA_WIN_H // 2, 0, rows - NA_WIN_H)
        valid = (krow[None, :] >= rs[:, None]) & (krow[None, :] < rs[:, None] + NA_WIN_H)
        dr = np.clip(krow[None, :] - r[:, None], -(NA_WIN_H - 1), NA_WIN_H - 1) + NA_WIN_H - 1
        bias = jnp.take(by_col, jnp.asarray(dr.reshape(-1)), axis=2)
        bias = bias.reshape(depth, NA_HEADS, NA_QROWS, NA_KROWS, GRID_W, GRID_W).transpose(0, 1, 2, 4, 3, 5)
        mask = valid[:, None, :, None] & in_win[None, :, None, :]
        bias = jnp.where(jnp.asarray(mask)[None, None], bias * LOG2E, NEG)
        tables.append(bias.reshape(depth, NA_HEADS, NA_TQ, NA_TK))
    return jnp.stack(tables, axis=1)


def _head_rmsnorm(x, w):
    ms = jnp.mean(x * x, axis=-1, keepdims=True)
    return x * lax.rsqrt(ms + EPS) * w


NA_SUB = 128
NA_HPS = 2


def _na_kernel(q_ref, k0, k1, k2, k3, v0, v1, v2, v3, bias_ref, qw_ref, kw_ref, o_ref, kn_scr, va_scr, s_scr, *, nb):
    i = pl.program_id(2)
    kind = jnp.where(i == 0, 0, jnp.where(i == nb - 1, 2, 1))
    scale = HEAD_DIM ** -0.5 * LOG2E
    ones = jnp.ones((NA_KPIECE, HEAD_DIM), BF16)
    for hh in range(NA_HPS):
        hc = pl.ds(hh * HEAD_DIM, HEAD_DIM)
        for t, (k_ref, v_ref) in enumerate(zip((k0, k1, k2, k3), (v0, v1, v2, v3))):
            piece = pl.ds(t * NA_KPIECE, NA_KPIECE)
            kn_scr[hh, piece, :] = _head_rmsnorm(k_ref[:, hc].astype(F32), kw_ref[...]).astype(BF16)
            va_scr[hh, piece, :HEAD_DIM] = v_ref[:, hc]
            va_scr[hh, piece, HEAD_DIM:] = ones
    nlane = NA_TK // HEAD_DIM
    nsub = NA_TQ // NA_SUB

    def scores(u):
        hh, qs = divmod(u, nsub)
        rows = pl.ds(qs * NA_SUB, NA_SUB)
        qn = (_head_rmsnorm(q_ref[rows, pl.ds(hh * HEAD_DIM, HEAD_DIM)].astype(F32), qw_ref[...]) * scale).astype(BF16)
        s = lax.dot_general(qn, kn_scr[hh], (((1,), (1,)), ((), ())), preferred_element_type=F32)
        return s + bias_ref[kind, hh, rows, :]

    def finish(u, slot):
        hh, qs = divmod(u, nsub)
        blocks = [s_scr[slot, :, j * HEAD_DIM:(j + 1) * HEAD_DIM] for j in range(nlane)]
        part = blocks[0]
        for blk in blocks[1:]:
            part = jnp.maximum(part, blk)
        m = jnp.broadcast_to(jnp.max(part, axis=-1, keepdims=True), part.shape)
        p = jnp.concatenate([jnp.exp2(blk - m).astype(BF16) for blk in blocks], axis=1)
        pv = jnp.dot(p, va_scr[hh], preferred_element_type=F32)
        o_ref[pl.ds(qs * NA_SUB, NA_SUB), pl.ds(hh * HEAD_DIM, HEAD_DIM)] = (
            pv[:, :HEAD_DIM] / pv[:, HEAD_DIM:]).astype(o_ref.dtype)

    units = NA_HPS * nsub
    s_scr[0] = scores(0)
    for u in range(units):
        if u + 1 < units:
            s_scr[(u + 1) % 2] = scores(u + 1)
        finish(u, u % 2)


def _na_attention(proj, bias, qw, kw, *, bsz, seq):
    m = proj.shape[0]
    rows = seq // GRID_W
    nb = rows // NA_QROWS
    npieces = seq // NA_KPIECE
    width = NA_HPS * HEAD_DIM
    q_col = OFF_NA // width
    k_col = (OFF_NA + NA_WIDTH) // width
    v_col = (OFF_NA + 2 * NA_WIDTH) // width

    def kbase(i):
        return jnp.clip(2 * i - 1, 0, npieces - NA_NPIECE)

    def kv_spec(col, t):
        return pl.BlockSpec((NA_KPIECE, width), lambda h, b, i: (b * npieces + kbase(i) + t, col + h))

    in_specs = [pl.BlockSpec((NA_TQ, width), lambda h, b, i: (b * nb + i, q_col + h))]
    in_specs += [kv_spec(k_col, t) for t in range(NA_NPIECE)]
    in_specs += [kv_spec(v_col, t) for t in range(NA_NPIECE)]
    in_specs += [
        pl.BlockSpec((3, NA_HPS, NA_TQ, NA_TK), lambda h, b, i: (0, h, 0, 0), pipeline_mode=pl.Buffered(1)),
        pl.BlockSpec((1, HEAD_DIM), lambda h, b, i: (0, 0)),
        pl.BlockSpec((1, HEAD_DIM), lambda h, b, i: (0, 0)),
    ]
    return pl.pallas_call(
        functools.partial(_na_kernel, nb=nb),
        grid=(NA_HEADS // NA_HPS, bsz, nb),
        in_specs=in_specs,
        out_specs=pl.BlockSpec((NA_TQ, width), lambda h, b, i: (b * nb + i, h)),
        out_shape=jax.ShapeDtypeStruct((m, NA_WIDTH), BF16),
        scratch_shapes=[
            pltpu.VMEM((NA_HPS, NA_TK, HEAD_DIM), BF16),
            pltpu.VMEM((NA_HPS, NA_TK, 2 * HEAD_DIM), BF16),
            pltpu.VMEM((2, NA_SUB, NA_TK), F32),
        ],
        compiler_params=_cp("parallel", "parallel", "parallel"),
        name="na_attention",
    )(proj, *([proj] * (2 * NA_NPIECE)), bias, qw, kw)


def _rope_tables(seq):
    t = np.arange(seq)
    pos = np.stack([t // GRID_W, t % GRID_W], axis=-1).astype(np.float32)
    n_freq = HEAD_DIM // 4
    inv_freq = jnp.asarray(ROPE_THETA, F32) ** (-jnp.arange(n_freq, dtype=F32) / n_freq)
    ang = jnp.asarray(pos)[:, :, None] * inv_freq
    cos = jnp.cos(ang)
    sin = jnp.sin(ang)
    cos_t = jnp.concatenate([cos[:, 0], cos[:, 0], cos[:, 1], cos[:, 1]], axis=-1)
    sin_t = jnp.concatenate([-sin[:, 0], sin[:, 0], -sin[:, 1], sin[:, 1]], axis=-1)
    return cos_t, sin_t


def _rope(x, cos_t, sin_t, first_half):
    swapped = jnp.where(first_half, pltpu.roll(x, HEAD_DIM - 32, axis=1), pltpu.roll(x, 32, axis=1))
    return x * cos_t + swapped * sin_t


def _gqa_prep_kernel(q_ref, k_ref, v_ref, cos_ref, sin_ref, qw_ref, kw_ref, qo_ref, ko_ref, vo_ref):
    cos_t = cos_ref[...]
    sin_t = sin_ref[...]
    lane = lax.broadcasted_iota(jnp.int32, cos_t.shape, 1)
    first_half = (lane % 64) < 32
    scale = HEAD_DIM ** -0.5 * LOG2E
    for h in range(GQA_HEADS):
        cols = pl.ds(h * HEAD_DIM, HEAD_DIM)
        x = _head_rmsnorm(q_ref[:, cols].astype(F32), qw_ref[...])
        qo_ref[:, cols] = (_rope(x, cos_t, sin_t, first_half) * scale).astype(BF16)
    ones = jnp.ones(cos_t.shape, BF16)
    for h in range(GQA_KV_HEADS):
        cols = pl.ds(h * HEAD_DIM, HEAD_DIM)
        x = _head_rmsnorm(k_ref[:, cols].astype(F32), kw_ref[...])
        ko_ref[:, cols] = _rope(x, cos_t, sin_t, first_half).astype(BF16)
        vo_ref[:, pl.ds(2 * h * HEAD_DIM, HEAD_DIM)] = v_ref[:, cols]
        vo_ref[:, pl.ds((2 * h + 1) * HEAD_DIM, HEAD_DIM)] = ones


def _gqa_prep(proj, cos_t, sin_t, qw, kw, *, seq, tm):
    m = proj.shape[0]
    tps = seq // tm
    return pl.pallas_call(
        _gqa_prep_kernel,
        grid=(m // tm,),
        in_specs=[
            pl.BlockSpec((tm, GQA_WIDTH), lambda i: (i, OFF_GQ // GQA_WIDTH)),
            pl.BlockSpec((tm, GQA_KV_WIDTH), lambda i: (i, OFF_GK // GQA_KV_WIDTH)),
            pl.BlockSpec((tm, GQA_KV_WIDTH), lambda i: (i, OFF_GV // GQA_KV_WIDTH)),
            pl.BlockSpec((tm, HEAD_DIM), lambda i: (i % tps, 0)),
            pl.BlockSpec((tm, HEAD_DIM), lambda i: (i % tps, 0)),
            pl.BlockSpec((1, HEAD_DIM), lambda i: (0, 0)),
            pl.BlockSpec((1, HEAD_DIM), lambda i: (0, 0)),
        ],
        out_specs=[
            pl.BlockSpec((tm, GQA_WIDTH), lambda i: (i, 0)),
            pl.BlockSpec((tm, GQA_KV_WIDTH), lambda i: (i, 0)),
            pl.BlockSpec((tm, 2 * GQA_KV_WIDTH), lambda i: (i, 0)),
        ],
        out_shape=[
            jax.ShapeDtypeStruct((m, GQA_WIDTH), BF16),
            jax.ShapeDtypeStruct((m, GQA_KV_WIDTH), BF16),
            jax.ShapeDtypeStruct((m, 2 * GQA_KV_WIDTH), BF16),
        ],
        compiler_params=_cp("parallel"),
        name="gqa_prep",
    )(proj, proj, proj, cos_t, sin_t, qw, kw)


def _flash_kernel(q_ref, k_ref, v_ref, o_ref, s_scr, m_scr, acc_scr, *, tq, tk, seq):
    q2 = jnp.concatenate([q_ref[:, :HEAD_DIM], q_ref[:, HEAD_DIM:]], axis=0)
    m_scr[...] = jnp.full(m_scr.shape, NEG, F32)
    acc_scr[...] = jnp.zeros(acc_scr.shape, F32)
    nblk = tk // HEAD_DIM
    n = seq // tk

    def chunk(kc):
        return pl.ds(kc * tk, tk)

    def scores(kc):
        return lax.dot_general(q2, k_ref[chunk(kc), :], (((1,), (1,)), ((), ())), preferred_element_type=F32)

    def softmax_pv(kc, slot):
        blocks = [s_scr[slot, :, j * HEAD_DIM:(j + 1) * HEAD_DIM] for j in range(nblk)]
        part = blocks[0]
        for blk in blocks[1:]:
            part = jnp.maximum(part, blk)
        m_prev = m_scr[...]
        m_new = jnp.maximum(m_prev, jnp.max(part, axis=-1, keepdims=True))
        alpha = jnp.exp2(m_prev - m_new)
        p = jnp.concatenate([jnp.exp2(blk - m_new).astype(BF16) for blk in blocks], axis=1)
        pv = jnp.dot(p, v_ref[chunk(kc), :], preferred_element_type=F32)
        acc_scr[...] = jnp.concatenate([alpha, alpha], axis=1) * acc_scr[...] + pv
        m_scr[...] = m_new

    s_scr[0] = scores(0)

    for kc in range(n):
        if kc + 1 < n:
            s_scr[(kc + 1) % 2] = scores(kc + 1)
        softmax_pv(kc, kc % 2)
    o = acc_scr[:, :HEAD_DIM] / acc_scr[:, HEAD_DIM:]
    o_ref[:, :HEAD_DIM] = o[:tq].astype(o_ref.dtype)
    o_ref[:, HEAD_DIM:] = o[tq:].astype(o_ref.dtype)


def _gqa_attention(qn, kn, vaug, *, bsz, seq, tq, tk):
    m = qn.shape[0]
    nq = seq // tq
    rep = GQA_HEADS // GQA_KV_HEADS
    return pl.pallas_call(
        functools.partial(_flash_kernel, tq=tq, tk=tk, seq=seq),
        grid=(bsz, GQA_KV_HEADS, nq),
        in_specs=[
            pl.BlockSpec((tq, rep * HEAD_DIM), lambda b, g, i: (b * nq + i, g)),
            pl.BlockSpec((seq, HEAD_DIM), lambda b, g, i: (b, g)),
            pl.BlockSpec((seq, 2 * HEAD_DIM), lambda b, g, i: (b, g)),
        ],
        out_specs=pl.BlockSpec((tq, rep * HEAD_DIM), lambda b, g, i: (b * nq + i, g)),
        out_shape=jax.ShapeDtypeStruct((m, GQA_WIDTH), BF16),
        scratch_shapes=[
            pltpu.VMEM((2, rep * tq, tk), F32),
            pltpu.VMEM((rep * tq, HEAD_DIM), F32),
            pltpu.VMEM((rep * tq, 2 * HEAD_DIM), F32),
        ],
        compiler_params=_cp("parallel", "parallel", "parallel"),
        name="gqa_flash",
    )(qn, kn, vaug)


def _with_halo(x, prev, nxt):
    return jnp.concatenate([x, nxt, prev], axis=0)


def _shift_rows(xcat, s, n):
    return pltpu.roll(xcat, s % xcat.shape[0], axis=0)[:n]


def _dwconv(x, prev, nxt, w, b, width):
    n = x.shape[0]
    xcat = _with_halo(x, prev, nxt)
    pad = width // 2
    out = b + x * w[pad:pad + 1, :]
    for i in range(width):
        if i != pad:
            out = out + _shift_rows(xcat, pad - i, n) * w[i:i + 1, :]
    return out


def _halo_specs(tm, width, col_block, m):
    nblk = m // HALO
    per = tm // HALO
    prev = pl.BlockSpec((HALO, width), lambda i, *_: (jnp.maximum(i * per - 1, 0), col_block))
    nxt = pl.BlockSpec((HALO, width), lambda i, *_: (jnp.minimum((i + 1) * per, nblk - 1), col_block))
    return prev, nxt


def _ssd_prep_kernel(x_ref, prev_ref, next_ref, dtr_ref, cw_ref, cb_ref, dtb_ref, xs_ref, bc_ref, dt_ref, *, tps):
    i = pl.program_id(0)
    first = (i % tps) == 0
    last = (i % tps) == tps - 1
    x = x_ref[...].astype(F32)
    prev = jnp.where(first, 0.0, prev_ref[...].astype(F32))
    nxt = jnp.where(last, 0.0, next_ref[...].astype(F32))
    y = _silu(_dwconv(x, prev, nxt, cw_ref[...], cb_ref[...], SSD_CONV))
    xs_ref[...] = y[:, :SSD_WIDTH].astype(BF16)
    bc_ref[...] = y[:, SSD_WIDTH:].astype(BF16)
    v = dtr_ref[...] + dtb_ref[...]
    dt_ref[...] = jnp.maximum(v, 0.0) + jnp.log(1.0 + jnp.exp(-jnp.abs(v)))


def _ssd_prep(proj, dt_raw, conv_w, conv_b, dt_bias, *, seq, tm):
    m = proj.shape[0]
    cb = OFF_XBC // SSD_XBC
    prev, nxt = _halo_specs(tm, SSD_XBC, cb, m)
    return pl.pallas_call(
        functools.partial(_ssd_prep_kernel, tps=seq // tm),
        grid=(m // tm,),
        in_specs=[
            pl.BlockSpec((tm, SSD_XBC), lambda i: (i, cb)),
            prev,
            nxt,
            pl.BlockSpec((tm, DT_PAD), lambda i: (i, 0)),
            pl.BlockSpec((SSD_CONV, SSD_XBC), lambda i: (0, 0)),
            pl.BlockSpec((1, SSD_XBC), lambda i: (0, 0)),
            pl.BlockSpec((1, DT_PAD), lambda i: (0, 0)),
        ],
        out_specs=[
            pl.BlockSpec((tm, SSD_WIDTH), lambda i: (i, 0)),
            pl.BlockSpec((tm, SSD_BC), lambda i: (i, 0)),
            pl.BlockSpec((tm, DT_PAD), lambda i: (i, 0)),
        ],
        out_shape=[
            jax.ShapeDtypeStruct((m, SSD_WIDTH), BF16),
            jax.ShapeDtypeStruct((m, SSD_BC), BF16),
            jax.ShapeDtypeStruct((m, DT_PAD), F32),
        ],
        compiler_params=_cp("parallel"),
        name="ssd_prep",
    )(proj, proj, proj, dt_raw, conv_w, conv_b, dt_bias)


def _ssd_constants():
    q = SSD_CHUNK
    li = np.arange(q)
    tri = np.stack([li[:, None] >= li[None, :], li[:, None] <= li[None, :]]).astype(np.float32)
    hp = np.arange(SSD_WIDTH) // SSD_HEAD_DIM
    expand = np.zeros((2, 2 * DT_PAD, SSD_WIDTH), np.float32)
    for d in range(2):
        expand[d, d * SSD_HEADS + hp, np.arange(SSD_WIDTH)] = 1.0
        expand[d, DT_PAD + d * SSD_HEADS + hp, np.arange(SSD_WIDTH)] = 1.0
    return jnp.asarray(tri, F32), jnp.asarray(expand, BF16)


def _expand_heads(v, e):
    hi = v.astype(BF16)
    lo = (v - hi.astype(F32)).astype(BF16)
    return jnp.dot(jnp.concatenate([hi, lo], axis=1), e, preferred_element_type=F32)


def _ssd_scan_kernel(xs_f, bc_f, dt_f, xs_b, bc_b, dt_b, alog_ref, tri_ref, exp_ref, yf_ref, yb_ref, h_scr):
    @pl.when(pl.program_id(1) == 0)
    def _():
        h_scr[...] = jnp.zeros(h_scr.shape, F32)

    q = SSD_CHUNK
    gw = SSD_WIDTH // SSD_GROUPS
    a_row = -jnp.exp(alog_ref[...])
    li = lax.broadcasted_iota(jnp.int32, (q, q), 0)
    si = lax.broadcasted_iota(jnp.int32, (q, q), 1)
    lane = lax.broadcasted_iota(jnp.int32, (q, HEAD_DIM), 1)
    left = lane < SSD_HEAD_DIM
    dirs = ((xs_f, bc_f, dt_f, yf_ref, q - 1), (xs_b, bc_b, dt_b, yb_ref, 0))
    for cc, (d, (xs_ref, bc_ref, dt_ref, y_ref, last)) in [(cc, dd) for cc in range(SSD_CPS) for dd in enumerate(dirs)]:
        rows = pl.ds((cc if d == 0 else SSD_CPS - 1 - cc) * q, q)
        keep = (li >= si) if d == 0 else (li <= si)
        e = exp_ref[d]
        dt = dt_ref[rows, :]
        da = dt * a_row
        cum = jnp.dot(tri_ref[d], da, precision=lax.Precision.HIGHEST, preferred_element_type=F32)
        cum_t = cum.T
        tot = cum[last:last + 1, :]
        stack_e = _expand_heads(jnp.concatenate([dt, jnp.exp(cum), jnp.exp(tot - cum)], axis=0), e)
        dt_e, ec_e, w_e = stack_e[:q], stack_e[q:2 * q], stack_e[2 * q:]
        dec_e = ec_e[last:last + 1, :]
        xd = xs_ref[rows, :].astype(F32) * dt_e
        xdb = xd.astype(BF16)
        xw = (xd * w_e).astype(BF16)
        for g in range(SSD_GROUPS):
            gl = slice(g * gw, (g + 1) * gw)
            bg = bc_ref[rows, g * SSD_STATE:(g + 1) * SSD_STATE]
            cg = bc_ref[rows, (SSD_GROUPS + g) * SSD_STATE:(SSD_GROUPS + g + 1) * SSD_STATE]
            cb = lax.dot_general(cg, bg, (((1,), (1,)), ((), ())), preferred_element_type=F32)
            bg_t = bg.astype(F32).T.astype(BF16)
            h_prev = h_scr[d, g]
            y_off = jnp.dot(cg, h_prev.astype(BF16), preferred_element_type=F32) * ec_e[:, gl]
            states_t = jnp.dot(bg_t, xw[:, gl], preferred_element_type=F32)
            h_scr[d, g] = h_prev * dec_e[:, gl] + states_t
            for j in range(gw // HEAD_DIM):
                mats = []
                for hh in range(2):
                    col = d * SSD_HEADS + g * (SSD_HEADS // SSD_GROUPS) + 2 * j + hh
                    seg = cum[:, col:col + 1] - cum_t[col:col + 1, :]
                    lm = jnp.exp(jnp.where(keep, seg, NEG))
                    mats.append((cb * lm).astype(BF16))
                lo = g * gw + j * HEAD_DIM
                slab = xdb[:, lo:lo + HEAD_DIM]
                zero = jnp.zeros_like(slab)
                wbd = jnp.concatenate([jnp.where(left, slab, zero), jnp.where(left, zero, slab)], axis=0)
                y_diag = jnp.dot(jnp.concatenate(mats, axis=1), wbd, preferred_element_type=F32)
                y_ref[rows, lo:lo + HEAD_DIM] = (y_diag + y_off[:, j * HEAD_DIM:(j + 1) * HEAD_DIM]).astype(y_ref.dtype)


def _ssd_scan(xs, bc, dt, a_log, tri, expand, *, bsz, seq):
    m = xs.shape[0]
    nc = seq // (SSD_CPS * SSD_CHUNK)
    blk = SSD_CPS * SSD_CHUNK
    fwd = lambda b, c: (b * nc + c, 0)
    bwd = lambda b, c: (b * nc + nc - 1 - c, 0)
    return pl.pallas_call(
        _ssd_scan_kernel,
        grid=(bsz, nc),
        in_specs=[
            pl.BlockSpec((blk,SSD_WIDTH), fwd),
            pl.BlockSpec((blk,SSD_BC), fwd),
            pl.BlockSpec((blk,DT_PAD), fwd),
            pl.BlockSpec((blk,SSD_WIDTH), bwd),
            pl.BlockSpec((blk,SSD_BC), bwd),
            pl.BlockSpec((blk,DT_PAD), bwd),
            pl.BlockSpec((1, DT_PAD), lambda b, c: (0, 0)),
            pl.BlockSpec((2, SSD_CHUNK, SSD_CHUNK), lambda b, c: (0, 0, 0)),
            pl.BlockSpec((2, 2 * DT_PAD, SSD_WIDTH), lambda b, c: (0, 0, 0)),
        ],
        out_specs=[
            pl.BlockSpec((blk,SSD_WIDTH), fwd),
            pl.BlockSpec((blk,SSD_WIDTH), bwd),
        ],
        out_shape=[
            jax.ShapeDtypeStruct((m, SSD_WIDTH), BF16),
            jax.ShapeDtypeStruct((m, SSD_WIDTH), BF16),
        ],
        scratch_shapes=[pltpu.VMEM((2, SSD_GROUPS, SSD_STATE, SSD_WIDTH // SSD_GROUPS), F32)],
        compiler_params=_cp("parallel", "arbitrary"),
        name="ssd_scan",
    )(xs, bc, dt, xs, bc, dt, a_log, tri, expand)


def _ssd_finish_kernel(yf_ref, yb_ref, xs_ref, z_ref, d_ref, nw_ref, o_ref):
    y = yf_ref[...].astype(F32) + yb_ref[...].astype(F32) + xs_ref[...].astype(F32) * d_ref[...]
    y = y * _silu(z_ref[...].astype(F32))
    gw = SSD_WIDTH // SSD_GROUPS
    for g in range(SSD_GROUPS):
        yg = y[:, g * gw:(g + 1) * gw]
        ms = jnp.mean(yg * yg, axis=-1, keepdims=True)
        o_ref[:, g * gw:(g + 1) * gw] = (yg * lax.rsqrt(ms + EPS) * nw_ref[:, g * gw:(g + 1) * gw]).astype(o_ref.dtype)


def _ssd_finish(yf, yb, xs, proj, d_exp, norm_w, *, tm):
    m = yf.shape[0]
    row = lambda i: (i, 0)
    const = lambda i: (0, 0)
    return pl.pallas_call(
        _ssd_finish_kernel,
        grid=(m // tm,),
        in_specs=[
            pl.BlockSpec((tm, SSD_WIDTH), row),
            pl.BlockSpec((tm, SSD_WIDTH), row),
            pl.BlockSpec((tm, SSD_WIDTH), row),
            pl.BlockSpec((tm, SSD_WIDTH), lambda i: (i, OFF_Z // SSD_WIDTH)),
            pl.BlockSpec((1, SSD_WIDTH), const),
            pl.BlockSpec((1, SSD_WIDTH), const),
        ],
        out_specs=pl.BlockSpec((tm, SSD_WIDTH), row),
        out_shape=jax.ShapeDtypeStruct((m, SSD_WIDTH), BF16),
        compiler_params=_cp("parallel"),
        name="ssd_finish",
    )(yf, yb, xs, proj, d_exp, norm_w)


def _merge_kernel(ona_ref, ossd_ref, ogqa_ref, g_ref, x_ref, wna_ref, wssd_ref, wgqa_ref, wout_ref, nw_ref,
                  o_ref, h_ref, mg_scr, *, tm):
    nc = 512
    branches = ((ona_ref, wna_ref), (ossd_ref, wssd_ref), (ogqa_ref, wgqa_ref))
    for c in range(D_MODEL // nc):
        cols = pl.ds(c * nc, nc)
        acc = None
        for b, (o_b, w_b) in enumerate(branches):
            p = jnp.dot(o_b[...], w_b[:, cols], preferred_element_type=F32)
            gate = _sigmoid(g_ref[:, pl.ds(b * D_MODEL + c * nc, nc)].astype(F32))
            acc = gate * p if acc is None else acc + gate * p
        mg_scr[:, cols] = acc.astype(BF16)
    for c in range(D_MODEL // nc):
        cols = pl.ds(c * nc, nc)
        o_ref[:, cols] = x_ref[:, cols] + jnp.dot(mg_scr[...], wout_ref[:, cols], preferred_element_type=F32)
    _rmsnorm_rows(o_ref, nw_ref, h_ref, tm)


def _merge(o_na, o_ssd, o_gqa, proj, x, w_na, w_ssd, w_gqa, w_out, norm_w, *, tm):
    m = x.shape[0]
    row = lambda i: (i, 0)
    const = lambda i: (0, 0)
    one = pl.Buffered(1)
    return pl.pallas_call(
        functools.partial(_merge_kernel, tm=tm),
        grid=(m // tm,),
        in_specs=[
            pl.BlockSpec((tm, NA_WIDTH), row),
            pl.BlockSpec((tm, SSD_WIDTH), row),
            pl.BlockSpec((tm, GQA_WIDTH), row),
            pl.BlockSpec((tm, GATES), row),
            pl.BlockSpec((tm, D_MODEL), row),
            pl.BlockSpec((NA_WIDTH, D_MODEL), const, pipeline_mode=one),
            pl.BlockSpec((SSD_WIDTH, D_MODEL), const, pipeline_mode=one),
            pl.BlockSpec((GQA_WIDTH, D_MODEL), const, pipeline_mode=one),
            pl.BlockSpec((D_MODEL, D_MODEL), const, pipeline_mode=one),
            pl.BlockSpec((1, D_MODEL), const),
        ],
        out_specs=[pl.BlockSpec((tm, D_MODEL), row), pl.BlockSpec((tm, D_MODEL), row)],
        out_shape=[jax.ShapeDtypeStruct((m, D_MODEL), F32), jax.ShapeDtypeStruct((m, D_MODEL), BF16)],
        scratch_shapes=[pltpu.VMEM((tm, D_MODEL), BF16)],
        compiler_params=_cp("parallel"),
        name="merge",
    )(o_na, o_ssd, o_gqa, proj, x, w_na, w_ssd, w_gqa, w_out, norm_w)


FFN_EPI_COLS = 512


def _ffn_down_kernel(*refs, tps, tm, with_norm):
    if with_norm:
        ug_ref, prev_ref, next_ref, uv_ref, cw_ref, cb_ref, wd_ref, x_ref, nw_ref, o_ref, h_ref, hid_scr = refs
    else:
        ug_ref, prev_ref, next_ref, uv_ref, cw_ref, cb_ref, wd_ref, x_ref, o_ref, hid_scr = refs
    i = pl.program_id(0)
    first = (i % tps) == 0
    last = (i % tps) == tps - 1
    nchunk = D_FF // FFN_EPI_COLS

    def gated(c):
        cols = pl.ds(c * FFN_EPI_COLS, FFN_EPI_COLS)
        g = ug_ref[:, cols].astype(F32)
        prev = jnp.where(first, 0.0, prev_ref[:, cols].astype(F32))
        nxt = jnp.where(last, 0.0, next_ref[:, cols].astype(F32))
        a = _dwconv(g, prev, nxt, cw_ref[:, cols], cb_ref[:, cols], 3)
        return (_silu(a) * uv_ref[:, cols].astype(F32)).astype(BF16)

    hid_scr[0] = gated(0)
    for c in range(nchunk):
        if c + 1 < nchunk:
            hid_scr[(c + 1) % 2] = gated(c + 1)
        part = jnp.dot(hid_scr[c % 2], wd_ref[pl.ds(c * FFN_EPI_COLS, FFN_EPI_COLS), :], preferred_element_type=F32)
        if c == 0:
            o_ref[...] = x_ref[...] + part
        else:
            o_ref[...] += part
    if with_norm:
        _rmsnorm_rows(o_ref, nw_ref, h_ref, tm)


def _ffn_down(u, conv_w, conv_b, w_down, x, next_norm_w=None, *, seq, tm):
    m = x.shape[0]
    nblk = m // HALO
    per = tm // HALO
    const = lambda i: (0, 0)
    row = lambda i: (i, 0)
    with_norm = next_norm_w is not None
    in_specs = [
        pl.BlockSpec((tm, D_FF), lambda i: (i, 0)),
        pl.BlockSpec((HALO, D_FF), lambda i: (jnp.maximum(i * per - 1, 0), 0)),
        pl.BlockSpec((HALO, D_FF), lambda i: (jnp.minimum((i + 1) * per, nblk - 1), 0)),
        pl.BlockSpec((tm, D_FF), lambda i: (i, 1)),
        pl.BlockSpec((3, D_FF), const),
        pl.BlockSpec((1, D_FF), const),
        pl.BlockSpec((D_FF, D_MODEL), const, pipeline_mode=pl.Buffered(1)),
        pl.BlockSpec((tm, D_MODEL), row),
    ]
    out_specs = [pl.BlockSpec((tm, D_MODEL), row)]
    out_shape = [jax.ShapeDtypeStruct((m, D_MODEL), F32)]
    args = [u, u, u, u, conv_w, conv_b, w_down, x]
    if with_norm:
        in_specs.append(pl.BlockSpec((1, D_MODEL), const))
        out_specs.append(pl.BlockSpec((tm, D_MODEL), row))
        out_shape.append(jax.ShapeDtypeStruct((m, D_MODEL), BF16))
        args.append(next_norm_w)
    res = pl.pallas_call(
        functools.partial(_ffn_down_kernel, tps=seq // tm, tm=tm, with_norm=with_norm),
        grid=(m // tm,),
        in_specs=in_specs,
        out_specs=out_specs,
        out_shape=out_shape,
        scratch_shapes=[pltpu.VMEM((2, tm, FFN_EPI_COLS), BF16)],
        compiler_params=_cp("parallel"),
        name="ffn_down",
    )(*args)
    return res if with_norm else (res[0], None)


MM_TN = 1024


def _tiles(bsz, seq):
    m = bsz * seq
    return dict(tm_mm=2048 if m % 2048 == 0 else 1024, tm_ew=min(seq, 1024), tm_merge=min(seq, 256),
                tm_down=min(seq, 256), tq=min(seq, 512), tk=min(seq, 1024))


def _prep_layer_params(p, na_bias):
    (norm1_w, w_in, na_q_norm, na_k_norm, _, ssd_conv_w, ssd_conv_b, ssd_dt_bias, ssd_a_log, ssd_d,
     ssd_norm_w, gqa_q_norm, gqa_k_norm, w_branch, w_out, norm2_w, ffn_w_up, ffn_conv_w, ffn_conv_b, ffn_w_down) = p
    sizes = (3 * NA_WIDTH, GQA_WIDTH, GQA_KV_WIDTH, GQA_KV_WIDTH, SSD_WIDTH, SSD_XBC, 2 * SSD_HEADS, GATES)
    offs = np.concatenate([[0], np.cumsum(sizes)])
    piece = lambda k: w_in[:, offs[k]:offs[k + 1]]
    na, gq, gk, gv, z, xbc, dtw, gates = (piece(k) for k in range(8))
    w_main = jnp.concatenate([gates, xbc, na, z, gq, gk, gv], axis=1).astype(BF16)
    w_dt = jnp.pad(dtw, ((0, 0), (0, DT_PAD - 2 * SSD_HEADS))).astype(BF16)
    pad_dt = lambda v: jnp.pad(v.reshape(1, 2 * SSD_HEADS), ((0, 0), (0, DT_PAD - 2 * SSD_HEADS)))
    row = lambda v: v.reshape(1, -1)
    wb = w_branch.astype(BF16)
    return dict(
        norm1_w=row(norm1_w), w_main=w_main, w_dt=w_dt,
        na_q_norm=row(na_q_norm), na_k_norm=row(na_k_norm), na_bias=na_bias,
        conv_w=ssd_conv_w, conv_b=row(ssd_conv_b), dt_bias=pad_dt(ssd_dt_bias), a_log=pad_dt(ssd_a_log),
        d_exp=row(jnp.repeat(ssd_d, SSD_HEAD_DIM)), ssd_norm_w=row(ssd_norm_w),
        gqa_q_norm=row(gqa_q_norm), gqa_k_norm=row(gqa_k_norm),
        w_na=wb[:NA_WIDTH], w_ssd=wb[NA_WIDTH:NA_WIDTH + SSD_WIDTH], w_gqa=wb[NA_WIDTH + SSD_WIDTH:],
        w_out=w_out.astype(BF16), norm2_w=row(norm2_w), w_up=ffn_w_up.astype(BF16),
        ffn_conv_w=ffn_conv_w, ffn_conv_b=row(ffn_conv_b), w_down=ffn_w_down.astype(BF16),
    )


def _prep_params(params):
    depth = params[0].shape[0]
    na_bias = _na_bias_tables(params[4])
    return [_prep_layer_params([p[i] for p in params], na_bias[i]) for i in range(depth)]


def _layer(x, h, lp, next_norm_w, consts, *, bsz, seq):
    t = _tiles(bsz, seq)
    cos_t, sin_t, tri, expand = consts
    proj, dt_raw = _matmul(h, lp["w_main"], lp["w_dt"], tm=t["tm_mm"], tn=MM_TN)
    o_na = _na_attention(proj, lp["na_bias"], lp["na_q_norm"], lp["na_k_norm"], bsz=bsz, seq=seq)
    qn, kn, vaug = _gqa_prep(proj, cos_t, sin_t, lp["gqa_q_norm"], lp["gqa_k_norm"], seq=seq, tm=t["tm_ew"])
    o_gqa = _gqa_attention(qn, kn, vaug, bsz=bsz, seq=seq, tq=t["tq"], tk=t["tk"])
    xs, bc, dt = _ssd_prep(proj, dt_raw, lp["conv_w"], lp["conv_b"], lp["dt_bias"], seq=seq, tm=t["tm_ew"])
    yf, yb = _ssd_scan(xs, bc, dt, lp["a_log"], tri, expand, bsz=bsz, seq=seq)
    o_ssd = _ssd_finish(yf, yb, xs, proj, lp["d_exp"], lp["ssd_norm_w"], tm=t["tm_ew"])
    x, h2 = _merge(o_na, o_ssd, o_gqa, proj, x, lp["w_na"], lp["w_ssd"], lp["w_gqa"], lp["w_out"], lp["norm2_w"],
                   tm=t["tm_merge"])
    u = _matmul(h2, lp["w_up"], tm=t["tm_mm"], tn=MM_TN)
    return _ffn_down(u, lp["ffn_conv_w"], lp["ffn_conv_b"], lp["w_down"], x, next_norm_w, seq=seq, tm=t["tm_down"])


def _trunk(x, layer_params):
    bsz, seq, d = x.shape
    assert d == D_MODEL and seq % 1024 == 0 and seq // GRID_W >= NA_KROWS
    consts = _rope_tables(seq) + _ssd_constants()
    x = x.reshape(bsz * seq, d)
    h = _rmsnorm(x, layer_params[0]["norm1_w"], tm=_tiles(bsz, seq)["tm_ew"])
    for li, lp in enumerate(layer_params):
        nxt = layer_params[li + 1]["norm1_w"] if li + 1 < len(layer_params) else None
        x, h = _layer(x, h, lp, nxt, consts, bsz=bsz, seq=seq)
    return x.reshape(bsz, seq, d)


def kernel(x_prompt, x_sample, norm1_w, w_in, na_q_norm, na_k_norm, na_rpb, ssd_conv_w, ssd_conv_b, ssd_dt_bias, ssd_a_log, ssd_d, ssd_norm_w, gqa_q_norm, gqa_k_norm, w_branch, w_out, norm2_w, ffn_w_up, ffn_conv_w, ffn_conv_b, ffn_w_down):
    params = (norm1_w, w_in, na_q_norm, na_k_norm, na_rpb, ssd_conv_w, ssd_conv_b, ssd_dt_bias, ssd_a_log,
              ssd_d, ssd_norm_w, gqa_q_norm, gqa_k_norm, w_branch, w_out, norm2_w, ffn_w_up, ffn_conv_w,
              ffn_conv_b, ffn_w_down)
    layer_params = _prep_params(params)
    return (_trunk(x_prompt, layer_params), _trunk(x_sample, layer_params))
```

```python
import functools
import math

import numpy as np
import jax
import jax.numpy as jnp
from jax import lax
from jax.experimental import pallas as pl
from jax.experimental.pallas import tpu as pltpu

F32 = jnp.float32
BF16 = jnp.bfloat16

D_MODEL = 2048
GRID_W = 64
EPS = 1e-6
NA_HEADS = 4
HEAD_DIM = 128
NA_WIDTH = NA_HEADS * HEAD_DIM
NA_WIN_H = 8
NA_WIN_W = 16
SSD_HEADS = 16
SSD_HEAD_DIM = 64
SSD_WIDTH = SSD_HEADS * SSD_HEAD_DIM
SSD_STATE = 128
SSD_GROUPS = 2
SSD_CONV = 5
SSD_CHUNK = 128
SSD_CPS = 4
SSD_BC = 2 * SSD_GROUPS * SSD_STATE
SSD_XBC = SSD_WIDTH + SSD_BC
GQA_HEADS = 4
GQA_KV_HEADS = 2
GQA_WIDTH = GQA_HEADS * HEAD_DIM
GQA_KV_WIDTH = GQA_KV_HEADS * HEAD_DIM
ROPE_THETA = 10000.0
D_FF = 5632
N_BRANCH = 3
GATES = N_BRANCH * D_MODEL
DT_PAD = 128

OFF_GATES = 0
OFF_XBC = OFF_GATES + GATES
OFF_NA = OFF_XBC + SSD_XBC
OFF_Z = OFF_NA + 3 * NA_WIDTH
OFF_GQ = OFF_Z + SSD_WIDTH
OFF_GK = OFF_GQ + GQA_WIDTH
OFF_GV = OFF_GK + GQA_KV_WIDTH
PROJ_COLS = OFF_GV + GQA_KV_WIDTH

NEG = -1e30
LOG2E = math.log2(math.e)
HALO = 8
VMEM_LIMIT = 56 * 2**20


def _cp(*sem):
    return pltpu.CompilerParams(dimension_semantics=sem, vmem_limit_bytes=VMEM_LIMIT)


def _sigmoid(x):
    return 1.0 / (1.0 + jnp.exp(-x))


def _silu(x):
    return x * _sigmoid(x)


NORM_ROWS = 128


def _rmsnorm_rows(x_ref, nw_ref, h_ref, n_rows):
    def body(r, c):
        rows = pl.ds(pl.multiple_of(r * NORM_ROWS, NORM_ROWS), NORM_ROWS)
        x = x_ref[rows, :]
        ms = jnp.mean(x * x, axis=-1, keepdims=True)
        h_ref[rows, :] = (x * lax.rsqrt(ms + EPS) * nw_ref[...]).astype(BF16)
        return c

    lax.fori_loop(0, n_rows // NORM_ROWS, body, 0)


def _rmsnorm_kernel(x_ref, nw_ref, h_ref, *, tm):
    _rmsnorm_rows(x_ref, nw_ref, h_ref, tm)


def _rmsnorm(x, nw, *, tm):
    m, d = x.shape
    return pl.pallas_call(
        functools.partial(_rmsnorm_kernel, tm=tm),
        grid=(m // tm,),
        in_specs=[pl.BlockSpec((tm, d), lambda i: (i, 0)), pl.BlockSpec((1, d), lambda i: (0, 0))],
        out_specs=pl.BlockSpec((tm, d), lambda i: (i, 0)),
        out_shape=jax.ShapeDtypeStruct((m, d), BF16),
        compiler_params=_cp("parallel"),
        name="rmsnorm",
    )(x, nw)


def _mm_kernel(*refs, with_dt):
    if with_dt:
        h_ref, w_ref, wdt_ref, o_ref, odt_ref = refs

        @pl.when(pl.program_id(1) == 0)
        def _():
            odt_ref[...] = jnp.dot(h_ref[...], wdt_ref[...], preferred_element_type=F32)
    else:
        h_ref, w_ref, o_ref = refs
    o_ref[...] = jnp.dot(h_ref[...], w_ref[...], preferred_element_type=F32).astype(o_ref.dtype)


def _matmul(h, w, wdt=None, *, tm, tn):
    m, d = h.shape
    n = w.shape[1]
    with_dt = wdt is not None
    in_specs = [
        pl.BlockSpec((tm, d), lambda i, j: (i, 0)),
        pl.BlockSpec((d, tn), lambda i, j: (0, j)),
    ]
    out_shape = [jax.ShapeDtypeStruct((m, n), BF16)]
    out_specs = [pl.BlockSpec((tm, tn), lambda i, j: (i, j))]
    args = [h, w]
    if with_dt:
        in_specs.append(pl.BlockSpec((d, DT_PAD), lambda i, j: (0, 0)))
        out_shape.append(jax.ShapeDtypeStruct((m, DT_PAD), F32))
        out_specs.append(pl.BlockSpec((tm, DT_PAD), lambda i, j: (i, 0)))
        args.append(wdt)
    res = pl.pallas_call(
        functools.partial(_mm_kernel, with_dt=with_dt),
        grid=(m // tm, n // tn),
        in_specs=in_specs,
        out_specs=out_specs,
        out_shape=out_shape,
        compiler_params=_cp("parallel", "arbitrary"),
        name="matmul_dt" if with_dt else "matmul",
    )(*args)
    return res if with_dt else res[0]


NA_QROWS = 8
NA_KROWS = 16
NA_TQ = NA_QROWS * GRID_W
NA_TK = NA_KROWS * GRID_W
NA_KPIECE = 256
NA_NPIECE = NA_TK // NA_KPIECE


def _na_bias_tables(rpb):
    rows = 4 * NA_QROWS
    nb = rows // NA_QROWS
    depth = rpb.shape[0]
    c = np.arange(GRID_W)
    col_start = np.clip(c - NA_WIN_W // 2, 0, GRID_W - NA_WIN_W)
    in_win = (c[None, :] >= col_start[:, None]) & (c[None, :] < col_start[:, None] + NA_WIN_W)
    dc = np.clip(c[None, :] - c[:, None], -(NA_WIN_W - 1), NA_WIN_W - 1) + NA_WIN_W - 1
    onehot_c = (dc[:, :, None] == np.arange(2 * NA_WIN_W - 1)).astype(np.float32)
    by_col = jnp.einsum("lhab,qcb->lhaqc", rpb.astype(F32), jnp.asarray(onehot_c), precision=lax.Precision.HIGHEST)
    by_col = by_col * LOG2E
    tables = []
    for b in (0, 1, nb - 1):
        r = NA_QROWS * b + np.arange(NA_QROWS)
        ks = int(np.clip(NA_QROWS * b - NA_WIN_H // 2, 0, rows - NA_KROWS))
        krow = ks + np.arange(NA_KROWS)
        rs = np.clip(r - NA_WIN_H // 2, 0, rows - NA_WIN_H)
        valid = (krow[None, :] >= rs[:, None]) & (krow[None, :] < rs[:, None] + NA_WIN_H)
        dr = np.clip(krow[None, :] - r[:, None], -(NA_WIN_H - 1), NA_WIN_H - 1) + NA_WIN_H - 1
        bias = jnp.take(by_col, jnp.asarray(dr.reshape(-1)), axis=2)
        bias = bias.reshape(depth, NA_HEADS, NA_QROWS, NA_KROWS, GRID_W, GRID_W).transpose(0, 1, 2, 4, 3, 5)
        mask = valid[:, None, :, None] & in_win[None, :, None, :]
        bias = jnp.where(jnp.asarray(mask)[None, None], bias, NEG)
        tables.append(bias.reshape(depth, NA_HEADS, NA_TQ, NA_TK))
    return jnp.stack(tables, axis=1)


def _head_rmsnorm(x, w):
    ms = jnp.mean(x * x, axis=-1, keepdims=True)
    return x * lax.rsqrt(ms + EPS) * w


NA_SUB = 128
NA_HPS = 2


def _na_kernel(q_ref, k0, k1, k2, k3, v0, v1, v2, v3, bias_ref, qw_ref, kw_ref, o_ref, kn_scr, va_scr, s_scr, *, nb):
    i = pl.program_id(2)
    kind = jnp.where(i == 0, 0, jnp.where(i == nb - 1, 2, 1))
    scale = HEAD_DIM ** -0.5 * LOG2E
    ones = jnp.ones((NA_KPIECE, HEAD_DIM), BF16)
    for hh in range(NA_HPS):
        hc = pl.ds(hh * HEAD_DIM, HEAD_DIM)
        for t, (k_ref, v_ref) in enumerate(zip((k0, k1, k2, k3), (v0, v1, v2, v3))):
            piece = pl.ds(t * NA_KPIECE, NA_KPIECE)
            kn_scr[hh, piece, :] = _head_rmsnorm(k_ref[:, hc].astype(F32), kw_ref[...]).astype(BF16)
            va_scr[hh, piece, :HEAD_DIM] = v_ref[:, hc]
            va_scr[hh, piece, HEAD_DIM:] = ones
    nlane = NA_TK // HEAD_DIM
    nsub = NA_TQ // NA_SUB

    def scores(u):
        hh, qs = divmod(u, nsub)
        rows = pl.ds(qs * NA_SUB, NA_SUB)
        qn = (_head_rmsnorm(q_ref[rows, pl.ds(hh * HEAD_DIM, HEAD_DIM)].astype(F32), qw_ref[...]) * scale).astype(BF16)
        s = lax.dot_general(qn, kn_scr[hh], (((1,), (1,)), ((), ())), preferred_element_type=F32)
        return s + bias_ref[kind, hh, rows, :]

    def finish(u, slot):
        hh, qs = divmod(u, nsub)
        blocks = [s_scr[slot, :, j * HEAD_DIM:(j + 1) * HEAD_DIM] for j in range(nlane)]
        part = blocks[0]
        for blk in blocks[1:]:
            part = jnp.maximum(part, blk)
        m = jnp.broadcast_to(jnp.max(part, axis=-1, keepdims=True), part.shape)
        p = jnp.concatenate([jnp.exp2(blk - m).astype(BF16) for blk in blocks], axis=1)
        pv = jnp.dot(p, va_scr[hh], preferred_element_type=F32)
        o_ref[pl.ds(qs * NA_SUB, NA_SUB), pl.ds(hh * HEAD_DIM, HEAD_DIM)] = (
            pv[:, :HEAD_DIM] / pv[:, HEAD_DIM:]).astype(o_ref.dtype)

    units = NA_HPS * nsub
    s_scr[0] = scores(0)
    for u in range(units):
        if u + 1 < units:
            s_scr[(u + 1) % 2] = scores(u + 1)
        finish(u, u % 2)


def _na_attention(proj, bias, qw, kw, *, bsz, seq):
    m = proj.shape[0]
    rows = seq // GRID_W
    nb = rows // NA_QROWS
    npieces = seq // NA_KPIECE
    width = NA_HPS * HEAD_DIM
    q_col = OFF_NA // width
    k_col = (OFF_NA + NA_WIDTH) // width
    v_col = (OFF_NA + 2 * NA_WIDTH) // width

    def kbase(i):
        return jnp.clip(2 * i - 1, 0, npieces - NA_NPIECE)

    def kv_spec(col, t):
        return pl.BlockSpec((NA_KPIECE, width), lambda h, b, i: (b * npieces + kbase(i) + t, col + h))

    in_specs = [pl.BlockSpec((NA_TQ, width), lambda h, b, i: (b * nb + i, q_col + h))]
    in_specs += [kv_spec(k_col, t) for t in range(NA_NPIECE)]
    in_specs += [kv_spec(v_col, t) for t in range(NA_NPIECE)]
    in_specs += [
        pl.BlockSpec((3, NA_HPS, NA_TQ, NA_TK), lambda h, b, i: (0, h, 0, 0), pipeline_mode=pl.Buffered(1)),
        pl.BlockSpec((1, HEAD_DIM), lambda h, b, i: (0, 0)),
        pl.BlockSpec((1, HEAD_DIM), lambda h, b, i: (0, 0)),
    ]
    return pl.pallas_call(
        functools.partial(_na_kernel, nb=nb),
        grid=(NA_HEADS // NA_HPS, bsz, nb),
        in_specs=in_specs,
        out_specs=pl.BlockSpec((NA_TQ, width), lambda h, b, i: (b * nb + i, h)),
        out_shape=jax.ShapeDtypeStruct((m, NA_WIDTH), BF16),
        scratch_shapes=[
            pltpu.VMEM((NA_HPS, NA_TK, HEAD_DIM), BF16),
            pltpu.VMEM((NA_HPS, NA_TK, 2 * HEAD_DIM), BF16),
            pltpu.VMEM((2, NA_SUB, NA_TK), F32),
        ],
        compiler_params=_cp("parallel", "parallel", "parallel"),
        name="na_attention",
    )(proj, *([proj] * (2 * NA_NPIECE)), bias, qw, kw)


def _rope_tables(seq):
    t = np.arange(seq)
    pos = np.stack([t // GRID_W, t % GRID_W], axis=-1).astype(np.float32)
    n_freq = HEAD_DIM // 4
    inv_freq = jnp.asarray(ROPE_THETA, F32) ** (-jnp.arange(n_freq, dtype=F32) / n_freq)
    ang = jnp.asarray(pos)[:, :, None] * inv_freq
    cos = jnp.cos(ang)
    sin = jnp.sin(ang)
    cos_t = jnp.concatenate([cos[:, 0], cos[:, 0], cos[:, 1], cos[:, 1]], axis=-1)
    sin_t = jnp.concatenate([-sin[:, 0], sin[:, 0], -sin[:, 1], sin[:, 1]], axis=-1)
    return cos_t, sin_t


def _rope(x, cos_t, sin_t, first_half):
    swapped = jnp.where(first_half, pltpu.roll(x, HEAD_DIM - 32, axis=1), pltpu.roll(x, 32, axis=1))
    return x * cos_t + swapped * sin_t


def _gqa_prep_kernel(q_ref, k_ref, v_ref, cos_ref, sin_ref, qw_ref, kw_ref, qo_ref, ko_ref, vo_ref):
    cos_t = cos_ref[...]
    sin_t = sin_ref[...]
    lane = lax.broadcasted_iota(jnp.int32, cos_t.shape, 1)
    first_half = (lane % 64) < 32
    scale = HEAD_DIM ** -0.5 * LOG2E
    for h in range(GQA_HEADS):
        cols = pl.ds(h * HEAD_DIM, HEAD_DIM)
        x = _head_rmsnorm(q_ref[:, cols].astype(F32), qw_ref[...])
        qo_ref[:, cols] = (_rope(x, cos_t, sin_t, first_half) * scale).astype(BF16)
    ones = jnp.ones(cos_t.shape, BF16)
    for h in range(GQA_KV_HEADS):
        cols = pl.ds(h * HEAD_DIM, HEAD_DIM)
        x = _head_rmsnorm(k_ref[:, cols].astype(F32), kw_ref[...])
        ko_ref[:, cols] = _rope(x, cos_t, sin_t, first_half).astype(BF16)
        vo_ref[:, pl.ds(2 * h * HEAD_DIM, HEAD_DIM)] = v_ref[:, cols]
        vo_ref[:, pl.ds((2 * h + 1) * HEAD_DIM, HEAD_DIM)] = ones


def _gqa_prep(proj, cos_t, sin_t, qw, kw, *, seq, tm):
    m = proj.shape[0]
    tps = seq // tm
    return pl.pallas_call(
        _gqa_prep_kernel,
        grid=(m // tm,),
        in_specs=[
            pl.BlockSpec((tm, GQA_WIDTH), lambda i: (i, OFF_GQ // GQA_WIDTH)),
            pl.BlockSpec((tm, GQA_KV_WIDTH), lambda i: (i, OFF_GK // GQA_KV_WIDTH)),
            pl.BlockSpec((tm, GQA_KV_WIDTH), lambda i: (i, OFF_GV // GQA_KV_WIDTH)),
            pl.BlockSpec((tm, HEAD_DIM), lambda i: (i % tps, 0)),
            pl.BlockSpec((tm, HEAD_DIM), lambda i: (i % tps, 0)),
            pl.BlockSpec((1, HEAD_DIM), lambda i: (0, 0)),
            pl.BlockSpec((1, HEAD_DIM), lambda i: (0, 0)),
        ],
        out_specs=[
            pl.BlockSpec((tm, GQA_WIDTH), lambda i: (i, 0)),
            pl.BlockSpec((tm, GQA_KV_WIDTH), lambda i: (i, 0)),
            pl.BlockSpec((tm, 2 * GQA_KV_WIDTH), lambda i: (i, 0)),
        ],
        out_shape=[
            jax.ShapeDtypeStruct((m, GQA_WIDTH), BF16),
            jax.ShapeDtypeStruct((m, GQA_KV_WIDTH), BF16),
            jax.ShapeDtypeStruct((m, 2 * GQA_KV_WIDTH), BF16),
        ],
        compiler_params=_cp("parallel"),
        name="gqa_prep",
    )(proj, proj, proj, cos_t, sin_t, qw, kw)


def _flash_kernel(q_ref, k_ref, v_ref, o_ref, s_scr, m_scr, acc_scr, *, tq, tk, seq):
    q2 = jnp.concatenate([q_ref[:, :HEAD_DIM], q_ref[:, HEAD_DIM:]], axis=0)
    m_scr[...] = jnp.full(m_scr.shape, NEG, F32)
    acc_scr[...] = jnp.zeros(acc_scr.shape, F32)
    nblk = tk // HEAD_DIM
    n = seq // tk

    def chunk(kc):
        return pl.ds(kc * tk, tk)

    def scores(kc):
        return lax.dot_general(q2, k_ref[chunk(kc), :], (((1,), (1,)), ((), ())), preferred_element_type=F32)

    def softmax_pv(kc, slot):
        blocks = [s_scr[slot, :, j * HEAD_DIM:(j + 1) * HEAD_DIM] for j in range(nblk)]
        part = blocks[0]
        for blk in blocks[1:]:
            part = jnp.maximum(part, blk)
        m_prev = m_scr[...]
        m_new = jnp.maximum(m_prev, jnp.max(part, axis=-1, keepdims=True))
        alpha = jnp.exp2(m_prev - m_new)
        p = jnp.concatenate([jnp.exp2(blk - m_new).astype(BF16) for blk in blocks], axis=1)
        pv = jnp.dot(p, v_ref[chunk(kc), :], preferred_element_type=F32)
        acc_scr[...] = jnp.concatenate([alpha, alpha], axis=1) * acc_scr[...] + pv
        m_scr[...] = m_new

    s_scr[0] = scores(0)

    for kc in range(n):
        if kc + 1 < n:
            s_scr[(kc + 1) % 2] = scores(kc + 1)
        softmax_pv(kc, kc % 2)
    o = acc_scr[:, :HEAD_DIM] / acc_scr[:, HEAD_DIM:]
    o_ref[:, :HEAD_DIM] = o[:tq].astype(o_ref.dtype)
    o_ref[:, HEAD_DIM:] = o[tq:].astype(o_ref.dtype)


def _gqa_attention(qn, kn, vaug, *, bsz, seq, tq, tk):
    m = qn.shape[0]
    nq = seq // tq
    rep = GQA_HEADS // GQA_KV_HEADS
    return pl.pallas_call(
        functools.partial(_flash_kernel, tq=tq, tk=tk, seq=seq),
        grid=(bsz, GQA_KV_HEADS, nq),
        in_specs=[
            pl.BlockSpec((tq, rep * HEAD_DIM), lambda b, g, i: (b * nq + i, g)),
            pl.BlockSpec((seq, HEAD_DIM), lambda b, g, i: (b, g)),
            pl.BlockSpec((seq, 2 * HEAD_DIM), lambda b, g, i: (b, g)),
        ],
        out_specs=pl.BlockSpec((tq, rep * HEAD_DIM), lambda b, g, i: (b * nq + i, g)),
        out_shape=jax.ShapeDtypeStruct((m, GQA_WIDTH), BF16),
        scratch_shapes=[
            pltpu.VMEM((2, rep * tq, tk), F32),
            pltpu.VMEM((rep * tq, HEAD_DIM), F32),
            pltpu.VMEM((rep * tq, 2 * HEAD_DIM), F32),
        ],
        compiler_params=_cp("parallel", "parallel", "parallel"),
        name="gqa_flash",
    )(qn, kn, vaug)


def _with_halo(x, prev, nxt):
    return jnp.concatenate([x, nxt, prev], axis=0)


def _shift_rows(xcat, s, n):
    return pltpu.roll(xcat, s % xcat.shape[0], axis=0)[:n]


def _dwconv(x, prev, nxt, w, b, width):
    n = x.shape[0]
    xcat = _with_halo(x, prev, nxt)
    pad = width // 2
    out = b + x * w[pad:pad + 1, :]
    for i in range(width):
        if i != pad:
            out = out + _shift_rows(xcat, pad - i, n) * w[i:i + 1, :]
    return out


def _halo_specs(tm, width, col_block, m):
    nblk = m // HALO
    per = tm // HALO
    prev = pl.BlockSpec((HALO, width), lambda i, *_: (jnp.maximum(i * per - 1, 0), col_block))
    nxt = pl.BlockSpec((HALO, width), lambda i, *_: (jnp.minimum((i + 1) * per, nblk - 1), col_block))
    return prev, nxt


def _ssd_prep_kernel(x_ref, prev_ref, next_ref, dtr_ref, cw_ref, cb_ref, dtb_ref, xs_ref, bc_ref, dt_ref, *, tps):
    i = pl.program_id(0)
    first = (i % tps) == 0
    last = (i % tps) == tps - 1
    x = x_ref[...].astype(F32)
    prev = jnp.where(first, 0.0, prev_ref[...].astype(F32))
    nxt = jnp.where(last, 0.0, next_ref[...].astype(F32))
    y = _silu(_dwconv(x, prev, nxt, cw_ref[...], cb_ref[...], SSD_CONV))
    xs_ref[...] = y[:, :SSD_WIDTH].astype(BF16)
    bc_ref[...] = y[:, SSD_WIDTH:].astype(BF16)
    v = dtr_ref[...] + dtb_ref[...]
    dt_ref[...] = jnp.maximum(v, 0.0) + jnp.log(1.0 + jnp.exp(-jnp.abs(v)))


def _ssd_prep(proj, dt_raw, conv_w, conv_b, dt_bias, *, seq, tm):
    m = proj.shape[0]
    cb = OFF_XBC // SSD_XBC
    prev, nxt = _halo_specs(tm, SSD_XBC, cb, m)
    return pl.pallas_call(
        functools.partial(_ssd_prep_kernel, tps=seq // tm),
        grid=(m // tm,),
        in_specs=[
            pl.BlockSpec((tm, SSD_XBC), lambda i: (i, cb)),
            prev,
            nxt,
            pl.BlockSpec((tm, DT_PAD), lambda i: (i, 0)),
            pl.BlockSpec((SSD_CONV, SSD_XBC), lambda i: (0, 0)),
            pl.BlockSpec((1, SSD_XBC), lambda i: (0, 0)),
            pl.BlockSpec((1, DT_PAD), lambda i: (0, 0)),
        ],
        out_specs=[
            pl.BlockSpec((tm, SSD_WIDTH), lambda i: (i, 0)),
            pl.BlockSpec((tm, SSD_BC), lambda i: (i, 0)),
            pl.BlockSpec((tm, DT_PAD), lambda i: (i, 0)),
        ],
        out_shape=[
            jax.ShapeDtypeStruct((m, SSD_WIDTH), BF16),
            jax.ShapeDtypeStruct((m, SSD_BC), BF16),
            jax.ShapeDtypeStruct((m, DT_PAD), F32),
        ],
        compiler_params=_cp("parallel"),
        name="ssd_prep",
    )(proj, proj, proj, dt_raw, conv_w, conv_b, dt_bias)


def _ssd_constants():
    q = SSD_CHUNK
    li = np.arange(q)
    tri = np.stack([li[:, None] >= li[None, :], li[:, None] <= li[None, :]]).astype(np.float32)
    hp = np.arange(SSD_WIDTH) // SSD_HEAD_DIM
    expand = np.zeros((2, 2 * DT_PAD, SSD_WIDTH), np.float32)
    for d in range(2):
        expand[d, d * SSD_HEADS + hp, np.arange(SSD_WIDTH)] = 1.0
        expand[d, DT_PAD + d * SSD_HEADS + hp, np.arange(SSD_WIDTH)] = 1.0
    return jnp.asarray(tri, F32), jnp.asarray(expand, BF16)


def _expand_heads(v, e):
    hi = v.astype(BF16)
    lo = (v - hi.astype(F32)).astype(BF16)
    return jnp.dot(jnp.concatenate([hi, lo], axis=1), e, preferred_element_type=F32)


def _ssd_scan_kernel(xs_f, bc_f, dt_f, xs_b, bc_b, dt_b, alog_ref, tri_ref, exp_ref, yf_ref, yb_ref, h_scr):
    @pl.when(pl.program_id(1) == 0)
    def _():
        h_scr[...] = jnp.zeros(h_scr.shape, F32)

    q = SSD_CHUNK
    gw = SSD_WIDTH // SSD_GROUPS
    a_row = -jnp.exp(alog_ref[...])
    li = lax.broadcasted_iota(jnp.int32, (q, q), 0)
    si = lax.broadcasted_iota(jnp.int32, (q, q), 1)
    lane = lax.broadcasted_iota(jnp.int32, (q, HEAD_DIM), 1)
    left = lane < SSD_HEAD_DIM
    dirs = ((xs_f, bc_f, dt_f, yf_ref, q - 1), (xs_b, bc_b, dt_b, yb_ref, 0))
    for cc, (d, (xs_ref, bc_ref, dt_ref, y_ref, last)) in [(cc, dd) for cc in range(SSD_CPS) for dd in enumerate(dirs)]:
        rows = pl.ds((cc if d == 0 else SSD_CPS - 1 - cc) * q, q)
        keep = (li >= si) if d == 0 else (li <= si)
        e = exp_ref[d]
        dt = dt_ref[rows, :]
        da = dt * a_row
        cum = jnp.dot(tri_ref[d], da, precision=lax.Precision.HIGHEST, preferred_element_type=F32)
        cum_t = cum.T
        tot = cum[last:last + 1, :]
        stack_e = _expand_heads(jnp.concatenate([dt, jnp.exp(cum), jnp.exp(tot - cum)], axis=0), e)
        dt_e, ec_e, w_e = stack_e[:q], stack_e[q:2 * q], stack_e[2 * q:]
        dec_e = ec_e[last:last + 1, :]
        xd = xs_ref[rows, :].astype(F32) * dt_e
        xdb = xd.astype(BF16)
        xw = (xd * w_e).astype(BF16)
        for g in range(SSD_GROUPS):
            gl = slice(g * gw, (g + 1) * gw)
            bg = bc_ref[rows, g * SSD_STATE:(g + 1) * SSD_STATE]
            cg = bc_ref[rows, (SSD_GROUPS + g) * SSD_STATE:(SSD_GROUPS + g + 1) * SSD_STATE]
            cb = lax.dot_general(cg, bg, (((1,), (1,)), ((), ())), preferred_element_type=F32)
            bg_t = bg.astype(F32).T.astype(BF16)
            h_prev = h_scr[d, g]
            y_off = jnp.dot(cg, h_prev.astype(BF16), preferred_element_type=F32) * ec_e[:, gl]
            states_t = jnp.dot(bg_t, xw[:, gl], preferred_element_type=F32)
            h_scr[d, g] = h_prev * dec_e[:, gl] + states_t
            for j in range(gw // HEAD_DIM):
                mats = []
                for hh in range(2):
                    col = d * SSD_HEADS + g * (SSD_HEADS // SSD_GROUPS) + 2 * j + hh
                    seg = cum[:, col:col + 1] - cum_t[col:col + 1, :]
                    lm = jnp.exp(jnp.where(keep, seg, NEG))
                    mats.append((cb * lm).astype(BF16))
                lo = g * gw + j * HEAD_DIM
                slab = xdb[:, lo:lo + HEAD_DIM]
                zero = jnp.zeros_like(slab)
                wbd = jnp.concatenate([jnp.where(left, slab, zero), jnp.where(left, zero, slab)], axis=0)
                y_diag = jnp.dot(jnp.concatenate(mats, axis=1), wbd, preferred_element_type=F32)
                y_ref[rows, lo:lo + HEAD_DIM] = (y_diag + y_off[:, j * HEAD_DIM:(j + 1) * HEAD_DIM]).astype(y_ref.dtype)


def _ssd_scan(xs, bc, dt, a_log, tri, expand, *, bsz, seq):
    m = xs.shape[0]
    nc = seq // (SSD_CPS * SSD_CHUNK)
    blk = SSD_CPS * SSD_CHUNK
    fwd = lambda b, c: (b * nc + c, 0)
    bwd = lambda b, c: (b * nc + nc - 1 - c, 0)
    return pl.pallas_call(
        _ssd_scan_kernel,
        grid=(bsz, nc),
        in_specs=[
            pl.BlockSpec((blk,SSD_WIDTH), fwd),
            pl.BlockSpec((blk,SSD_BC), fwd),
            pl.BlockSpec((blk,DT_PAD), fwd),
            pl.BlockSpec((blk,SSD_WIDTH), bwd),
            pl.BlockSpec((blk,SSD_BC), bwd),
            pl.BlockSpec((blk,DT_PAD), bwd),
            pl.BlockSpec((1, DT_PAD), lambda b, c: (0, 0)),
            pl.BlockSpec((2, SSD_CHUNK, SSD_CHUNK), lambda b, c: (0, 0, 0)),
            pl.BlockSpec((2, 2 * DT_PAD, SSD_WIDTH), lambda b, c: (0, 0, 0)),
        ],
        out_specs=[
            pl.BlockSpec((blk,SSD_WIDTH), fwd),
            pl.BlockSpec((blk,SSD_WIDTH), bwd),
        ],
        out_shape=[
            jax.ShapeDtypeStruct((m, SSD_WIDTH), BF16),
            jax.ShapeDtypeStruct((m, SSD_WIDTH), BF16),
        ],
        scratch_shapes=[pltpu.VMEM((2, SSD_GROUPS, SSD_STATE, SSD_WIDTH // SSD_GROUPS), F32)],
        compiler_params=_cp("parallel", "arbitrary"),
        name="ssd_scan",
    )(xs, bc, dt, xs, bc, dt, a_log, tri, expand)


def _ssd_finish_kernel(yf_ref, yb_ref, xs_ref, z_ref, d_ref, nw_ref, o_ref):
    y = yf_ref[...].astype(F32) + yb_ref[...].astype(F32) + xs_ref[...].astype(F32) * d_ref[...]
    y = y * _silu(z_ref[...].astype(F32))
    gw = SSD_WIDTH // SSD_GROUPS
    for g in range(SSD_GROUPS):
        yg = y[:, g * gw:(g + 1) * gw]
        ms = jnp.mean(yg * yg, axis=-1, keepdims=True)
        o_ref[:, g * gw:(g + 1) * gw] = (yg * lax.rsqrt(ms + EPS) * nw_ref[:, g * gw:(g + 1) * gw]).astype(o_ref.dtype)


def _ssd_finish(yf, yb, xs, proj, d_exp, norm_w, *, tm):
    m = yf.shape[0]
    row = lambda i: (i, 0)
    const = lambda i: (0, 0)
    return pl.pallas_call(
        _ssd_finish_kernel,
        grid=(m // tm,),
        in_specs=[
            pl.BlockSpec((tm, SSD_WIDTH), row),
            pl.BlockSpec((tm, SSD_WIDTH), row),
            pl.BlockSpec((tm, SSD_WIDTH), row),
            pl.BlockSpec((tm, SSD_WIDTH), lambda i: (i, OFF_Z // SSD_WIDTH)),
            pl.BlockSpec((1, SSD_WIDTH), const),
            pl.BlockSpec((1, SSD_WIDTH), const),
        ],
        out_specs=pl.BlockSpec((tm, SSD_WIDTH), row),
        out_shape=jax.ShapeDtypeStruct((m, SSD_WIDTH), BF16),
        compiler_params=_cp("parallel"),
        name="ssd_finish",
    )(yf, yb, xs, proj, d_exp, norm_w)


def _merge_kernel(ona_ref, ossd_ref, ogqa_ref, g_ref, x_ref, wna_ref, wssd_ref, wgqa_ref, wout_ref, nw_ref,
                  o_ref, h_ref, mg_scr, *, tm):
    nc = 512
    branches = ((ona_ref, wna_ref), (ossd_ref, wssd_ref), (ogqa_ref, wgqa_ref))
    for c in range(D_MODEL // nc):
        cols = pl.ds(c * nc, nc)
        acc = None
        for b, (o_b, w_b) in enumerate(branches):
            p = jnp.dot(o_b[...], w_b[:, cols], preferred_element_type=F32)
            gate = _sigmoid(g_ref[:, pl.ds(b * D_MODEL + c * nc, nc)].astype(F32))
            acc = gate * p if acc is None else acc + gate * p
        mg_scr[:, cols] = acc.astype(BF16)
    for c in range(D_MODEL // nc):
        cols = pl.ds(c * nc, nc)
        o_ref[:, cols] = x_ref[:, cols] + jnp.dot(mg_scr[...], wout_ref[:, cols], preferred_element_type=F32)
    _rmsnorm_rows(o_ref, nw_ref, h_ref, tm)


def _merge(o_na, o_ssd, o_gqa, proj, x, w_na, w_ssd, w_gqa, w_out, norm_w, *, tm):
    m = x.shape[0]
    row = lambda i: (i, 0)
    const = lambda i: (0, 0)
    one = pl.Buffered(1)
    return pl.pallas_call(
        functools.partial(_merge_kernel, tm=tm),
        grid=(m // tm,),
        in_specs=[
            pl.BlockSpec((tm, NA_WIDTH), row),
            pl.BlockSpec((tm, SSD_WIDTH), row),
            pl.BlockSpec((tm, GQA_WIDTH), row),
            pl.BlockSpec((tm, GATES), row),
            pl.BlockSpec((tm, D_MODEL), row),
            pl.BlockSpec((NA_WIDTH, D_MODEL), const, pipeline_mode=one),
            pl.BlockSpec((SSD_WIDTH, D_MODEL), const, pipeline_mode=one),
            pl.BlockSpec((GQA_WIDTH, D_MODEL), const, pipeline_mode=one),
            pl.BlockSpec((D_MODEL, D_MODEL), const, pipeline_mode=one),
            pl.BlockSpec((1, D_MODEL), const),
        ],
        out_specs=[pl.BlockSpec((tm, D_MODEL), row), pl.BlockSpec((tm, D_MODEL), row)],
        out_shape=[jax.ShapeDtypeStruct((m, D_MODEL), F32), jax.ShapeDtypeStruct((m, D_MODEL), BF16)],
        scratch_shapes=[pltpu.VMEM((tm, D_MODEL), BF16)],
        compiler_params=_cp("parallel"),
        name="merge",
    )(o_na, o_ssd, o_gqa, proj, x, w_na, w_ssd, w_gqa, w_out, norm_w)


FFN_CHUNK = 256


def _ffn_down_kernel(*refs, tps, tm, with_norm):
    if with_norm:
        ug_ref, prev_ref, next_ref, uv_ref, cw_ref, cb_ref, wd_ref, x_ref, nw_ref, o_ref, h_ref, hid_scr = refs
    else:
        ug_ref, prev_ref, next_ref, uv_ref, cw_ref, cb_ref, wd_ref, x_ref, o_ref, hid_scr = refs
    i = pl.program_id(0)
    first = (i % tps) == 0
    last = (i % tps) == tps - 1

    def gated(c):
        cols = pl.ds(c * FFN_CHUNK, FFN_CHUNK)
        g = ug_ref[:, cols].astype(F32)
        prev = jnp.where(first, 0.0, prev_ref[:, cols].astype(F32))
        nxt = jnp.where(last, 0.0, next_ref[:, cols].astype(F32))
        a = _dwconv(g, prev, nxt, cw_ref[:, cols], cb_ref[:, cols], 3)
        hid_scr[:, cols] = (_silu(a) * uv_ref[:, cols].astype(F32)).astype(BF16)

    nchunk = D_FF // FFN_CHUNK
    gated(0)
    for c in range(nchunk):
        rows = pl.ds(c * FFN_CHUNK, FFN_CHUNK)
        part = jnp.dot(hid_scr[:, rows], wd_ref[rows, :], preferred_element_type=F32)
        if c + 1 < nchunk:
            gated(c + 1)
        if c == 0:
            o_ref[...] = x_ref[...] + part
        else:
            o_ref[...] += part
    if with_norm:
        _rmsnorm_rows(o_ref, nw_ref, h_ref, tm)


def _ffn_down(u, conv_w, conv_b, w_down, x, next_norm_w=None, *, seq, tm):
    m = x.shape[0]
    nblk = m // HALO
    per = tm // HALO
    const = lambda i: (0, 0)
    row = lambda i: (i, 0)
    with_norm = next_norm_w is not None
    in_specs = [
        pl.BlockSpec((tm, D_FF), lambda i: (i, 0)),
        pl.BlockSpec((HALO, D_FF), lambda i: (jnp.maximum(i * per - 1, 0), 0)),
        pl.BlockSpec((HALO, D_FF), lambda i: (jnp.minimum((i + 1) * per, nblk - 1), 0)),
        pl.BlockSpec((tm, D_FF), lambda i: (i, 1)),
        pl.BlockSpec((3, D_FF), const),
        pl.BlockSpec((1, D_FF), const),
        pl.BlockSpec((D_FF, D_MODEL), const, pipeline_mode=pl.Buffered(1)),
        pl.BlockSpec((tm, D_MODEL), row),
    ]
    out_specs = [pl.BlockSpec((tm, D_MODEL), row)]
    out_shape = [jax.ShapeDtypeStruct((m, D_MODEL), F32)]
    args = [u, u, u, u, conv_w, conv_b, w_down, x]
    if with_norm:
        in_specs.append(pl.BlockSpec((1, D_MODEL), const))
        out_specs.append(pl.BlockSpec((tm, D_MODEL), row))
        out_shape.append(jax.ShapeDtypeStruct((m, D_MODEL), BF16))
        args.append(next_norm_w)
    res = pl.pallas_call(
        functools.partial(_ffn_down_kernel, tps=seq // tm, tm=tm, with_norm=with_norm),
        grid=(m // tm,),
        in_specs=in_specs,
        out_specs=out_specs,
        out_shape=out_shape,
        scratch_shapes=[pltpu.VMEM((tm, D_FF), BF16)],
        compiler_params=_cp("parallel"),
        name="ffn_down",
    )(*args)
    return res if with_norm else (res[0], None)


MM_TN = 1024


def _tiles(bsz, seq):
    m = bsz * seq
    return dict(tm_mm=2048 if m % 2048 == 0 else 1024, tm_ew=min(seq, 1024), tm_merge=min(seq, 256),
                tm_down=min(seq, 256), tq=min(seq, 512), tk=min(seq, 1024))


def _prep_layer_params(p, na_bias):
    (norm1_w, w_in, na_q_norm, na_k_norm, _, ssd_conv_w, ssd_conv_b, ssd_dt_bias, ssd_a_log, ssd_d,
     ssd_norm_w, gqa_q_norm, gqa_k_norm, w_branch, w_out, norm2_w, ffn_w_up, ffn_conv_w, ffn_conv_b, ffn_w_down) = p
    sizes = (3 * NA_WIDTH, GQA_WIDTH, GQA_KV_WIDTH, GQA_KV_WIDTH, SSD_WIDTH, SSD_XBC, 2 * SSD_HEADS, GATES)
    offs = np.concatenate([[0], np.cumsum(sizes)])
    piece = lambda k: w_in[:, offs[k]:offs[k + 1]].astype(BF16)
    na, gq, gk, gv, z, xbc, dtw, gates = (piece(k) for k in range(8))
    w_main = jnp.concatenate([gates, xbc, na, z, gq, gk, gv], axis=1)
    w_dt = jnp.pad(dtw, ((0, 0), (0, DT_PAD - 2 * SSD_HEADS)))
    pad_dt = lambda v: jnp.pad(v.reshape(1, 2 * SSD_HEADS), ((0, 0), (0, DT_PAD - 2 * SSD_HEADS)))
    row = lambda v: v.reshape(1, -1)
    wb = w_branch.astype(BF16)
    return dict(
        norm1_w=row(norm1_w), w_main=w_main, w_dt=w_dt,
        na_q_norm=row(na_q_norm), na_k_norm=row(na_k_norm), na_bias=na_bias,
        conv_w=ssd_conv_w, conv_b=row(ssd_conv_b), dt_bias=pad_dt(ssd_dt_bias), a_log=pad_dt(ssd_a_log),
        d_exp=row(jnp.repeat(ssd_d, SSD_HEAD_DIM)), ssd_norm_w=row(ssd_norm_w),
        gqa_q_norm=row(gqa_q_norm), gqa_k_norm=row(gqa_k_norm),
        w_na=wb[:NA_WIDTH], w_ssd=wb[NA_WIDTH:NA_WIDTH + SSD_WIDTH], w_gqa=wb[NA_WIDTH + SSD_WIDTH:],
        w_out=w_out.astype(BF16), norm2_w=row(norm2_w), w_up=ffn_w_up.astype(BF16),
        ffn_conv_w=ffn_conv_w, ffn_conv_b=row(ffn_conv_b), w_down=ffn_w_down.astype(BF16),
    )


def _prep_params(params):
    depth = params[0].shape[0]
    na_bias = _na_bias_tables(params[4])
    return [_prep_layer_params([p[i] for p in params], na_bias[i]) for i in range(depth)]


def _layer(x, h, lp, next_norm_w, consts, *, bsz, seq):
    t = _tiles(bsz, seq)
    cos_t, sin_t, tri, expand = consts
    proj, dt_raw = _matmul(h, lp["w_main"], lp["w_dt"], tm=t["tm_mm"], tn=MM_TN)
    o_na = _na_attention(proj, lp["na_bias"], lp["na_q_norm"], lp["na_k_norm"], bsz=bsz, seq=seq)
    qn, kn, vaug = _gqa_prep(proj, cos_t, sin_t, lp["gqa_q_norm"], lp["gqa_k_norm"], seq=seq, tm=t["tm_ew"])
    o_gqa = _gqa_attention(qn, kn, vaug, bsz=bsz, seq=seq, tq=t["tq"], tk=t["tk"])
    xs, bc, dt = _ssd_prep(proj, dt_raw, lp["conv_w"], lp["conv_b"], lp["dt_bias"], seq=seq, tm=t["tm_ew"])
    yf, yb = _ssd_scan(xs, bc, dt, lp["a_log"], tri, expand, bsz=bsz, seq=seq)
    o_ssd = _ssd_finish(yf, yb, xs, proj, lp["d_exp"], lp["ssd_norm_w"], tm=t["tm_ew"])
    x, h2 = _merge(o_na, o_ssd, o_gqa, proj, x, lp["w_na"], lp["w_ssd"], lp["w_gqa"], lp["w_out"], lp["norm2_w"],
                   tm=t["tm_merge"])
    u = _matmul(h2, lp["w_up"], tm=t["tm_mm"], tn=MM_TN)
    return _ffn_down(u, lp["ffn_conv_w"], lp["ffn_conv_b"], lp["w_down"], x, next_norm_w, seq=seq, tm=t["tm_down"])


def _trunk(x, layer_params):
    bsz, seq, d = x.shape
    assert d == D_MODEL and seq % 1024 == 0 and seq // GRID_W >= NA_KROWS
    consts = _rope_tables(seq) + _ssd_constants()
    x = x.reshape(bsz * seq, d)
    h = _rmsnorm(x, layer_params[0]["norm1_w"], tm=_tiles(bsz, seq)["tm_ew"])
    for li, lp in enumerate(layer_params):
        nxt = layer_params[li + 1]["norm1_w"] if li + 1 < len(layer_params) else None
        x, h = _layer(x, h, lp, nxt, consts, bsz=bsz, seq=seq)
    return x.reshape(bsz, seq, d)


def kernel(x_prompt, x_sample, norm1_w, w_in, na_q_norm, na_k_norm, na_rpb, ssd_conv_w, ssd_conv_b, ssd_dt_bias, ssd_a_log, ssd_d, ssd_norm_w, gqa_q_norm, gqa_k_norm, w_branch, w_out, norm2_w, ffn_w_up, ffn_conv_w, ffn_conv_b, ffn_w_down):
    params = (norm1_w, w_in, na_q_norm, na_k_norm, na_rpb, ssd_conv_w, ssd_conv_b, ssd_dt_bias, ssd_a_log,
              ssd_d, ssd_norm_w, gqa_q_norm, gqa_k_norm, w_branch, w_out, norm2_w, ffn_w_up, ffn_conv_w,
              ffn_conv_b, ffn_w_down)
    layer_params = _prep_params(params)
    return (_trunk(x_prompt, layer_params), _trunk(x_sample, layer_params))
```

```python
import functools
import math

import numpy as np
import jax
import jax.numpy as jnp
from jax import lax
from jax.experimental import pallas as pl
from jax.experimental.pallas import tpu as pltpu

F32 = jnp.float32
BF16 = jnp.bfloat16

D_MODEL = 2048
GRID_W = 64
EPS = 1e-6
NA_HEADS = 4
HEAD_DIM = 128
NA_WIDTH = NA_HEADS * HEAD_DIM
NA_WIN_H = 8
NA_WIN_W = 16
SSD_HEADS = 16
SSD_HEAD_DIM = 64
SSD_WIDTH = SSD_HEADS * SSD_HEAD_DIM
SSD_STATE = 128
SSD_GROUPS = 2
SSD_CONV = 5
SSD_CHUNK = 128
SSD_CPS = 4
SSD_BC = 2 * SSD_GROUPS * SSD_STATE
SSD_XBC = SSD_WIDTH + SSD_BC
GQA_HEADS = 4
GQA_KV_HEADS = 2
GQA_WIDTH = GQA_HEADS * HEAD_DIM
GQA_KV_WIDTH = GQA_KV_HEADS * HEAD_DIM
ROPE_THETA = 10000.0
D_FF = 5632
N_BRANCH = 3
GATES = N_BRANCH * D_MODEL
DT_PAD = 128

OFF_GATES = 0
OFF_XBC = OFF_GATES + GATES
OFF_NA = OFF_XBC + SSD_XBC
OFF_Z = OFF_NA + 3 * NA_WIDTH
OFF_GQ = OFF_Z + SSD_WIDTH
OFF_GK = OFF_GQ + GQA_WIDTH
OFF_GV = OFF_GK + GQA_KV_WIDTH
PROJ_COLS = OFF_GV + GQA_KV_WIDTH

NEG = -1e30
LOG2E = math.log2(math.e)
HALO = 8
VMEM_LIMIT = 56 * 2**20


def _cp(*sem):
    return pltpu.CompilerParams(dimension_semantics=sem, vmem_limit_bytes=VMEM_LIMIT)


def _sigmoid(x):
    return 0.5 + 0.5 * jnp.tanh(0.5 * x)


def _silu(x):
    half = 0.5 * x
    return half + half * jnp.tanh(half)


NORM_ROWS = 128


def _rmsnorm_rows(x_ref, nw_ref, h_ref, n_rows):
    def body(r, c):
        rows = pl.ds(pl.multiple_of(r * NORM_ROWS, NORM_ROWS), NORM_ROWS)
        x = x_ref[rows, :]
        ms = jnp.mean(x * x, axis=-1, keepdims=True)
        h_ref[rows, :] = (x * lax.rsqrt(ms + EPS) * nw_ref[...]).astype(BF16)
        return c

    lax.fori_loop(0, n_rows // NORM_ROWS, body, 0)


def _rmsnorm_kernel(x_ref, nw_ref, h_ref, *, tm):
    _rmsnorm_rows(x_ref, nw_ref, h_ref, tm)


def _rmsnorm(x, nw, *, tm):
    m, d = x.shape
    return pl.pallas_call(
        functools.partial(_rmsnorm_kernel, tm=tm),
        grid=(m // tm,),
        in_specs=[pl.BlockSpec((tm, d), lambda i: (i, 0)), pl.BlockSpec((1, d), lambda i: (0, 0))],
        out_specs=pl.BlockSpec((tm, d), lambda i: (i, 0)),
        out_shape=jax.ShapeDtypeStruct((m, d), BF16),
        compiler_params=_cp("parallel"),
        name="rmsnorm",
    )(x, nw)


def _mm_kernel(*refs, with_dt):
    if with_dt:
        h_ref, w_ref, wdt_ref, o_ref, odt_ref = refs

        @pl.when(pl.program_id(1) == 0)
        def _():
            odt_ref[...] = jnp.dot(h_ref[...], wdt_ref[...], preferred_element_type=F32)
    else:
        h_ref, w_ref, o_ref = refs
    o_ref[...] = jnp.dot(h_ref[...], w_ref[...], preferred_element_type=F32).astype(o_ref.dtype)


def _matmul(h, w, wdt=None, *, tm, tn):
    m, d = h.shape
    n = w.shape[1]
    with_dt = wdt is not None
    in_specs = [
        pl.BlockSpec((tm, d), lambda i, j: (i, 0)),
        pl.BlockSpec((d, tn), lambda i, j: (0, j)),
    ]
    out_shape = [jax.ShapeDtypeStruct((m, n), BF16)]
    out_specs = [pl.BlockSpec((tm, tn), lambda i, j: (i, j))]
    args = [h, w]
    if with_dt:
        in_specs.append(pl.BlockSpec((d, DT_PAD), lambda i, j: (0, 0)))
        out_shape.append(jax.ShapeDtypeStruct((m, DT_PAD), F32))
        out_specs.append(pl.BlockSpec((tm, DT_PAD), lambda i, j: (i, 0)))
        args.append(wdt)
    res = pl.pallas_call(
        functools.partial(_mm_kernel, with_dt=with_dt),
        grid=(m // tm, n // tn),
        in_specs=in_specs,
        out_specs=out_specs,
        out_shape=out_shape,
        compiler_params=_cp("parallel", "arbitrary"),
        name="matmul_dt" if with_dt else "matmul",
    )(*args)
    return res if with_dt else res[0]


NA_QROWS = 8
NA_KROWS = 16
NA_TQ = NA_QROWS * GRID_W
NA_TK = NA_KROWS * GRID_W
NA_KPIECE = 256
NA_NPIECE = NA_TK // NA_KPIECE


def _na_bias_tables(rpb):
    rows = 4 * NA_QROWS
    nb = rows // NA_QROWS
    depth = rpb.shape[0]
    c = np.arange(GRID_W)
    col_start = np.clip(c - NA_WIN_W // 2, 0, GRID_W - NA_WIN_W)
    in_win = (c[None, :] >= col_start[:, None]) & (c[None, :] < col_start[:, None] + NA_WIN_W)
    dc = np.clip(c[None, :] - c[:, None], -(NA_WIN_W - 1), NA_WIN_W - 1) + NA_WIN_W - 1
    onehot_c = (dc[:, :, None] == np.arange(2 * NA_WIN_W - 1)).astype(np.float32)
    by_col = jnp.einsum("lhab,qcb->lhaqc", rpb.astype(F32), jnp.asarray(onehot_c), precision=lax.Precision.HIGHEST)
    by_col = by_col * LOG2E
    tables = []
    for b in (0, 1, nb - 1):
        r = NA_QROWS * b + np.arange(NA_QROWS)
        ks = int(np.clip(NA_QROWS * b - NA_WIN_H // 2, 0, rows - NA_KROWS))
        krow = ks + np.arange(NA_KROWS)
        rs = np.clip(r - NA_WIN_H // 2, 0, rows - NA_WIN_H)
        valid = (krow[None, :] >= rs[:, None]) & (krow[None, :] < rs[:, None] + NA_WIN_H)
        dr = np.clip(krow[None, :] - r[:, None], -(NA_WIN_H - 1), NA_WIN_H - 1) + NA_WIN_H - 1
        bias = jnp.take(by_col, jnp.asarray(dr.reshape(-1)), axis=2)
        bias = bias.reshape(depth, NA_HEADS, NA_QROWS, NA_KROWS, GRID_W, GRID_W).transpose(0, 1, 2, 4, 3, 5)
        mask = valid[:, None, :, None] & in_win[None, :, None, :]
        bias = jnp.where(jnp.asarray(mask)[None, None], bias, NEG)
        tables.append(bias.reshape(depth, NA_HEADS, NA_TQ, NA_TK))
    return jnp.stack(tables, axis=1)


def _head_rmsnorm(x, w):
    ms = jnp.mean(x * x, axis=-1, keepdims=True)
    return x * lax.rsqrt(ms + EPS) * w


NA_SUB = 128
NA_HPS = 4


def _na_kernel(q_ref, k0, k1, k2, k3, v0, v1, v2, v3, bias_ref, qw_ref, kw_ref, o_ref, kn_scr, va_scr, s_scr, *, nb):
    i = pl.program_id(2)
    kind = jnp.where(i == 0, 0, jnp.where(i == nb - 1, 2, 1))
    scale = HEAD_DIM ** -0.5 * LOG2E
    ones = jnp.ones((NA_KPIECE, HEAD_DIM), BF16)
    for hh in range(NA_HPS):
        hc = pl.ds(hh * HEAD_DIM, HEAD_DIM)
        for t, (k_ref, v_ref) in enumerate(zip((k0, k1, k2, k3), (v0, v1, v2, v3))):
            piece = pl.ds(t * NA_KPIECE, NA_KPIECE)
            kn_scr[hh, piece, :] = _head_rmsnorm(k_ref[:, hc].astype(F32), kw_ref[...]).astype(BF16)
            va_scr[hh, piece, :HEAD_DIM] = v_ref[:, hc]
            va_scr[hh, piece, HEAD_DIM:] = ones
    nlane = NA_TK // HEAD_DIM
    nsub = NA_TQ // NA_SUB

    def scores(u):
        hh, qs = divmod(u, nsub)
        rows = pl.ds(qs * NA_SUB, NA_SUB)
        qn = (_head_rmsnorm(q_ref[rows, pl.ds(hh * HEAD_DIM, HEAD_DIM)].astype(F32), qw_ref[...]) * scale).astype(BF16)
        s = lax.dot_general(qn, kn_scr[hh], (((1,), (1,)), ((), ())), preferred_element_type=F32)
        return s + bias_ref[kind, hh, rows, :]

    def finish(u, slot):
        hh, qs = divmod(u, nsub)
        blocks = [s_scr[slot, :, j * HEAD_DIM:(j + 1) * HEAD_DIM] for j in range(nlane)]
        part = blocks[0]
        for blk in blocks[1:]:
            part = jnp.maximum(part, blk)
        m = jnp.broadcast_to(jnp.max(part, axis=-1, keepdims=True), part.shape)
        p = jnp.concatenate([jnp.exp2(blk - m).astype(BF16) for blk in blocks], axis=1)
        pv = jnp.dot(p, va_scr[hh], preferred_element_type=F32)
        o_ref[pl.ds(qs * NA_SUB, NA_SUB), pl.ds(hh * HEAD_DIM, HEAD_DIM)] = (
            pv[:, :HEAD_DIM] / pv[:, HEAD_DIM:]).astype(o_ref.dtype)

    units = NA_HPS * nsub
    s_scr[0] = scores(0)
    for u in range(units):
        if u + 1 < units:
            s_scr[(u + 1) % 2] = scores(u + 1)
        finish(u, u % 2)


def _na_attention(proj, bias, qw, kw, *, bsz, seq):
    m = proj.shape[0]
    rows = seq // GRID_W
    nb = rows // NA_QROWS
    npieces = seq // NA_KPIECE
    width = NA_HPS * HEAD_DIM
    q_col = OFF_NA // width
    k_col = (OFF_NA + NA_WIDTH) // width
    v_col = (OFF_NA + 2 * NA_WIDTH) // width

    def kbase(i):
        return jnp.clip(2 * i - 1, 0, npieces - NA_NPIECE)

    def kv_spec(col, t):
        return pl.BlockSpec((NA_KPIECE, width), lambda h, b, i: (b * npieces + kbase(i) + t, col + h))

    in_specs = [pl.BlockSpec((NA_TQ, width), lambda h, b, i: (b * nb + i, q_col + h))]
    in_specs += [kv_spec(k_col, t) for t in range(NA_NPIECE)]
    in_specs += [kv_spec(v_col, t) for t in range(NA_NPIECE)]
    in_specs += [
        pl.BlockSpec((3, NA_HPS, NA_TQ, NA_TK), lambda h, b, i: (0, h, 0, 0), pipeline_mode=pl.Buffered(1)),
        pl.BlockSpec((1, HEAD_DIM), lambda h, b, i: (0, 0)),
        pl.BlockSpec((1, HEAD_DIM), lambda h, b, i: (0, 0)),
    ]
    return pl.pallas_call(
        functools.partial(_na_kernel, nb=nb),
        grid=(NA_HEADS // NA_HPS, bsz, nb),
        in_specs=in_specs,
        out_specs=pl.BlockSpec((NA_TQ, width), lambda h, b, i: (b * nb + i, h)),
        out_shape=jax.ShapeDtypeStruct((m, NA_WIDTH), BF16),
        scratch_shapes=[
            pltpu.VMEM((NA_HPS, NA_TK, HEAD_DIM), BF16),
            pltpu.VMEM((NA_HPS, NA_TK, 2 * HEAD_DIM), BF16),
            pltpu.VMEM((2, NA_SUB, NA_TK), F32),
        ],
        compiler_params=_cp("parallel", "parallel", "parallel"),
        name="na_attention",
    )(proj, *([proj] * (2 * NA_NPIECE)), bias, qw, kw)


def _rope_tables(seq):
    t = np.arange(seq)
    pos = np.stack([t // GRID_W, t % GRID_W], axis=-1).astype(np.float32)
    n_freq = HEAD_DIM // 4
    inv_freq = jnp.asarray(ROPE_THETA, F32) ** (-jnp.arange(n_freq, dtype=F32) / n_freq)
    ang = jnp.asarray(pos)[:, :, None] * inv_freq
    cos = jnp.cos(ang)
    sin = jnp.sin(ang)
    cos_t = jnp.concatenate([cos[:, 0], cos[:, 0], cos[:, 1], cos[:, 1]], axis=-1)
    sin_t = jnp.concatenate([-sin[:, 0], sin[:, 0], -sin[:, 1], sin[:, 1]], axis=-1)
    return cos_t, sin_t


def _rope(x, cos_t, sin_t, first_half):
    swapped = jnp.where(first_half, pltpu.roll(x, HEAD_DIM - 32, axis=1), pltpu.roll(x, 32, axis=1))
    return x * cos_t + swapped * sin_t


def _gqa_prep_kernel(q_ref, k_ref, v_ref, cos_ref, sin_ref, qw_ref, kw_ref, qo_ref, ko_ref, vo_ref):
    cos_t = cos_ref[...]
    sin_t = sin_ref[...]
    lane = lax.broadcasted_iota(jnp.int32, cos_t.shape, 1)
    first_half = (lane % 64) < 32
    scale = HEAD_DIM ** -0.5 * LOG2E
    for h in range(GQA_HEADS):
        cols = pl.ds(h * HEAD_DIM, HEAD_DIM)
        x = _head_rmsnorm(q_ref[:, cols].astype(F32), qw_ref[...])
        qo_ref[:, cols] = (_rope(x, cos_t, sin_t, first_half) * scale).astype(BF16)
    ones = jnp.ones(cos_t.shape, BF16)
    for h in range(GQA_KV_HEADS):
        cols = pl.ds(h * HEAD_DIM, HEAD_DIM)
        x = _head_rmsnorm(k_ref[:, cols].astype(F32), kw_ref[...])
        ko_ref[:, cols] = _rope(x, cos_t, sin_t, first_half).astype(BF16)
        vo_ref[:, pl.ds(2 * h * HEAD_DIM, HEAD_DIM)] = v_ref[:, cols]
        vo_ref[:, pl.ds((2 * h + 1) * HEAD_DIM, HEAD_DIM)] = ones


def _gqa_prep(proj, cos_t, sin_t, qw, kw, *, seq, tm):
    m = proj.shape[0]
    tps = seq // tm
    return pl.pallas_call(
        _gqa_prep_kernel,
        grid=(m // tm,),
        in_specs=[
            pl.BlockSpec((tm, GQA_WIDTH), lambda i: (i, OFF_GQ // GQA_WIDTH)),
            pl.BlockSpec((tm, GQA_KV_WIDTH), lambda i: (i, OFF_GK // GQA_KV_WIDTH)),
            pl.BlockSpec((tm, GQA_KV_WIDTH), lambda i: (i, OFF_GV // GQA_KV_WIDTH)),
            pl.BlockSpec((tm, HEAD_DIM), lambda i: (i % tps, 0)),
            pl.BlockSpec((tm, HEAD_DIM), lambda i: (i % tps, 0)),
            pl.BlockSpec((1, HEAD_DIM), lambda i: (0, 0)),
            pl.BlockSpec((1, HEAD_DIM), lambda i: (0, 0)),
        ],
        out_specs=[
            pl.BlockSpec((tm, GQA_WIDTH), lambda i: (i, 0)),
            pl.BlockSpec((tm, GQA_KV_WIDTH), lambda i: (i, 0)),
            pl.BlockSpec((tm, 2 * GQA_KV_WIDTH), lambda i: (i, 0)),
        ],
        out_shape=[
            jax.ShapeDtypeStruct((m, GQA_WIDTH), BF16),
            jax.ShapeDtypeStruct((m, GQA_KV_WIDTH), BF16),
            jax.ShapeDtypeStruct((m, 2 * GQA_KV_WIDTH), BF16),
        ],
        compiler_params=_cp("parallel"),
        name="gqa_prep",
    )(proj, proj, proj, cos_t, sin_t, qw, kw)


def _flash_kernel(q_ref, k_ref, v_ref, o_ref, s_scr, m_scr, acc_scr, *, tq, tk, seq):
    q2 = jnp.concatenate([q_ref[:, :HEAD_DIM], q_ref[:, HEAD_DIM:]], axis=0)
    m_scr[...] = jnp.full(m_scr.shape, NEG, F32)
    acc_scr[...] = jnp.zeros(acc_scr.shape, F32)
    nblk = tk // HEAD_DIM
    n = seq // tk

    def chunk(kc):
        return pl.ds(kc * tk, tk)

    def scores(kc):
        return lax.dot_general(q2, k_ref[chunk(kc), :], (((1,), (1,)), ((), ())), preferred_element_type=F32)

    def softmax_pv(kc, slot):
        blocks = [s_scr[slot, :, j * HEAD_DIM:(j + 1) * HEAD_DIM] for j in range(nblk)]
        part = blocks[0]
        for blk in blocks[1:]:
            part = jnp.maximum(part, blk)
        m_prev = m_scr[...]
        m_new = jnp.maximum(m_prev, jnp.max(part, axis=-1, keepdims=True))
        alpha = jnp.exp2(m_prev - m_new)
        p = jnp.concatenate([jnp.exp2(blk - m_new).astype(BF16) for blk in blocks], axis=1)
        pv = jnp.dot(p, v_ref[chunk(kc), :], preferred_element_type=F32)
        acc_scr[...] = jnp.concatenate([alpha, alpha], axis=1) * acc_scr[...] + pv
        m_scr[...] = m_new

    s_scr[0] = scores(0)

    for kc in range(n):
        if kc + 1 < n:
            s_scr[(kc + 1) % 2] = scores(kc + 1)
        softmax_pv(kc, kc % 2)
    o = acc_scr[:, :HEAD_DIM] / acc_scr[:, HEAD_DIM:]
    o_ref[:, :HEAD_DIM] = o[:tq].astype(o_ref.dtype)
    o_ref[:, HEAD_DIM:] = o[tq:].astype(o_ref.dtype)


def _gqa_attention(qn, kn, vaug, *, bsz, seq, tq, tk):
    m = qn.shape[0]
    nq = seq // tq
    rep = GQA_HEADS // GQA_KV_HEADS
    return pl.pallas_call(
        functools.partial(_flash_kernel, tq=tq, tk=tk, seq=seq),
        grid=(bsz, GQA_KV_HEADS, nq),
        in_specs=[
            pl.BlockSpec((tq, rep * HEAD_DIM), lambda b, g, i: (b * nq + i, g)),
            pl.BlockSpec((seq, HEAD_DIM), lambda b, g, i: (b, g)),
            pl.BlockSpec((seq, 2 * HEAD_DIM), lambda b, g, i: (b, g)),
        ],
        out_specs=pl.BlockSpec((tq, rep * HEAD_DIM), lambda b, g, i: (b * nq + i, g)),
        out_shape=jax.ShapeDtypeStruct((m, GQA_WIDTH), BF16),
        scratch_shapes=[
            pltpu.VMEM((2, rep * tq, tk), F32),
            pltpu.VMEM((rep * tq, HEAD_DIM), F32),
            pltpu.VMEM((rep * tq, 2 * HEAD_DIM), F32),
        ],
        compiler_params=_cp("parallel", "parallel", "parallel"),
        name="gqa_flash",
    )(qn, kn, vaug)


def _with_halo(x, prev, nxt):
    return jnp.concatenate([x, nxt, prev], axis=0)


def _shift_rows(xcat, s, n):
    return pltpu.roll(xcat, s % xcat.shape[0], axis=0)[:n]


def _dwconv(x, prev, nxt, w, b, width):
    n = x.shape[0]
    xcat = _with_halo(x, prev, nxt)
    pad = width // 2
    out = b + x * w[pad:pad + 1, :]
    for i in range(width):
        if i != pad:
            out = out + _shift_rows(xcat, pad - i, n) * w[i:i + 1, :]
    return out


def _halo_specs(tm, width, col_block, m):
    nblk = m // HALO
    per = tm // HALO
    prev = pl.BlockSpec((HALO, width), lambda i, *_: (jnp.maximum(i * per - 1, 0), col_block))
    nxt = pl.BlockSpec((HALO, width), lambda i, *_: (jnp.minimum((i + 1) * per, nblk - 1), col_block))
    return prev, nxt


def _ssd_prep_kernel(x_ref, prev_ref, next_ref, dtr_ref, cw_ref, cb_ref, dtb_ref, xs_ref, bc_ref, dt_ref, *, tps):
    i = pl.program_id(0)
    first = (i % tps) == 0
    last = (i % tps) == tps - 1
    v = dtr_ref[...] + dtb_ref[...]
    dt_ref[...] = jnp.maximum(v, 0.0) + jnp.log(1.0 + jnp.exp(-jnp.abs(v)))
    x = x_ref[...].astype(F32)
    prev = jnp.where(first, 0.0, prev_ref[...].astype(F32))
    nxt = jnp.where(last, 0.0, next_ref[...].astype(F32))
    y = _silu(_dwconv(x, prev, nxt, cw_ref[...], cb_ref[...], SSD_CONV))
    xs_ref[...] = y[:, :SSD_WIDTH].astype(BF16)
    bc_ref[...] = y[:, SSD_WIDTH:].astype(BF16)


def _ssd_prep(proj, dt_raw, conv_w, conv_b, dt_bias, *, seq, tm):
    m = proj.shape[0]
    cb = OFF_XBC // SSD_XBC
    prev, nxt = _halo_specs(tm, SSD_XBC, cb, m)
    return pl.pallas_call(
        functools.partial(_ssd_prep_kernel, tps=seq // tm),
        grid=(m // tm,),
        in_specs=[
            pl.BlockSpec((tm, SSD_XBC), lambda i: (i, cb)),
            prev,
            nxt,
            pl.BlockSpec((tm, DT_PAD), lambda i: (i, 0)),
            pl.BlockSpec((SSD_CONV, SSD_XBC), lambda i: (0, 0)),
            pl.BlockSpec((1, SSD_XBC), lambda i: (0, 0)),
            pl.BlockSpec((1, DT_PAD), lambda i: (0, 0)),
        ],
        out_specs=[
            pl.BlockSpec((tm, SSD_WIDTH), lambda i: (i, 0)),
            pl.BlockSpec((tm, SSD_BC), lambda i: (i, 0)),
            pl.BlockSpec((tm, DT_PAD), lambda i: (i, 0)),
        ],
        out_shape=[
            jax.ShapeDtypeStruct((m, SSD_WIDTH), BF16),
            jax.ShapeDtypeStruct((m, SSD_BC), BF16),
            jax.ShapeDtypeStruct((m, DT_PAD), F32),
        ],
        compiler_params=_cp("parallel"),
        name="ssd_prep",
    )(proj, proj, proj, dt_raw, conv_w, conv_b, dt_bias)


def _ssd_constants():
    q = SSD_CHUNK
    li = np.arange(q)
    tri = np.stack([li[:, None] >= li[None, :], li[:, None] <= li[None, :]]).astype(np.float32)
    hp = np.arange(SSD_WIDTH) // SSD_HEAD_DIM
    expand = np.zeros((2, 2 * DT_PAD, SSD_WIDTH), np.float32)
    for d in range(2):
        expand[d, d * SSD_HEADS + hp, np.arange(SSD_WIDTH)] = 1.0
        expand[d, DT_PAD + d * SSD_HEADS + hp, np.arange(SSD_WIDTH)] = 1.0
    return jnp.asarray(tri, F32), jnp.asarray(expand, BF16)


def _expand_heads(v, e):
    hi = v.astype(BF16)
    lo = (v - hi.astype(F32)).astype(BF16)
    return jnp.dot(jnp.concatenate([hi, lo], axis=1), e, preferred_element_type=F32)


def _ssd_scan_kernel(xs_f, bc_f, dt_f, xs_b, bc_b, dt_b, alog_ref, tri_ref, exp_ref, yf_ref, yb_ref, h_scr):
    @pl.when(pl.program_id(1) == 0)
    def _():
        h_scr[...] = jnp.zeros(h_scr.shape, F32)

    q = SSD_CHUNK
    gw = SSD_WIDTH // SSD_GROUPS
    a_row = -jnp.exp(alog_ref[...])
    li = lax.broadcasted_iota(jnp.int32, (q, q), 0)
    si = lax.broadcasted_iota(jnp.int32, (q, q), 1)
    lane = lax.broadcasted_iota(jnp.int32, (q, HEAD_DIM), 1)
    left = lane < SSD_HEAD_DIM
    dirs = ((xs_f, bc_f, dt_f, yf_ref, q - 1), (xs_b, bc_b, dt_b, yb_ref, 0))
    for cc, (d, (xs_ref, bc_ref, dt_ref, y_ref, last)) in [(cc, dd) for cc in range(SSD_CPS) for dd in enumerate(dirs)]:
        rows = pl.ds((cc if d == 0 else SSD_CPS - 1 - cc) * q, q)
        keep = (li >= si) if d == 0 else (li <= si)
        e = exp_ref[d]
        dt = dt_ref[rows, :]
        da = dt * a_row
        cum = jnp.dot(tri_ref[d], da, precision=lax.Precision.HIGHEST, preferred_element_type=F32)
        cum_t = cum.T
        tot = cum[last:last + 1, :]
        stack_e = _expand_heads(jnp.concatenate([dt, jnp.exp(cum), jnp.exp(tot - cum)], axis=0), e)
        dt_e, ec_e, w_e = stack_e[:q], stack_e[q:2 * q], stack_e[2 * q:]
        dec_e = ec_e[last:last + 1, :]
        xd = xs_ref[rows, :].astype(F32) * dt_e
        xdb = xd.astype(BF16)
        xw = (xd * w_e).astype(BF16)
        for g in range(SSD_GROUPS):
            gl = slice(g * gw, (g + 1) * gw)
            bg = bc_ref[rows, g * SSD_STATE:(g + 1) * SSD_STATE]
            cg = bc_ref[rows, (SSD_GROUPS + g) * SSD_STATE:(SSD_GROUPS + g + 1) * SSD_STATE]
            cb = lax.dot_general(cg, bg, (((1,), (1,)), ((), ())), preferred_element_type=F32)
            bg_t = bg.astype(F32).T.astype(BF16)
            h_prev = h_scr[d, g]
            y_off = jnp.dot(cg, h_prev.astype(BF16), preferred_element_type=F32) * ec_e[:, gl]
            states_t = jnp.dot(bg_t, xw[:, gl], preferred_element_type=F32)
            h_scr[d, g] = h_prev * dec_e[:, gl] + states_t
            for j in range(gw // HEAD_DIM):
                mats = []
                for hh in range(2):
                    col = d * SSD_HEADS + g * (SSD_HEADS // SSD_GROUPS) + 2 * j + hh
                    seg = cum[:, col:col + 1] - cum_t[col:col + 1, :]
                    lm = jnp.exp(jnp.where(keep, seg, NEG))
                    mats.append((cb * lm).astype(BF16))
                lo = g * gw + j * HEAD_DIM
                slab = xdb[:, lo:lo + HEAD_DIM]
                zero = jnp.zeros_like(slab)
                wbd = jnp.concatenate([jnp.where(left, slab, zero), jnp.where(left, zero, slab)], axis=0)
                y_diag = jnp.dot(jnp.concatenate(mats, axis=1), wbd, preferred_element_type=F32)
                y_ref[rows, lo:lo + HEAD_DIM] = (y_diag + y_off[:, j * HEAD_DIM:(j + 1) * HEAD_DIM]).astype(y_ref.dtype)


def _ssd_scan(xs, bc, dt, a_log, tri, expand, *, bsz, seq):
    m = xs.shape[0]
    nc = seq // (SSD_CPS * SSD_CHUNK)
    blk = SSD_CPS * SSD_CHUNK
    fwd = lambda b, c: (b * nc + c, 0)
    bwd = lambda b, c: (b * nc + nc - 1 - c, 0)
    return pl.pallas_call(
        _ssd_scan_kernel,
        grid=(bsz, nc),
        in_specs=[
            pl.BlockSpec((blk,SSD_WIDTH), fwd),
            pl.BlockSpec((blk,SSD_BC), fwd),
            pl.BlockSpec((blk,DT_PAD), fwd),
            pl.BlockSpec((blk,SSD_WIDTH), bwd),
            pl.BlockSpec((blk,SSD_BC), bwd),
            pl.BlockSpec((blk,DT_PAD), bwd),
            pl.BlockSpec((1, DT_PAD), lambda b, c: (0, 0)),
            pl.BlockSpec((2, SSD_CHUNK, SSD_CHUNK), lambda b, c: (0, 0, 0)),
            pl.BlockSpec((2, 2 * DT_PAD, SSD_WIDTH), lambda b, c: (0, 0, 0)),
        ],
        out_specs=[
            pl.BlockSpec((blk,SSD_WIDTH), fwd),
            pl.BlockSpec((blk,SSD_WIDTH), bwd),
        ],
        out_shape=[
            jax.ShapeDtypeStruct((m, SSD_WIDTH), BF16),
            jax.ShapeDtypeStruct((m, SSD_WIDTH), BF16),
        ],
        scratch_shapes=[pltpu.VMEM((2, SSD_GROUPS, SSD_STATE, SSD_WIDTH // SSD_GROUPS), F32)],
        compiler_params=_cp("parallel", "arbitrary"),
        name="ssd_scan",
    )(xs, bc, dt, xs, bc, dt, a_log, tri, expand)


def _ssd_finish_kernel(yf_ref, yb_ref, xs_ref, z_ref, d_ref, nw_ref, o_ref):
    y = yf_ref[...].astype(F32) + yb_ref[...].astype(F32) + xs_ref[...].astype(F32) * d_ref[...]
    y = y * _silu(z_ref[...].astype(F32))
    gw = SSD_WIDTH // SSD_GROUPS
    for g in range(SSD_GROUPS):
        yg = y[:, g * gw:(g + 1) * gw]
        ms = jnp.mean(yg * yg, axis=-1, keepdims=True)
        o_ref[:, g * gw:(g + 1) * gw] = (yg * lax.rsqrt(ms + EPS) * nw_ref[:, g * gw:(g + 1) * gw]).astype(o_ref.dtype)


def _ssd_finish(yf, yb, xs, proj, d_exp, norm_w, *, tm):
    m = yf.shape[0]
    row = lambda i: (i, 0)
    const = lambda i: (0, 0)
    return pl.pallas_call(
        _ssd_finish_kernel,
        grid=(m // tm,),
        in_specs=[
            pl.BlockSpec((tm, SSD_WIDTH), row),
            pl.BlockSpec((tm, SSD_WIDTH), row),
            pl.BlockSpec((tm, SSD_WIDTH), row),
            pl.BlockSpec((tm, SSD_WIDTH), lambda i: (i, OFF_Z // SSD_WIDTH)),
            pl.BlockSpec((1, SSD_WIDTH), const),
            pl.BlockSpec((1, SSD_WIDTH), const),
        ],
        out_specs=pl.BlockSpec((tm, SSD_WIDTH), row),
        out_shape=jax.ShapeDtypeStruct((m, SSD_WIDTH), BF16),
        compiler_params=_cp("parallel"),
        name="ssd_finish",
    )(yf, yb, xs, proj, d_exp, norm_w)


def _merge_kernel(ona_ref, ossd_ref, ogqa_ref, g_ref, x_ref, wna_ref, wssd_ref, wgqa_ref, wout_ref, nw_ref,
                  o_ref, h_ref, mg_scr, *, tm):
    nc = 512
    branches = ((ona_ref, wna_ref), (ossd_ref, wssd_ref), (ogqa_ref, wgqa_ref))
    for c in range(D_MODEL // nc):
        cols = pl.ds(c * nc, nc)
        acc = None
        for b, (o_b, w_b) in enumerate(branches):
            p = jnp.dot(o_b[...], w_b[:, cols], preferred_element_type=F32)
            gate = _sigmoid(g_ref[:, pl.ds(b * D_MODEL + c * nc, nc)].astype(F32))
            acc = gate * p if acc is None else acc + gate * p
        mg_scr[:, cols] = acc.astype(BF16)
    for c in range(D_MODEL // nc):
        cols = pl.ds(c * nc, nc)
        o_ref[:, cols] = x_ref[:, cols] + jnp.dot(mg_scr[...], wout_ref[:, cols], preferred_element_type=F32)
    _rmsnorm_rows(o_ref, nw_ref, h_ref, tm)


def _merge(o_na, o_ssd, o_gqa, proj, x, w_na, w_ssd, w_gqa, w_out, norm_w, *, tm):
    m = x.shape[0]
    row = lambda i: (i, 0)
    const = lambda i: (0, 0)
    one = pl.Buffered(1)
    return pl.pallas_call(
        functools.partial(_merge_kernel, tm=tm),
        grid=(m // tm,),
        in_specs=[
            pl.BlockSpec((tm, NA_WIDTH), row),
            pl.BlockSpec((tm, SSD_WIDTH), row),
            pl.BlockSpec((tm, GQA_WIDTH), row),
            pl.BlockSpec((tm, GATES), row),
            pl.BlockSpec((tm, D_MODEL), row),
            pl.BlockSpec((NA_WIDTH, D_MODEL), const, pipeline_mode=one),
            pl.BlockSpec((SSD_WIDTH, D_MODEL), const, pipeline_mode=one),
            pl.BlockSpec((GQA_WIDTH, D_MODEL), const, pipeline_mode=one),
            pl.BlockSpec((D_MODEL, D_MODEL), const, pipeline_mode=one),
            pl.BlockSpec((1, D_MODEL), const),
        ],
        out_specs=[pl.BlockSpec((tm, D_MODEL), row), pl.BlockSpec((tm, D_MODEL), row)],
        out_shape=[jax.ShapeDtypeStruct((m, D_MODEL), F32), jax.ShapeDtypeStruct((m, D_MODEL), BF16)],
        scratch_shapes=[pltpu.VMEM((tm, D_MODEL), BF16)],
        compiler_params=_cp("parallel"),
        name="merge",
    )(o_na, o_ssd, o_gqa, proj, x, w_na, w_ssd, w_gqa, w_out, norm_w)


FFN_CHUNK = 256


def _ffn_down_kernel(*refs, tps, tm, with_norm):
    if with_norm:
        ug_ref, prev_ref, next_ref, uv_ref, cw_ref, cb_ref, wd_ref, x_ref, nw_ref, o_ref, h_ref, hid_scr = refs
    else:
        ug_ref, prev_ref, next_ref, uv_ref, cw_ref, cb_ref, wd_ref, x_ref, o_ref, hid_scr = refs
    i = pl.program_id(0)
    first = (i % tps) == 0
    last = (i % tps) == tps - 1

    def gated(c):
        cols = pl.ds(c * FFN_CHUNK, FFN_CHUNK)
        g = ug_ref[:, cols].astype(F32)
        prev = jnp.where(first, 0.0, prev_ref[:, cols].astype(F32))
        nxt = jnp.where(last, 0.0, next_ref[:, cols].astype(F32))
        a = _dwconv(g, prev, nxt, cw_ref[:, cols], cb_ref[:, cols], 3)
        hid_scr[:, cols] = (_silu(a) * uv_ref[:, cols].astype(F32)).astype(BF16)

    nchunk = D_FF // FFN_CHUNK
    gated(0)
    for c in range(nchunk):
        rows = pl.ds(c * FFN_CHUNK, FFN_CHUNK)
        part = jnp.dot(hid_scr[:, rows], wd_ref[rows, :], preferred_element_type=F32)
        if c + 1 < nchunk:
            gated(c + 1)
        if c == 0:
            o_ref[...] = x_ref[...] + part
        else:
            o_ref[...] += part
    if with_norm:
        _rmsnorm_rows(o_ref, nw_ref, h_ref, tm)


def _ffn_down(u, conv_w, conv_b, w_down, x, next_norm_w=None, *, seq, tm):
    m = x.shape[0]
    nblk = m // HALO
    per = tm // HALO
    const = lambda i: (0, 0)
    row = lambda i: (i, 0)
    with_norm = next_norm_w is not None
    in_specs = [
        pl.BlockSpec((tm, D_FF), lambda i: (i, 0)),
        pl.BlockSpec((HALO, D_FF), lambda i: (jnp.maximum(i * per - 1, 0), 0)),
        pl.BlockSpec((HALO, D_FF), lambda i: (jnp.minimum((i + 1) * per, nblk - 1), 0)),
        pl.BlockSpec((tm, D_FF), lambda i: (i, 1)),
        pl.BlockSpec((3, D_FF), const),
        pl.BlockSpec((1, D_FF), const),
        pl.BlockSpec((D_FF, D_MODEL), const, pipeline_mode=pl.Buffered(1)),
        pl.BlockSpec((tm, D_MODEL), row),
    ]
    out_specs = [pl.BlockSpec((tm, D_MODEL), row)]
    out_shape = [jax.ShapeDtypeStruct((m, D_MODEL), F32)]
    args = [u, u, u, u, conv_w, conv_b, w_down, x]
    if with_norm:
        in_specs.append(pl.BlockSpec((1, D_MODEL), const))
        out_specs.append(pl.BlockSpec((tm, D_MODEL), row))
        out_shape.append(jax.ShapeDtypeStruct((m, D_MODEL), BF16))
        args.append(next_norm_w)
    res = pl.pallas_call(
        functools.partial(_ffn_down_kernel, tps=seq // tm, tm=tm, with_norm=with_norm),
        grid=(m // tm,),
        in_specs=in_specs,
        out_specs=out_specs,
        out_shape=out_shape,
        scratch_shapes=[pltpu.VMEM((tm, D_FF), BF16)],
        compiler_params=_cp("parallel"),
        name="ffn_down",
    )(*args)
    return res if with_norm else (res[0], None)


MM_TN = 1024


def _tiles(bsz, seq):
    m = bsz * seq
    return dict(tm_mm=2048 if m % 2048 == 0 else 1024, tm_ew=min(seq, 1024), tm_merge=min(seq, 256),
                tm_down=min(seq, 256), tq=min(seq, 512), tk=min(seq, 1024))


def _prep_layer_params(p, na_bias):
    (norm1_w, w_in, na_q_norm, na_k_norm, _, ssd_conv_w, ssd_conv_b, ssd_dt_bias, ssd_a_log, ssd_d,
     ssd_norm_w, gqa_q_norm, gqa_k_norm, w_branch, w_out, norm2_w, ffn_w_up, ffn_conv_w, ffn_conv_b, ffn_w_down) = p
    sizes = (3 * NA_WIDTH, GQA_WIDTH, GQA_KV_WIDTH, GQA_KV_WIDTH, SSD_WIDTH, SSD_XBC, 2 * SSD_HEADS, GATES)
    offs = np.concatenate([[0], np.cumsum(sizes)])
    piece = lambda k: w_in[:, offs[k]:offs[k + 1]].astype(BF16)
    na, gq, gk, gv, z, xbc, dtw, gates = (piece(k) for k in range(8))
    w_main = jnp.concatenate([gates, xbc, na, z, gq, gk, gv], axis=1)
    w_dt = jnp.pad(dtw, ((0, 0), (0, DT_PAD - 2 * SSD_HEADS)))
    pad_dt = lambda v: jnp.pad(v.reshape(1, 2 * SSD_HEADS), ((0, 0), (0, DT_PAD - 2 * SSD_HEADS)))
    row = lambda v: v.reshape(1, -1)
    wb = w_branch.astype(BF16)
    return dict(
        norm1_w=row(norm1_w), w_main=w_main, w_dt=w_dt,
        na_q_norm=row(na_q_norm), na_k_norm=row(na_k_norm), na_bias=na_bias,
        conv_w=ssd_conv_w, conv_b=row(ssd_conv_b), dt_bias=pad_dt(ssd_dt_bias), a_log=pad_dt(ssd_a_log),
        d_exp=row(jnp.repeat(ssd_d, SSD_HEAD_DIM)), ssd_norm_w=row(ssd_norm_w),
        gqa_q_norm=row(gqa_q_norm), gqa_k_norm=row(gqa_k_norm),
        w_na=wb[:NA_WIDTH], w_ssd=wb[NA_WIDTH:NA_WIDTH + SSD_WIDTH], w_gqa=wb[NA_WIDTH + SSD_WIDTH:],
        w_out=w_out.astype(BF16), norm2_w=row(norm2_w), w_up=ffn_w_up.astype(BF16),
        ffn_conv_w=ffn_conv_w, ffn_conv_b=row(ffn_conv_b), w_down=ffn_w_down.astype(BF16),
    )


def _prep_params(params):
    depth = params[0].shape[0]
    na_bias = _na_bias_tables(params[4])
    return [_prep_layer_params([p[i] for p in params], na_bias[i]) for i in range(depth)]


def _layer(x, h, lp, next_norm_w, consts, *, bsz, seq):
    t = _tiles(bsz, seq)
    cos_t, sin_t, tri, expand = consts
    proj, dt_raw = _matmul(h, lp["w_main"], lp["w_dt"], tm=t["tm_mm"], tn=MM_TN)
    o_na = _na_attention(proj, lp["na_bias"], lp["na_q_norm"], lp["na_k_norm"], bsz=bsz, seq=seq)
    qn, kn, vaug = _gqa_prep(proj, cos_t, sin_t, lp["gqa_q_norm"], lp["gqa_k_norm"], seq=seq, tm=t["tm_ew"])
    o_gqa = _gqa_attention(qn, kn, vaug, bsz=bsz, seq=seq, tq=t["tq"], tk=t["tk"])
    xs, bc, dt = _ssd_prep(proj, dt_raw, lp["conv_w"], lp["conv_b"], lp["dt_bias"], seq=seq, tm=t["tm_ew"])
    yf, yb = _ssd_scan(xs, bc, dt, lp["a_log"], tri, expand, bsz=bsz, seq=seq)
    o_ssd = _ssd_finish(yf, yb, xs, proj, lp["d_exp"], lp["ssd_norm_w"], tm=t["tm_ew"])
    x, h2 = _merge(o_na, o_ssd, o_gqa, proj, x, lp["w_na"], lp["w_ssd"], lp["w_gqa"], lp["w_out"], lp["norm2_w"],
                   tm=t["tm_merge"])
    u = _matmul(h2, lp["w_up"], tm=t["tm_mm"], tn=MM_TN)
    return _ffn_down(u, lp["ffn_conv_w"], lp["ffn_conv_b"], lp["w_down"], x, next_norm_w, seq=seq, tm=t["tm_down"])


def _trunk(x, layer_params):
    bsz, seq, d = x.shape
    assert d == D_MODEL and seq % 1024 == 0 and seq // GRID_W >= NA_KROWS
    consts = _rope_tables(seq) + _ssd_constants()
    x = x.reshape(bsz * seq, d)
    h = _rmsnorm(x, layer_params[0]["norm1_w"], tm=_tiles(bsz, seq)["tm_ew"])
    for li, lp in enumerate(layer_params):
        nxt = layer_params[li + 1]["norm1_w"] if li + 1 < len(layer_params) else None
        x, h = _layer(x, h, lp, nxt, consts, bsz=bsz, seq=seq)
    return x.reshape(bsz, seq, d)


def kernel(x_prompt, x_sample, norm1_w, w_in, na_q_norm, na_k_norm, na_rpb, ssd_conv_w, ssd_conv_b, ssd_dt_bias, ssd_a_log, ssd_d, ssd_norm_w, gqa_q_norm, gqa_k_norm, w_branch, w_out, norm2_w, ffn_w_up, ffn_conv_w, ffn_conv_b, ffn_w_down):
    params = (norm1_w, w_in, na_q_norm, na_k_norm, na_rpb, ssd_conv_w, ssd_conv_b, ssd_dt_bias, ssd_a_log,
              ssd_d, ssd_norm_w, gqa_q_norm, gqa_k_norm, w_branch, w_out, norm2_w, ffn_w_up, ffn_conv_w,
              ffn_conv_b, ffn_w_down)
    layer_params = _prep_params(params)
    return (_trunk(x_prompt, layer_params), _trunk(x_sample, layer_params))
```

```python
import functools
import math

import numpy as np
import jax
import jax.numpy as jnp
from jax import lax
from jax.experimental import pallas as pl
from jax.experimental.pallas import tpu as pltpu

F32 = jnp.float32
BF16 = jnp.bfloat16

D_MODEL = 2048
GRID_W = 64
EPS = 1e-6
NA_HEADS = 4
HEAD_DIM = 128
NA_WIDTH = NA_HEADS * HEAD_DIM
NA_WIN_H = 8
NA_WIN_W = 16
SSD_HEADS = 16
SSD_HEAD_DIM = 64
SSD_WIDTH = SSD_HEADS * SSD_HEAD_DIM
SSD_STATE = 128
SSD_GROUPS = 2
SSD_CONV = 5
SSD_CHUNK = 128
SSD_CPS = 4
SSD_BC = 2 * SSD_GROUPS * SSD_STATE
SSD_XBC = SSD_WIDTH + SSD_BC
GQA_HEADS = 4
GQA_KV_HEADS = 2
GQA_WIDTH = GQA_HEADS * HEAD_DIM
GQA_KV_WIDTH = GQA_KV_HEADS * HEAD_DIM
ROPE_THETA = 10000.0
D_FF = 5632
N_BRANCH = 3
GATES = N_BRANCH * D_MODEL
DT_PAD = 128

OFF_GATES = 0
OFF_XBC = OFF_GATES + GATES
OFF_NA = OFF_XBC + SSD_XBC
OFF_Z = OFF_NA + 3 * NA_WIDTH
OFF_GQ = OFF_Z + SSD_WIDTH
OFF_GK = OFF_GQ + GQA_WIDTH
OFF_GV = OFF_GK + GQA_KV_WIDTH
PROJ_COLS = OFF_GV + GQA_KV_WIDTH

NEG = -1e30
LOG2E = math.log2(math.e)
HALO = 8
VMEM_LIMIT = 56 * 2**20


def _cp(*sem):
    return pltpu.CompilerParams(dimension_semantics=sem, vmem_limit_bytes=VMEM_LIMIT)


def _sigmoid(x):
    return 0.5 + 0.5 * jnp.tanh(0.5 * x)


def _silu_of_half(half):
    return half + half * jnp.tanh(half)


def _silu(x):
    return _silu_of_half(0.5 * x)


NORM_ROWS = 128


def _rmsnorm_rows(x_ref, nw_ref, h_ref, n_rows):
    def body(r, c):
        rows = pl.ds(pl.multiple_of(r * NORM_ROWS, NORM_ROWS), NORM_ROWS)
        x = x_ref[rows, :]
        ms = jnp.mean(x * x, axis=-1, keepdims=True)
        h_ref[rows, :] = (x * lax.rsqrt(ms + EPS) * nw_ref[...]).astype(BF16)
        return c

    lax.fori_loop(0, n_rows // NORM_ROWS, body, 0)


def _rmsnorm_kernel(x_ref, nw_ref, h_ref, *, tm):
    _rmsnorm_rows(x_ref, nw_ref, h_ref, tm)


def _rmsnorm(x, nw, *, tm):
    m, d = x.shape
    return pl.pallas_call(
        functools.partial(_rmsnorm_kernel, tm=tm),
        grid=(m // tm,),
        in_specs=[pl.BlockSpec((tm, d), lambda i: (i, 0)), pl.BlockSpec((1, d), lambda i: (0, 0))],
        out_specs=pl.BlockSpec((tm, d), lambda i: (i, 0)),
        out_shape=jax.ShapeDtypeStruct((m, d), BF16),
        compiler_params=_cp("parallel"),
        name="rmsnorm",
    )(x, nw)


def _mm_kernel(*refs, with_dt):
    if with_dt:
        h_ref, w_ref, wdt_ref, o_ref, odt_ref = refs

        @pl.when(pl.program_id(1) == 0)
        def _():
            odt_ref[...] = jnp.dot(h_ref[...], wdt_ref[...], preferred_element_type=F32)
    else:
        h_ref, w_ref, o_ref = refs
    o_ref[...] = jnp.dot(h_ref[...], w_ref[...], preferred_element_type=F32).astype(o_ref.dtype)


def _matmul(h, w, wdt=None, *, tm, tn):
    m, d = h.shape
    n = w.shape[1]
    with_dt = wdt is not None
    in_specs = [
        pl.BlockSpec((tm, d), lambda i, j: (i, 0)),
        pl.BlockSpec((d, tn), lambda i, j: (0, j)),
    ]
    out_shape = [jax.ShapeDtypeStruct((m, n), BF16)]
    out_specs = [pl.BlockSpec((tm, tn), lambda i, j: (i, j))]
    args = [h, w]
    if with_dt:
        in_specs.append(pl.BlockSpec((d, DT_PAD), lambda i, j: (0, 0)))
        out_shape.append(jax.ShapeDtypeStruct((m, DT_PAD), F32))
        out_specs.append(pl.BlockSpec((tm, DT_PAD), lambda i, j: (i, 0)))
        args.append(wdt)
    res = pl.pallas_call(
        functools.partial(_mm_kernel, with_dt=with_dt),
        grid=(m // tm, n // tn),
        in_specs=in_specs,
        out_specs=out_specs,
        out_shape=out_shape,
        compiler_params=_cp("parallel", "arbitrary"),
        name="matmul_dt" if with_dt else "matmul",
    )(*args)
    return res if with_dt else res[0]


NA_QROWS = 8
NA_KROWS = 16
NA_TQ = NA_QROWS * GRID_W
NA_TK = NA_KROWS * GRID_W
NA_KPIECE = 256
NA_NPIECE = NA_TK // NA_KPIECE


def _na_bias_tables(rpb):
    rows = 4 * NA_QROWS
    nb = rows // NA_QROWS
    depth = rpb.shape[0]
    c = np.arange(GRID_W)
    col_start = np.clip(c - NA_WIN_W // 2, 0, GRID_W - NA_WIN_W)
    in_win = (c[None, :] >= col_start[:, None]) & (c[None, :] < col_start[:, None] + NA_WIN_W)
    dc = np.clip(c[None, :] - c[:, None], -(NA_WIN_W - 1), NA_WIN_W - 1) + NA_WIN_W - 1
    onehot_c = (dc[:, :, None] == np.arange(2 * NA_WIN_W - 1)).astype(np.float32)
    by_col = jnp.einsum("lhab,qcb->lhaqc", rpb.astype(F32), jnp.asarray(onehot_c), precision=lax.Precision.HIGHEST)
    by_col = by_col * LOG2E
    tables = []
    for b in (0, 1, nb - 1):
        r = NA_QROWS * b + np.arange(NA_QROWS)
        ks = int(np.clip(NA_QROWS * b - NA_WIN_H // 2, 0, rows - NA_KROWS))
        krow = ks + np.arange(NA_KROWS)
        rs = np.clip(r - NA_WIN_H // 2, 0, rows - NA_WIN_H)
        valid = (krow[None, :] >= rs[:, None]) & (krow[None, :] < rs[:, None] + NA_WIN_H)
        dr = np.clip(krow[None, :] - r[:, None], -(NA_WIN_H - 1), NA_WIN_H - 1) + NA_WIN_H - 1
        bias = jnp.take(by_col, jnp.asarray(dr.reshape(-1)), axis=2)
        bias = bias.reshape(depth, NA_HEADS, NA_QROWS, NA_KROWS, GRID_W, GRID_W).transpose(0, 1, 2, 4, 3, 5)
        mask = valid[:, None, :, None] & in_win[None, :, None, :]
        bias = jnp.where(jnp.asarray(mask)[None, None], bias, NEG)
        tables.append(bias.reshape(depth, NA_HEADS, NA_TQ, NA_TK))
    return jnp.stack(tables, axis=1)


def _head_rmsnorm(x, w):
    ms = jnp.mean(x * x, axis=-1, keepdims=True)
    return x * lax.rsqrt(ms + EPS) * w


NA_SUB = 128
NA_HPS = 4


def _na_kernel(q_ref, k0, k1, k2, k3, v0, v1, v2, v3, bias_ref, qw_ref, kw_ref, o_ref, kn_scr, va_scr, s_scr, *, nb):
    i = pl.program_id(2)
    kind = jnp.where(i == 0, 0, jnp.where(i == nb - 1, 2, 1))
    scale = HEAD_DIM ** -0.5 * LOG2E
    ones = jnp.ones((NA_KPIECE, HEAD_DIM), BF16)
    for hh in range(NA_HPS):
        hc = pl.ds(hh * HEAD_DIM, HEAD_DIM)
        for t, (k_ref, v_ref) in enumerate(zip((k0, k1, k2, k3), (v0, v1, v2, v3))):
            piece = pl.ds(t * NA_KPIECE, NA_KPIECE)
            kn_scr[hh, piece, :] = _head_rmsnorm(k_ref[:, hc].astype(F32), kw_ref[...]).astype(BF16)
            va_scr[hh, piece, :HEAD_DIM] = v_ref[:, hc]
            va_scr[hh, piece, HEAD_DIM:] = ones
    nlane = NA_TK // HEAD_DIM
    nsub = NA_TQ // NA_SUB

    def scores(u):
        hh, qs = divmod(u, nsub)
        rows = pl.ds(qs * NA_SUB, NA_SUB)
        qn = (_head_rmsnorm(q_ref[rows, pl.ds(hh * HEAD_DIM, HEAD_DIM)].astype(F32), qw_ref[...]) * scale).astype(BF16)
        s = lax.dot_general(qn, kn_scr[hh], (((1,), (1,)), ((), ())), preferred_element_type=F32)
        return s + bias_ref[kind, hh, rows, :]

    def finish(u, slot):
        hh, qs = divmod(u, nsub)
        blocks = [s_scr[slot, :, j * HEAD_DIM:(j + 1) * HEAD_DIM] for j in range(nlane)]
        part = blocks[0]
        for blk in blocks[1:]:
            part = jnp.maximum(part, blk)
        m = jnp.broadcast_to(jnp.max(part, axis=-1, keepdims=True), part.shape)
        p = jnp.concatenate([jnp.exp2(blk - m).astype(BF16) for blk in blocks], axis=1)
        pv = jnp.dot(p, va_scr[hh], preferred_element_type=F32)
        o_ref[pl.ds(qs * NA_SUB, NA_SUB), pl.ds(hh * HEAD_DIM, HEAD_DIM)] = (
            pv[:, :HEAD_DIM] / pv[:, HEAD_DIM:]).astype(o_ref.dtype)

    units = NA_HPS * nsub
    s_scr[0] = scores(0)
    for u in range(units):
        if u + 1 < units:
            s_scr[(u + 1) % 2] = scores(u + 1)
        finish(u, u % 2)


def _na_attention(proj, bias, qw, kw, *, bsz, seq):
    m = proj.shape[0]
    rows = seq // GRID_W
    nb = rows // NA_QROWS
    npieces = seq // NA_KPIECE
    width = NA_HPS * HEAD_DIM
    q_col = OFF_NA // width
    k_col = (OFF_NA + NA_WIDTH) // width
    v_col = (OFF_NA + 2 * NA_WIDTH) // width

    def kbase(i):
        return jnp.clip(2 * i - 1, 0, npieces - NA_NPIECE)

    def kv_spec(col, t):
        return pl.BlockSpec((NA_KPIECE, width), lambda h, b, i: (b * npieces + kbase(i) + t, col + h))

    in_specs = [pl.BlockSpec((NA_TQ, width), lambda h, b, i: (b * nb + i, q_col + h))]
    in_specs += [kv_spec(k_col, t) for t in range(NA_NPIECE)]
    in_specs += [kv_spec(v_col, t) for t in range(NA_NPIECE)]
    in_specs += [
        pl.BlockSpec((3, NA_HPS, NA_TQ, NA_TK), lambda h, b, i: (0, h, 0, 0), pipeline_mode=pl.Buffered(1)),
        pl.BlockSpec((1, HEAD_DIM), lambda h, b, i: (0, 0)),
        pl.BlockSpec((1, HEAD_DIM), lambda h, b, i: (0, 0)),
    ]
    return pl.pallas_call(
        functools.partial(_na_kernel, nb=nb),
        grid=(NA_HEADS // NA_HPS, bsz, nb),
        in_specs=in_specs,
        out_specs=pl.BlockSpec((NA_TQ, width), lambda h, b, i: (b * nb + i, h)),
        out_shape=jax.ShapeDtypeStruct((m, NA_WIDTH), BF16),
        scratch_shapes=[
            pltpu.VMEM((NA_HPS, NA_TK, HEAD_DIM), BF16),
            pltpu.VMEM((NA_HPS, NA_TK, 2 * HEAD_DIM), BF16),
            pltpu.VMEM((2, NA_SUB, NA_TK), F32),
        ],
        compiler_params=_cp("parallel", "parallel", "parallel"),
        name="na_attention",
    )(proj, *([proj] * (2 * NA_NPIECE)), bias, qw, kw)


def _rope_tables(seq):
    t = np.arange(seq)
    pos = np.stack([t // GRID_W, t % GRID_W], axis=-1).astype(np.float32)
    n_freq = HEAD_DIM // 4
    inv_freq = jnp.asarray(ROPE_THETA, F32) ** (-jnp.arange(n_freq, dtype=F32) / n_freq)
    ang = jnp.asarray(pos)[:, :, None] * inv_freq
    cos = jnp.cos(ang)
    sin = jnp.sin(ang)
    cos_t = jnp.concatenate([cos[:, 0], cos[:, 0], cos[:, 1], cos[:, 1]], axis=-1)
    sin_t = jnp.concatenate([-sin[:, 0], sin[:, 0], -sin[:, 1], sin[:, 1]], axis=-1)
    return cos_t, sin_t


def _rope(x, cos_t, sin_t, first_half):
    swapped = jnp.where(first_half, pltpu.roll(x, HEAD_DIM - 32, axis=1), pltpu.roll(x, 32, axis=1))
    return x * cos_t + swapped * sin_t


def _gqa_prep_kernel(q_ref, k_ref, v_ref, cos_ref, sin_ref, qw_ref, kw_ref, qo_ref, ko_ref, vo_ref):
    cos_t = cos_ref[...]
    sin_t = sin_ref[...]
    lane = lax.broadcasted_iota(jnp.int32, cos_t.shape, 1)
    first_half = (lane % 64) < 32
    scale = HEAD_DIM ** -0.5 * LOG2E
    for h in range(GQA_HEADS):
        cols = pl.ds(h * HEAD_DIM, HEAD_DIM)
        x = _head_rmsnorm(q_ref[:, cols].astype(F32), qw_ref[...])
        qo_ref[:, cols] = (_rope(x, cos_t, sin_t, first_half) * scale).astype(BF16)
    ones = jnp.ones(cos_t.shape, BF16)
    for h in range(GQA_KV_HEADS):
        cols = pl.ds(h * HEAD_DIM, HEAD_DIM)
        x = _head_rmsnorm(k_ref[:, cols].astype(F32), kw_ref[...])
        ko_ref[:, cols] = _rope(x, cos_t, sin_t, first_half).astype(BF16)
        vo_ref[:, pl.ds(2 * h * HEAD_DIM, HEAD_DIM)] = v_ref[:, cols]
        vo_ref[:, pl.ds((2 * h + 1) * HEAD_DIM, HEAD_DIM)] = ones


def _gqa_prep(proj, cos_t, sin_t, qw, kw, *, seq, tm):
    m = proj.shape[0]
    tps = seq // tm
    return pl.pallas_call(
        _gqa_prep_kernel,
        grid=(m // tm,),
        in_specs=[
            pl.BlockSpec((tm, GQA_WIDTH), lambda i: (i, OFF_GQ // GQA_WIDTH)),
            pl.BlockSpec((tm, GQA_KV_WIDTH), lambda i: (i, OFF_GK // GQA_KV_WIDTH)),
            pl.BlockSpec((tm, GQA_KV_WIDTH), lambda i: (i, OFF_GV // GQA_KV_WIDTH)),
            pl.BlockSpec((tm, HEAD_DIM), lambda i: (i % tps, 0)),
            pl.BlockSpec((tm, HEAD_DIM), lambda i: (i % tps, 0)),
            pl.BlockSpec((1, HEAD_DIM), lambda i: (0, 0)),
            pl.BlockSpec((1, HEAD_DIM), lambda i: (0, 0)),
        ],
        out_specs=[
            pl.BlockSpec((tm, GQA_WIDTH), lambda i: (i, 0)),
            pl.BlockSpec((tm, GQA_KV_WIDTH), lambda i: (i, 0)),
            pl.BlockSpec((tm, 2 * GQA_KV_WIDTH), lambda i: (i, 0)),
        ],
        out_shape=[
            jax.ShapeDtypeStruct((m, GQA_WIDTH), BF16),
            jax.ShapeDtypeStruct((m, GQA_KV_WIDTH), BF16),
            jax.ShapeDtypeStruct((m, 2 * GQA_KV_WIDTH), BF16),
        ],
        compiler_params=_cp("parallel"),
        name="gqa_prep",
    )(proj, proj, proj, cos_t, sin_t, qw, kw)


def _flash_kernel(q_ref, k_ref, v_ref, o_ref, s_scr, m_scr, acc_scr, *, tq, tk, seq):
    q2 = jnp.concatenate([q_ref[:, :HEAD_DIM], q_ref[:, HEAD_DIM:]], axis=0)
    m_scr[...] = jnp.full(m_scr.shape, NEG, F32)
    acc_scr[...] = jnp.zeros(acc_scr.shape, F32)
    nblk = tk // HEAD_DIM
    n = seq // tk

    def chunk(kc):
        return pl.ds(kc * tk, tk)

    def scores(kc):
        return lax.dot_general(q2, k_ref[chunk(kc), :], (((1,), (1,)), ((), ())), preferred_element_type=F32)

    def softmax_pv(kc, slot):
        blocks = [s_scr[slot, :, j * HEAD_DIM:(j + 1) * HEAD_DIM] for j in range(nblk)]
        part = blocks[0]
        for blk in blocks[1:]:
            part = jnp.maximum(part, blk)
        m_prev = m_scr[...]
        m_new = jnp.maximum(m_prev, jnp.max(part, axis=-1, keepdims=True))
        alpha = jnp.exp2(m_prev - m_new)
        p = jnp.concatenate([jnp.exp2(blk - m_new).astype(BF16) for blk in blocks], axis=1)
        pv = jnp.dot(p, v_ref[chunk(kc), :], preferred_element_type=F32)
        acc_scr[...] = jnp.concatenate([alpha, alpha], axis=1) * acc_scr[...] + pv
        m_scr[...] = m_new

    s_scr[0] = scores(0)

    for kc in range(n):
        if kc + 1 < n:
            s_scr[(kc + 1) % 2] = scores(kc + 1)
        softmax_pv(kc, kc % 2)
    o = acc_scr[:, :HEAD_DIM] / acc_scr[:, HEAD_DIM:]
    o_ref[:, :HEAD_DIM] = o[:tq].astype(o_ref.dtype)
    o_ref[:, HEAD_DIM:] = o[tq:].astype(o_ref.dtype)


def _gqa_attention(qn, kn, vaug, *, bsz, seq, tq, tk):
    m = qn.shape[0]
    nq = seq // tq
    rep = GQA_HEADS // GQA_KV_HEADS
    return pl.pallas_call(
        functools.partial(_flash_kernel, tq=tq, tk=tk, seq=seq),
        grid=(bsz, GQA_KV_HEADS, nq),
        in_specs=[
            pl.BlockSpec((tq, rep * HEAD_DIM), lambda b, g, i: (b * nq + i, g)),
            pl.BlockSpec((seq, HEAD_DIM), lambda b, g, i: (b, g)),
            pl.BlockSpec((seq, 2 * HEAD_DIM), lambda b, g, i: (b, g)),
        ],
        out_specs=pl.BlockSpec((tq, rep * HEAD_DIM), lambda b, g, i: (b * nq + i, g)),
        out_shape=jax.ShapeDtypeStruct((m, GQA_WIDTH), BF16),
        scratch_shapes=[
            pltpu.VMEM((2, rep * tq, tk), F32),
            pltpu.VMEM((rep * tq, HEAD_DIM), F32),
            pltpu.VMEM((rep * tq, 2 * HEAD_DIM), F32),
        ],
        compiler_params=_cp("parallel", "parallel", "parallel"),
        name="gqa_flash",
    )(qn, kn, vaug)


def _with_halo(x, prev, nxt):
    return jnp.concatenate([x, nxt, prev], axis=0)


def _shift_rows(xcat, s, n):
    return pltpu.roll(xcat, s % xcat.shape[0], axis=0)[:n]


def _dwconv(x, prev, nxt, w, b, width):
    n = x.shape[0]
    xcat = _with_halo(x, prev, nxt)
    pad = width // 2
    out = b + x * w[pad:pad + 1, :]
    for i in range(width):
        if i != pad:
            out = out + _shift_rows(xcat, pad - i, n) * w[i:i + 1, :]
    return out


def _halo_specs(tm, width, col_block, m):
    nblk = m // HALO
    per = tm // HALO
    prev = pl.BlockSpec((HALO, width), lambda i, *_: (jnp.maximum(i * per - 1, 0), col_block))
    nxt = pl.BlockSpec((HALO, width), lambda i, *_: (jnp.minimum((i + 1) * per, nblk - 1), col_block))
    return prev, nxt


def _ssd_prep_kernel(x_ref, prev_ref, next_ref, dtr_ref, cw_ref, cb_ref, dtb_ref, xs_ref, bc_ref, dt_ref, *, tps):
    i = pl.program_id(0)
    first = (i % tps) == 0
    last = (i % tps) == tps - 1
    v = dtr_ref[...] + dtb_ref[...]
    dt_ref[...] = jnp.maximum(v, 0.0) + jnp.log(1.0 + jnp.exp(-jnp.abs(v)))
    x = x_ref[...].astype(F32)
    prev = jnp.where(first, 0.0, prev_ref[...].astype(F32))
    nxt = jnp.where(last, 0.0, next_ref[...].astype(F32))
    y = _silu_of_half(_dwconv(x, prev, nxt, 0.5 * cw_ref[...], 0.5 * cb_ref[...], SSD_CONV))
    xs_ref[...] = y[:, :SSD_WIDTH].astype(BF16)
    bc_ref[...] = y[:, SSD_WIDTH:].astype(BF16)


def _ssd_prep(proj, dt_raw, conv_w, conv_b, dt_bias, *, seq, tm):
    m = proj.shape[0]
    cb = OFF_XBC // SSD_XBC
    prev, nxt = _halo_specs(tm, SSD_XBC, cb, m)
    return pl.pallas_call(
        functools.partial(_ssd_prep_kernel, tps=seq // tm),
        grid=(m // tm,),
        in_specs=[
            pl.BlockSpec((tm, SSD_XBC), lambda i: (i, cb)),
            prev,
            nxt,
            pl.BlockSpec((tm, DT_PAD), lambda i: (i, 0)),
            pl.BlockSpec((SSD_CONV, SSD_XBC), lambda i: (0, 0)),
            pl.BlockSpec((1, SSD_XBC), lambda i: (0, 0)),
            pl.BlockSpec((1, DT_PAD), lambda i: (0, 0)),
        ],
        out_specs=[
            pl.BlockSpec((tm, SSD_WIDTH), lambda i: (i, 0)),
            pl.BlockSpec((tm, SSD_BC), lambda i: (i, 0)),
            pl.BlockSpec((tm, DT_PAD), lambda i: (i, 0)),
        ],
        out_shape=[
            jax.ShapeDtypeStruct((m, SSD_WIDTH), BF16),
            jax.ShapeDtypeStruct((m, SSD_BC), BF16),
            jax.ShapeDtypeStruct((m, DT_PAD), F32),
        ],
        compiler_params=_cp("parallel"),
        name="ssd_prep",
    )(proj, proj, proj, dt_raw, conv_w, conv_b, dt_bias)


def _ssd_constants():
    q = SSD_CHUNK
    li = np.arange(q)
    tri = np.stack([li[:, None] >= li[None, :], li[:, None] <= li[None, :]]).astype(np.float32)
    hp = np.arange(SSD_WIDTH) // SSD_HEAD_DIM
    expand = np.zeros((2, 2 * DT_PAD, SSD_WIDTH), np.float32)
    for d in range(2):
        expand[d, d * SSD_HEADS + hp, np.arange(SSD_WIDTH)] = 1.0
        expand[d, DT_PAD + d * SSD_HEADS + hp, np.arange(SSD_WIDTH)] = 1.0
    return jnp.asarray(tri, F32), jnp.asarray(expand, BF16)


def _expand_heads(v, e):
    hi = v.astype(BF16)
    lo = (v - hi.astype(F32)).astype(BF16)
    return jnp.dot(jnp.concatenate([hi, lo], axis=1), e, preferred_element_type=F32)


def _ssd_scan_kernel(xs_f, bc_f, dt_f, xs_b, bc_b, dt_b, alog_ref, tri_ref, exp_ref, yf_ref, yb_ref, h_scr):
    @pl.when(pl.program_id(1) == 0)
    def _():
        h_scr[...] = jnp.zeros(h_scr.shape, F32)

    q = SSD_CHUNK
    gw = SSD_WIDTH // SSD_GROUPS
    a_row = -jnp.exp(alog_ref[...])
    li = lax.broadcasted_iota(jnp.int32, (q, q), 0)
    si = lax.broadcasted_iota(jnp.int32, (q, q), 1)
    lane = lax.broadcasted_iota(jnp.int32, (q, HEAD_DIM), 1)
    left = lane < SSD_HEAD_DIM
    dirs = ((xs_f, bc_f, dt_f, yf_ref, q - 1), (xs_b, bc_b, dt_b, yb_ref, 0))
    for cc, (d, (xs_ref, bc_ref, dt_ref, y_ref, last)) in [(cc, dd) for cc in range(SSD_CPS) for dd in enumerate(dirs)]:
        rows = pl.ds((cc if d == 0 else SSD_CPS - 1 - cc) * q, q)
        keep = (li >= si) if d == 0 else (li <= si)
        e = exp_ref[d]
        dt = dt_ref[rows, :]
        da = dt * a_row
        cum = jnp.dot(tri_ref[d], da, precision=lax.Precision.HIGHEST, preferred_element_type=F32)
        cum_t = cum.T
        tot = cum[last:last + 1, :]
        stack_e = _expand_heads(jnp.concatenate([dt, jnp.exp(cum), jnp.exp(tot - cum)], axis=0), e)
        dt_e, ec_e, w_e = stack_e[:q], stack_e[q:2 * q], stack_e[2 * q:]
        dec_e = ec_e[last:last + 1, :]
        xd = xs_ref[rows, :].astype(F32) * dt_e
        xdb = xd.astype(BF16)
        xw = (xd * w_e).astype(BF16)
        for g in range(SSD_GROUPS):
            gl = slice(g * gw, (g + 1) * gw)
            bg = bc_ref[rows, g * SSD_STATE:(g + 1) * SSD_STATE]
            cg = bc_ref[rows, (SSD_GROUPS + g) * SSD_STATE:(SSD_GROUPS + g + 1) * SSD_STATE]
            cb = lax.dot_general(cg, bg, (((1,), (1,)), ((), ())), preferred_element_type=F32)
            bg_t = bg.astype(F32).T.astype(BF16)
            h_prev = h_scr[d, g]
            y_off = jnp.dot(cg, h_prev.astype(BF16), preferred_element_type=F32) * ec_e[:, gl]
            states_t = jnp.dot(bg_t, xw[:, gl], preferred_element_type=F32)
            h_scr[d, g] = h_prev * dec_e[:, gl] + states_t
            for j in range(gw // HEAD_DIM):
                mats = []
                for hh in range(2):
                    col = d * SSD_HEADS + g * (SSD_HEADS // SSD_GROUPS) + 2 * j + hh
                    seg = cum[:, col:col + 1] - cum_t[col:col + 1, :]
                    lm = jnp.exp(jnp.where(keep, seg, NEG))
                    mats.append((cb * lm).astype(BF16))
                lo = g * gw + j * HEAD_DIM
                slab = xdb[:, lo:lo + HEAD_DIM]
                zero = jnp.zeros_like(slab)
                wbd = jnp.concatenate([jnp.where(left, slab, zero), jnp.where(left, zero, slab)], axis=0)
                y_diag = jnp.dot(jnp.concatenate(mats, axis=1), wbd, preferred_element_type=F32)
                y_ref[rows, lo:lo + HEAD_DIM] = (y_diag + y_off[:, j * HEAD_DIM:(j + 1) * HEAD_DIM]).astype(y_ref.dtype)


def _ssd_scan(xs, bc, dt, a_log, tri, expand, *, bsz, seq):
    m = xs.shape[0]
    nc = seq // (SSD_CPS * SSD_CHUNK)
    blk = SSD_CPS * SSD_CHUNK
    fwd = lambda b, c: (b * nc + c, 0)
    bwd = lambda b, c: (b * nc + nc - 1 - c, 0)
    return pl.pallas_call(
        _ssd_scan_kernel,
        grid=(bsz, nc),
        in_specs=[
            pl.BlockSpec((blk,SSD_WIDTH), fwd),
            pl.BlockSpec((blk,SSD_BC), fwd),
            pl.BlockSpec((blk,DT_PAD), fwd),
            pl.BlockSpec((blk,SSD_WIDTH), bwd),
            pl.BlockSpec((blk,SSD_BC), bwd),
            pl.BlockSpec((blk,DT_PAD), bwd),
            pl.BlockSpec((1, DT_PAD), lambda b, c: (0, 0)),
            pl.BlockSpec((2, SSD_CHUNK, SSD_CHUNK), lambda b, c: (0, 0, 0)),
            pl.BlockSpec((2, 2 * DT_PAD, SSD_WIDTH), lambda b, c: (0, 0, 0)),
        ],
        out_specs=[
            pl.BlockSpec((blk,SSD_WIDTH), fwd),
            pl.BlockSpec((blk,SSD_WIDTH), bwd),
        ],
        out_shape=[
            jax.ShapeDtypeStruct((m, SSD_WIDTH), BF16),
            jax.ShapeDtypeStruct((m, SSD_WIDTH), BF16),
        ],
        scratch_shapes=[pltpu.VMEM((2, SSD_GROUPS, SSD_STATE, SSD_WIDTH // SSD_GROUPS), F32)],
        compiler_params=_cp("parallel", "arbitrary"),
        name="ssd_scan",
    )(xs, bc, dt, xs, bc, dt, a_log, tri, expand)


def _ssd_finish_kernel(yf_ref, yb_ref, xs_ref, z_ref, d_ref, nw_ref, o_ref):
    y = yf_ref[...].astype(F32) + yb_ref[...].astype(F32) + xs_ref[...].astype(F32) * d_ref[...]
    y = y * _silu(z_ref[...].astype(F32))
    gw = SSD_WIDTH // SSD_GROUPS
    for g in range(SSD_GROUPS):
        yg = y[:, g * gw:(g + 1) * gw]
        ms = jnp.mean(yg * yg, axis=-1, keepdims=True)
        o_ref[:, g * gw:(g + 1) * gw] = (yg * lax.rsqrt(ms + EPS) * nw_ref[:, g * gw:(g + 1) * gw]).astype(o_ref.dtype)


def _ssd_finish(yf, yb, xs, proj, d_exp, norm_w, *, tm):
    m = yf.shape[0]
    row = lambda i: (i, 0)
    const = lambda i: (0, 0)
    return pl.pallas_call(
        _ssd_finish_kernel,
        grid=(m // tm,),
        in_specs=[
            pl.BlockSpec((tm, SSD_WIDTH), row),
            pl.BlockSpec((tm, SSD_WIDTH), row),
            pl.BlockSpec((tm, SSD_WIDTH), row),
            pl.BlockSpec((tm, SSD_WIDTH), lambda i: (i, OFF_Z // SSD_WIDTH)),
            pl.BlockSpec((1, SSD_WIDTH), const),
            pl.BlockSpec((1, SSD_WIDTH), const),
        ],
        out_specs=pl.BlockSpec((tm, SSD_WIDTH), row),
        out_shape=jax.ShapeDtypeStruct((m, SSD_WIDTH), BF16),
        compiler_params=_cp("parallel"),
        name="ssd_finish",
    )(yf, yb, xs, proj, d_exp, norm_w)


def _merge_kernel(ona_ref, ossd_ref, ogqa_ref, g_ref, x_ref, wna_ref, wssd_ref, wgqa_ref, wout_ref, nw_ref,
                  o_ref, h_ref, mg_scr, *, tm):
    nc = 512
    branches = ((ona_ref, wna_ref), (ossd_ref, wssd_ref), (ogqa_ref, wgqa_ref))
    for c in range(D_MODEL // nc):
        cols = pl.ds(c * nc, nc)
        acc = None
        for b, (o_b, w_b) in enumerate(branches):
            p = jnp.dot(o_b[...], w_b[:, cols], preferred_element_type=F32)
            gate = _sigmoid(g_ref[:, pl.ds(b * D_MODEL + c * nc, nc)].astype(F32))
            acc = gate * p if acc is None else acc + gate * p
        mg_scr[:, cols] = acc.astype(BF16)
    for c in range(D_MODEL // nc):
        cols = pl.ds(c * nc, nc)
        o_ref[:, cols] = x_ref[:, cols] + jnp.dot(mg_scr[...], wout_ref[:, cols], preferred_element_type=F32)
    _rmsnorm_rows(o_ref, nw_ref, h_ref, tm)


def _merge(o_na, o_ssd, o_gqa, proj, x, w_na, w_ssd, w_gqa, w_out, norm_w, *, tm):
    m = x.shape[0]
    row = lambda i: (i, 0)
    const = lambda i: (0, 0)
    one = pl.Buffered(1)
    return pl.pallas_call(
        functools.partial(_merge_kernel, tm=tm),
        grid=(m // tm,),
        in_specs=[
            pl.BlockSpec((tm, NA_WIDTH), row),
            pl.BlockSpec((tm, SSD_WIDTH), row),
            pl.BlockSpec((tm, GQA_WIDTH), row),
            pl.BlockSpec((tm, GATES), row),
            pl.BlockSpec((tm, D_MODEL), row),
            pl.BlockSpec((NA_WIDTH, D_MODEL), const, pipeline_mode=one),
            pl.BlockSpec((SSD_WIDTH, D_MODEL), const, pipeline_mode=one),
            pl.BlockSpec((GQA_WIDTH, D_MODEL), const, pipeline_mode=one),
            pl.BlockSpec((D_MODEL, D_MODEL), const, pipeline_mode=one),
            pl.BlockSpec((1, D_MODEL), const),
        ],
        out_specs=[pl.BlockSpec((tm, D_MODEL), row), pl.BlockSpec((tm, D_MODEL), row)],
        out_shape=[jax.ShapeDtypeStruct((m, D_MODEL), F32), jax.ShapeDtypeStruct((m, D_MODEL), BF16)],
        scratch_shapes=[pltpu.VMEM((tm, D_MODEL), BF16)],
        compiler_params=_cp("parallel"),
        name="merge",
    )(o_na, o_ssd, o_gqa, proj, x, w_na, w_ssd, w_gqa, w_out, norm_w)


FFN_CHUNK = 256


def _ffn_down_kernel(*refs, tps, tm, with_norm):
    if with_norm:
        ug_ref, prev_ref, next_ref, uv_ref, cw_ref, cb_ref, wd_ref, x_ref, nw_ref, o_ref, h_ref, hid_scr = refs
    else:
        ug_ref, prev_ref, next_ref, uv_ref, cw_ref, cb_ref, wd_ref, x_ref, o_ref, hid_scr = refs
    i = pl.program_id(0)
    first = (i % tps) == 0
    last = (i % tps) == tps - 1

    def gated(c):
        cols = pl.ds(c * FFN_CHUNK, FFN_CHUNK)
        g = ug_ref[:, cols].astype(F32)
        prev = jnp.where(first, 0.0, prev_ref[:, cols].astype(F32))
        nxt = jnp.where(last, 0.0, next_ref[:, cols].astype(F32))
        half = _dwconv(g, prev, nxt, 0.5 * cw_ref[:, cols], 0.5 * cb_ref[:, cols], 3)
        hid_scr[:, cols] = (_silu_of_half(half) * uv_ref[:, cols].astype(F32)).astype(BF16)

    nchunk = D_FF // FFN_CHUNK
    gated(0)
    for c in range(nchunk):
        rows = pl.ds(c * FFN_CHUNK, FFN_CHUNK)
        part = jnp.dot(hid_scr[:, rows], wd_ref[rows, :], preferred_element_type=F32)
        if c + 1 < nchunk:
            gated(c + 1)
        if c == 0:
            o_ref[...] = x_ref[...] + part
        else:
            o_ref[...] += part
    if with_norm:
        _rmsnorm_rows(o_ref, nw_ref, h_ref, tm)


def _ffn_down(u, conv_w, conv_b, w_down, x, next_norm_w=None, *, seq, tm):
    m = x.shape[0]
    nblk = m // HALO
    per = tm // HALO
    const = lambda i: (0, 0)
    row = lambda i: (i, 0)
    with_norm = next_norm_w is not None
    in_specs = [
        pl.BlockSpec((tm, D_FF), lambda i: (i, 0)),
        pl.BlockSpec((HALO, D_FF), lambda i: (jnp.maximum(i * per - 1, 0), 0)),
        pl.BlockSpec((HALO, D_FF), lambda i: (jnp.minimum((i + 1) * per, nblk - 1), 0)),
        pl.BlockSpec((tm, D_FF), lambda i: (i, 1)),
        pl.BlockSpec((3, D_FF), const),
        pl.BlockSpec((1, D_FF), const),
        pl.BlockSpec((D_FF, D_MODEL), const, pipeline_mode=pl.Buffered(1)),
        pl.BlockSpec((tm, D_MODEL), row),
    ]
    out_specs = [pl.BlockSpec((tm, D_MODEL), row)]
    out_shape = [jax.ShapeDtypeStruct((m, D_MODEL), F32)]
    args = [u, u, u, u, conv_w, conv_b, w_down, x]
    if with_norm:
        in_specs.append(pl.BlockSpec((1, D_MODEL), const))
        out_specs.append(pl.BlockSpec((tm, D_MODEL), row))
        out_shape.append(jax.ShapeDtypeStruct((m, D_MODEL), BF16))
        args.append(next_norm_w)
    res = pl.pallas_call(
        functools.partial(_ffn_down_kernel, tps=seq // tm, tm=tm, with_norm=with_norm),
        grid=(m // tm,),
        in_specs=in_specs,
        out_specs=out_specs,
        out_shape=out_shape,
        scratch_shapes=[pltpu.VMEM((tm, D_FF), BF16)],
        compiler_params=_cp("parallel"),
        name="ffn_down",
    )(*args)
    return res if with_norm else (res[0], None)


MM_TN = 1024


def _tiles(bsz, seq):
    m = bsz * seq
    return dict(tm_mm=2048 if m % 2048 == 0 else 1024, tm_ew=min(seq, 1024), tm_merge=min(seq, 256),
                tm_down=min(seq, 256), tq=min(seq, 512), tk=min(seq, 1024))


def _prep_layer_params(p, na_bias):
    (norm1_w, w_in, na_q_norm, na_k_norm, _, ssd_conv_w, ssd_conv_b, ssd_dt_bias, ssd_a_log, ssd_d,
     ssd_norm_w, gqa_q_norm, gqa_k_norm, w_branch, w_out, norm2_w, ffn_w_up, ffn_conv_w, ffn_conv_b, ffn_w_down) = p
    sizes = (3 * NA_WIDTH, GQA_WIDTH, GQA_KV_WIDTH, GQA_KV_WIDTH, SSD_WIDTH, SSD_XBC, 2 * SSD_HEADS, GATES)
    offs = np.concatenate([[0], np.cumsum(sizes)])
    piece = lambda k: w_in[:, offs[k]:offs[k + 1]].astype(BF16)
    na, gq, gk, gv, z, xbc, dtw, gates = (piece(k) for k in range(8))
    w_main = jnp.concatenate([gates, xbc, na, z, gq, gk, gv], axis=1)
    w_dt = jnp.pad(dtw, ((0, 0), (0, DT_PAD - 2 * SSD_HEADS)))
    pad_dt = lambda v: jnp.pad(v.reshape(1, 2 * SSD_HEADS), ((0, 0), (0, DT_PAD - 2 * SSD_HEADS)))
    row = lambda v: v.reshape(1, -1)
    wb = w_branch.astype(BF16)
    return dict(
        norm1_w=row(norm1_w), w_main=w_main, w_dt=w_dt,
        na_q_norm=row(na_q_norm), na_k_norm=row(na_k_norm), na_bias=na_bias,
        conv_w=ssd_conv_w, conv_b=row(ssd_conv_b), dt_bias=pad_dt(ssd_dt_bias), a_log=pad_dt(ssd_a_log),
        d_exp=row(jnp.repeat(ssd_d, SSD_HEAD_DIM)), ssd_norm_w=row(ssd_norm_w),
        gqa_q_norm=row(gqa_q_norm), gqa_k_norm=row(gqa_k_norm),
        w_na=wb[:NA_WIDTH], w_ssd=wb[NA_WIDTH:NA_WIDTH + SSD_WIDTH], w_gqa=wb[NA_WIDTH + SSD_WIDTH:],
        w_out=w_out.astype(BF16), norm2_w=row(norm2_w), w_up=ffn_w_up.astype(BF16),
        ffn_conv_w=ffn_conv_w, ffn_conv_b=row(ffn_conv_b), w_down=ffn_w_down.astype(BF16),
    )


def _prep_params(params):
    depth = params[0].shape[0]
    na_bias = _na_bias_tables(params[4])
    return [_prep_layer_params([p[i] for p in params], na_bias[i]) for i in range(depth)]


def _layer(x, h, lp, next_norm_w, consts, *, bsz, seq):
    t = _tiles(bsz, seq)
    cos_t, sin_t, tri, expand = consts
    proj, dt_raw = _matmul(h, lp["w_main"], lp["w_dt"], tm=t["tm_mm"], tn=MM_TN)
    o_na = _na_attention(proj, lp["na_bias"], lp["na_q_norm"], lp["na_k_norm"], bsz=bsz, seq=seq)
    qn, kn, vaug = _gqa_prep(proj, cos_t, sin_t, lp["gqa_q_norm"], lp["gqa_k_norm"], seq=seq, tm=t["tm_ew"])
    o_gqa = _gqa_attention(qn, kn, vaug, bsz=bsz, seq=seq, tq=t["tq"], tk=t["tk"])
    xs, bc, dt = _ssd_prep(proj, dt_raw, lp["conv_w"], lp["conv_b"], lp["dt_bias"], seq=seq, tm=t["tm_ew"])
    yf, yb = _ssd_scan(xs, bc, dt, lp["a_log"], tri, expand, bsz=bsz, seq=seq)
    o_ssd = _ssd_finish(yf, yb, xs, proj, lp["d_exp"], lp["ssd_norm_w"], tm=t["tm_ew"])
    x, h2 = _merge(o_na, o_ssd, o_gqa, proj, x, lp["w_na"], lp["w_ssd"], lp["w_gqa"], lp["w_out"], lp["norm2_w"],
                   tm=t["tm_merge"])
    u = _matmul(h2, lp["w_up"], tm=t["tm_mm"], tn=MM_TN)
    return _ffn_down(u, lp["ffn_conv_w"], lp["ffn_conv_b"], lp["w_down"], x, next_norm_w, seq=seq, tm=t["tm_down"])


def _trunk(x, layer_params):
    bsz, seq, d = x.shape
    assert d == D_MODEL and seq % 1024 == 0 and seq // GRID_W >= NA_KROWS
    consts = _rope_tables(seq) + _ssd_constants()
    x = x.reshape(bsz * seq, d)
    h = _rmsnorm(x, layer_params[0]["norm1_w"], tm=_tiles(bsz, seq)["tm_ew"])
    for li, lp in enumerate(layer_params):
        nxt = layer_params[li + 1]["norm1_w"] if li + 1 < len(layer_params) else None
        x, h = _layer(x, h, lp, nxt, consts, bsz=bsz, seq=seq)
    return x.reshape(bsz, seq, d)


def kernel(x_prompt, x_sample, norm1_w, w_in, na_q_norm, na_k_norm, na_rpb, ssd_conv_w, ssd_conv_b, ssd_dt_bias, ssd_a_log, ssd_d, ssd_norm_w, gqa_q_norm, gqa_k_norm, w_branch, w_out, norm2_w, ffn_w_up, ffn_conv_w, ffn_conv_b, ffn_w_down):
    params = (norm1_w, w_in, na_q_norm, na_k_norm, na_rpb, ssd_conv_w, ssd_conv_b, ssd_dt_bias, ssd_a_log,
              ssd_d, ssd_norm_w, gqa_q_norm, gqa_k_norm, w_branch, w_out, norm2_w, ffn_w_up, ffn_conv_w,
              ffn_conv_b, ffn_w_down)
    layer_params = _prep_params(params)
    return (_trunk(x_prompt, layer_params), _trunk(x_sample, layer_params))
```

```python
import functools
import math

import numpy as np
import jax
import jax.numpy as jnp
from jax import lax
from jax.experimental import pallas as pl
from jax.experimental.pallas import tpu as pltpu

F32 = jnp.float32
BF16 = jnp.bfloat16

D_MODEL = 2048
GRID_W = 64
EPS = 1e-6
NA_HEADS = 4
HEAD_DIM = 128
NA_WIDTH = NA_HEADS * HEAD_DIM
NA_WIN_H = 8
NA_WIN_W = 16
SSD_HEADS = 16
SSD_HEAD_DIM = 64
SSD_WIDTH = SSD_HEADS * SSD_HEAD_DIM
SSD_STATE = 128
SSD_GROUPS = 2
SSD_CONV = 5
SSD_CHUNK = 128
SSD_CPS = 4
SSD_BC = 2 * SSD_GROUPS * SSD_STATE
SSD_XBC = SSD_WIDTH + SSD_BC
GQA_HEADS = 4
GQA_KV_HEADS = 2
GQA_WIDTH = GQA_HEADS * HEAD_DIM
GQA_KV_WIDTH = GQA_KV_HEADS * HEAD_DIM
ROPE_THETA = 10000.0
D_FF = 5632
N_BRANCH = 3
GATES = N_BRANCH * D_MODEL
DT_PAD = 128

OFF_GATES = 0
OFF_XBC = OFF_GATES + GATES
OFF_NA = OFF_XBC + SSD_XBC
OFF_Z = OFF_NA + 3 * NA_WIDTH
OFF_GQ = OFF_Z + SSD_WIDTH
OFF_GK = OFF_GQ + GQA_WIDTH
OFF_GV = OFF_GK + GQA_KV_WIDTH
PROJ_COLS = OFF_GV + GQA_KV_WIDTH

NEG = -1e30
LOG2E = math.log2(math.e)
HALO = 8
VMEM_LIMIT = 56 * 2**20


def _cp(*sem):
    return pltpu.CompilerParams(dimension_semantics=sem, vmem_limit_bytes=VMEM_LIMIT)


def _sigmoid(x):
    return 0.5 + 0.5 * jnp.tanh(0.5 * x)


def _silu_of_half(half):
    return half + half * jnp.tanh(half)


def _silu(x):
    return _silu_of_half(0.5 * x)


NORM_ROWS = 128


def _rmsnorm_rows(x_ref, nw_ref, h_ref, n_rows):
    def body(r, c):
        rows = pl.ds(pl.multiple_of(r * NORM_ROWS, NORM_ROWS), NORM_ROWS)
        x = x_ref[rows, :]
        ms = jnp.mean(x * x, axis=-1, keepdims=True)
        h_ref[rows, :] = (x * lax.rsqrt(ms + EPS) * nw_ref[...]).astype(BF16)
        return c

    lax.fori_loop(0, n_rows // NORM_ROWS, body, 0)


def _rmsnorm_kernel(x_ref, nw_ref, h_ref, *, tm):
    _rmsnorm_rows(x_ref, nw_ref, h_ref, tm)


def _rmsnorm(x, nw, *, tm):
    m, d = x.shape
    return pl.pallas_call(
        functools.partial(_rmsnorm_kernel, tm=tm),
        grid=(m // tm,),
        in_specs=[pl.BlockSpec((tm, d), lambda i: (i, 0)), pl.BlockSpec((1, d), lambda i: (0, 0))],
        out_specs=pl.BlockSpec((tm, d), lambda i: (i, 0)),
        out_shape=jax.ShapeDtypeStruct((m, d), BF16),
        compiler_params=_cp("parallel"),
        name="rmsnorm",
    )(x, nw)


def _mm_kernel(*refs, with_dt):
    if with_dt:
        h_ref, w_ref, wdt_ref, o_ref, odt_ref = refs

        @pl.when(pl.program_id(1) == 0)
        def _():
            odt_ref[...] = jnp.dot(h_ref[...], wdt_ref[...], preferred_element_type=F32)
    else:
        h_ref, w_ref, o_ref = refs
    o_ref[...] = jnp.dot(h_ref[...], w_ref[...], preferred_element_type=F32).astype(o_ref.dtype)


def _matmul(h, w, wdt=None, *, tm, tn):
    m, d = h.shape
    n = w.shape[1]
    with_dt = wdt is not None
    in_specs = [
        pl.BlockSpec((tm, d), lambda i, j: (i, 0)),
        pl.BlockSpec((d, tn), lambda i, j: (0, j)),
    ]
    out_shape = [jax.ShapeDtypeStruct((m, n), BF16)]
    out_specs = [pl.BlockSpec((tm, tn), lambda i, j: (i, j))]
    args = [h, w]
    if with_dt:
        in_specs.append(pl.BlockSpec((d, DT_PAD), lambda i, j: (0, 0)))
        out_shape.append(jax.ShapeDtypeStruct((m, DT_PAD), F32))
        out_specs.append(pl.BlockSpec((tm, DT_PAD), lambda i, j: (i, 0)))
        args.append(wdt)
    res = pl.pallas_call(
        functools.partial(_mm_kernel, with_dt=with_dt),
        grid=(m // tm, n // tn),
        in_specs=in_specs,
        out_specs=out_specs,
        out_shape=out_shape,
        compiler_params=_cp("parallel", "arbitrary"),
        name="matmul_dt" if with_dt else "matmul",
    )(*args)
    return res if with_dt else res[0]


NA_QROWS = 8
NA_KROWS = 16
NA_TQ = NA_QROWS * GRID_W
NA_TK = NA_KROWS * GRID_W
NA_KPIECE = 256
NA_NPIECE = NA_TK // NA_KPIECE


def _na_bias_tables(rpb):
    rows = 4 * NA_QROWS
    nb = rows // NA_QROWS
    depth = rpb.shape[0]
    c = np.arange(GRID_W)
    col_start = np.clip(c - NA_WIN_W // 2, 0, GRID_W - NA_WIN_W)
    in_win = (c[None, :] >= col_start[:, None]) & (c[None, :] < col_start[:, None] + NA_WIN_W)
    dc = np.clip(c[None, :] - c[:, None], -(NA_WIN_W - 1), NA_WIN_W - 1) + NA_WIN_W - 1
    onehot_c = (dc[:, :, None] == np.arange(2 * NA_WIN_W - 1)).astype(np.float32)
    by_col = jnp.einsum("lhab,qcb->lhaqc", rpb.astype(F32), jnp.asarray(onehot_c), precision=lax.Precision.HIGHEST)
    by_col = by_col * LOG2E
    tables = []
    for b in (0, 1, nb - 1):
        r = NA_QROWS * b + np.arange(NA_QROWS)
        ks = int(np.clip(NA_QROWS * b - NA_WIN_H // 2, 0, rows - NA_KROWS))
        krow = ks + np.arange(NA_KROWS)
        rs = np.clip(r - NA_WIN_H // 2, 0, rows - NA_WIN_H)
        valid = (krow[None, :] >= rs[:, None]) & (krow[None, :] < rs[:, None] + NA_WIN_H)
        dr = np.clip(krow[None, :] - r[:, None], -(NA_WIN_H - 1), NA_WIN_H - 1) + NA_WIN_H - 1
        bias = jnp.take(by_col, jnp.asarray(dr.reshape(-1)), axis=2)
        bias = bias.reshape(depth, NA_HEADS, NA_QROWS, NA_KROWS, GRID_W, GRID_W).transpose(0, 1, 2, 4, 3, 5)
        mask = valid[:, None, :, None] & in_win[None, :, None, :]
        bias = jnp.where(jnp.asarray(mask)[None, None], bias, NEG)
        tables.append(bias.reshape(depth, NA_HEADS, NA_TQ, NA_TK))
    return jnp.stack(tables, axis=1)


def _head_rmsnorm(x, w):
    ms = jnp.mean(x * x, axis=-1, keepdims=True)
    return x * lax.rsqrt(ms + EPS) * w


NA_SUB = 128
NA_HPS = 4


def _na_kernel(q_ref, k0, k1, k2, k3, v0, v1, v2, v3, bias_ref, qw_ref, kw_ref, o_ref, kn_scr, va_scr, s_scr, *, nb):
    i = pl.program_id(2)
    kind = jnp.where(i == 0, 0, jnp.where(i == nb - 1, 2, 1))
    scale = HEAD_DIM ** -0.5 * LOG2E
    ones = jnp.ones((NA_KPIECE, HEAD_DIM), BF16)
    for hh in range(NA_HPS):
        hc = pl.ds(hh * HEAD_DIM, HEAD_DIM)
        for t, (k_ref, v_ref) in enumerate(zip((k0, k1, k2, k3), (v0, v1, v2, v3))):
            piece = pl.ds(t * NA_KPIECE, NA_KPIECE)
            kn_scr[hh, piece, :] = _head_rmsnorm(k_ref[:, hc].astype(F32), kw_ref[...]).astype(BF16)
            va_scr[hh, piece, :HEAD_DIM] = v_ref[:, hc]
            va_scr[hh, piece, HEAD_DIM:] = ones
    nlane = NA_TK // HEAD_DIM
    nsub = NA_TQ // NA_SUB

    def scores(u):
        hh, qs = divmod(u, nsub)
        rows = pl.ds(qs * NA_SUB, NA_SUB)
        qn = (_head_rmsnorm(q_ref[rows, pl.ds(hh * HEAD_DIM, HEAD_DIM)].astype(F32), qw_ref[...]) * scale).astype(BF16)
        s = lax.dot_general(qn, kn_scr[hh], (((1,), (1,)), ((), ())), preferred_element_type=F32)
        return s + bias_ref[kind, hh, rows, :]

    def finish(u, slot):
        hh, qs = divmod(u, nsub)
        blocks = [s_scr[slot, :, j * HEAD_DIM:(j + 1) * HEAD_DIM] for j in range(nlane)]
        part = blocks[0]
        for blk in blocks[1:]:
            part = jnp.maximum(part, blk)
        m = jnp.broadcast_to(jnp.max(part, axis=-1, keepdims=True), part.shape)
        p = jnp.concatenate([jnp.exp2(blk - m).astype(BF16) for blk in blocks], axis=1)
        pv = jnp.dot(p, va_scr[hh], preferred_element_type=F32)
        o_ref[pl.ds(qs * NA_SUB, NA_SUB), pl.ds(hh * HEAD_DIM, HEAD_DIM)] = (
            pv[:, :HEAD_DIM] / pv[:, HEAD_DIM:]).astype(o_ref.dtype)

    units = NA_HPS * nsub
    s_scr[0] = scores(0)
    for u in range(units):
        if u + 1 < units:
            s_scr[(u + 1) % 2] = scores(u + 1)
        finish(u, u % 2)


def _na_attention(proj, bias, qw, kw, *, bsz, seq):
    m = proj.shape[0]
    rows = seq // GRID_W
    nb = rows // NA_QROWS
    npieces = seq // NA_KPIECE
    width = NA_HPS * HEAD_DIM
    q_col = OFF_NA // width
    k_col = (OFF_NA + NA_WIDTH) // width
    v_col = (OFF_NA + 2 * NA_WIDTH) // width

    def kbase(i):
        return jnp.clip(2 * i - 1, 0, npieces - NA_NPIECE)

    def kv_spec(col, t):
        return pl.BlockSpec((NA_KPIECE, width), lambda h, b, i: (b * npieces + kbase(i) + t, col + h))

    in_specs = [pl.BlockSpec((NA_TQ, width), lambda h, b, i: (b * nb + i, q_col + h))]
    in_specs += [kv_spec(k_col, t) for t in range(NA_NPIECE)]
    in_specs += [kv_spec(v_col, t) for t in range(NA_NPIECE)]
    in_specs += [
        pl.BlockSpec((3, NA_HPS, NA_TQ, NA_TK), lambda h, b, i: (0, h, 0, 0), pipeline_mode=pl.Buffered(1)),
        pl.BlockSpec((1, HEAD_DIM), lambda h, b, i: (0, 0)),
        pl.BlockSpec((1, HEAD_DIM), lambda h, b, i: (0, 0)),
    ]
    return pl.pallas_call(
        functools.partial(_na_kernel, nb=nb),
        grid=(NA_HEADS // NA_HPS, bsz, nb),
        in_specs=in_specs,
        out_specs=pl.BlockSpec((NA_TQ, width), lambda h, b, i: (b * nb + i, h)),
        out_shape=jax.ShapeDtypeStruct((m, NA_WIDTH), BF16),
        scratch_shapes=[
            pltpu.VMEM((NA_HPS, NA_TK, HEAD_DIM), BF16),
            pltpu.VMEM((NA_HPS, NA_TK, 2 * HEAD_DIM), BF16),
            pltpu.VMEM((2, NA_SUB, NA_TK), F32),
        ],
        compiler_params=_cp("parallel", "parallel", "parallel"),
        name="na_attention",
    )(proj, *([proj] * (2 * NA_NPIECE)), bias, qw, kw)


def _rope_tables(seq):
    t = np.arange(seq)
    pos = np.stack([t // GRID_W, t % GRID_W], axis=-1).astype(np.float32)
    n_freq = HEAD_DIM // 4
    inv_freq = jnp.asarray(ROPE_THETA, F32) ** (-jnp.arange(n_freq, dtype=F32) / n_freq)
    ang = jnp.asarray(pos)[:, :, None] * inv_freq
    cos = jnp.cos(ang)
    sin = jnp.sin(ang)
    cos_t = jnp.concatenate([cos[:, 0], cos[:, 0], cos[:, 1], cos[:, 1]], axis=-1)
    sin_t = jnp.concatenate([-sin[:, 0], sin[:, 0], -sin[:, 1], sin[:, 1]], axis=-1)
    return cos_t, sin_t


def _rope(x, cos_t, sin_t, first_half):
    swapped = jnp.where(first_half, pltpu.roll(x, HEAD_DIM - 32, axis=1), pltpu.roll(x, 32, axis=1))
    return x * cos_t + swapped * sin_t


def _gqa_prep_kernel(q_ref, k_ref, v_ref, cos_ref, sin_ref, qw_ref, kw_ref, qo_ref, ko_ref, vo_ref):
    cos_t = cos_ref[...]
    sin_t = sin_ref[...]
    lane = lax.broadcasted_iota(jnp.int32, cos_t.shape, 1)
    first_half = (lane % 64) < 32
    scale = HEAD_DIM ** -0.5 * LOG2E
    for h in range(GQA_HEADS):
        cols = pl.ds(h * HEAD_DIM, HEAD_DIM)
        x = _head_rmsnorm(q_ref[:, cols].astype(F32), qw_ref[...])
        qo_ref[:, cols] = (_rope(x, cos_t, sin_t, first_half) * scale).astype(BF16)
    ones = jnp.ones(cos_t.shape, BF16)
    for h in range(GQA_KV_HEADS):
        cols = pl.ds(h * HEAD_DIM, HEAD_DIM)
        x = _head_rmsnorm(k_ref[:, cols].astype(F32), kw_ref[...])
        ko_ref[:, cols] = _rope(x, cos_t, sin_t, first_half).astype(BF16)
        vo_ref[:, pl.ds(2 * h * HEAD_DIM, HEAD_DIM)] = v_ref[:, cols]
        vo_ref[:, pl.ds((2 * h + 1) * HEAD_DIM, HEAD_DIM)] = ones


def _gqa_prep(proj, cos_t, sin_t, qw, kw, *, seq, tm):
    m = proj.shape[0]
    tps = seq // tm
    return pl.pallas_call(
        _gqa_prep_kernel,
        grid=(m // tm,),
        in_specs=[
            pl.BlockSpec((tm, GQA_WIDTH), lambda i: (i, OFF_GQ // GQA_WIDTH)),
            pl.BlockSpec((tm, GQA_KV_WIDTH), lambda i: (i, OFF_GK // GQA_KV_WIDTH)),
            pl.BlockSpec((tm, GQA_KV_WIDTH), lambda i: (i, OFF_GV // GQA_KV_WIDTH)),
            pl.BlockSpec((tm, HEAD_DIM), lambda i: (i % tps, 0)),
            pl.BlockSpec((tm, HEAD_DIM), lambda i: (i % tps, 0)),
            pl.BlockSpec((1, HEAD_DIM), lambda i: (0, 0)),
            pl.BlockSpec((1, HEAD_DIM), lambda i: (0, 0)),
        ],
        out_specs=[
            pl.BlockSpec((tm, GQA_WIDTH), lambda i: (i, 0)),
            pl.BlockSpec((tm, GQA_KV_WIDTH), lambda i: (i, 0)),
            pl.BlockSpec((tm, 2 * GQA_KV_WIDTH), lambda i: (i, 0)),
        ],
        out_shape=[
            jax.ShapeDtypeStruct((m, GQA_WIDTH), BF16),
            jax.ShapeDtypeStruct((m, GQA_KV_WIDTH), BF16),
            jax.ShapeDtypeStruct((m, 2 * GQA_KV_WIDTH), BF16),
        ],
        compiler_params=_cp("parallel"),
        name="gqa_prep",
    )(proj, proj, proj, cos_t, sin_t, qw, kw)


def _flash_kernel(q_ref, k_ref, v_ref, o_ref, s_scr, m_scr, acc_scr, *, tq, tk, seq):
    q2 = jnp.concatenate([q_ref[:, :HEAD_DIM], q_ref[:, HEAD_DIM:]], axis=0)
    m_scr[...] = jnp.full(m_scr.shape, NEG, F32)
    acc_scr[...] = jnp.zeros(acc_scr.shape, F32)
    nblk = tk // HEAD_DIM
    n = seq // tk

    def chunk(kc):
        return pl.ds(kc * tk, tk)

    def scores(kc):
        return lax.dot_general(q2, k_ref[chunk(kc), :], (((1,), (1,)), ((), ())), preferred_element_type=F32)

    def softmax_pv(kc, slot):
        blocks = [s_scr[slot, :, j * HEAD_DIM:(j + 1) * HEAD_DIM] for j in range(nblk)]
        part = blocks[0]
        for blk in blocks[1:]:
            part = jnp.maximum(part, blk)
        m_prev = m_scr[...]
        m_new = jnp.maximum(m_prev, jnp.max(part, axis=-1, keepdims=True))
        alpha = jnp.exp2(m_prev - m_new)
        p = jnp.concatenate([jnp.exp2(blk - m_new).astype(BF16) for blk in blocks], axis=1)
        pv = jnp.dot(p, v_ref[chunk(kc), :], preferred_element_type=F32)
        acc_scr[...] = jnp.concatenate([alpha, alpha], axis=1) * acc_scr[...] + pv
        m_scr[...] = m_new

    s_scr[0] = scores(0)

    for kc in range(n):
        if kc + 1 < n:
            s_scr[(kc + 1) % 2] = scores(kc + 1)
        softmax_pv(kc, kc % 2)
    o = acc_scr[:, :HEAD_DIM] / acc_scr[:, HEAD_DIM:]
    o_ref[:, :HEAD_DIM] = o[:tq].astype(o_ref.dtype)
    o_ref[:, HEAD_DIM:] = o[tq:].astype(o_ref.dtype)


def _gqa_attention(qn, kn, vaug, *, bsz, seq, tq, tk):
    m = qn.shape[0]
    nq = seq // tq
    rep = GQA_HEADS // GQA_KV_HEADS
    return pl.pallas_call(
        functools.partial(_flash_kernel, tq=tq, tk=tk, seq=seq),
        grid=(bsz, GQA_KV_HEADS, nq),
        in_specs=[
            pl.BlockSpec((tq, rep * HEAD_DIM), lambda b, g, i: (b * nq + i, g)),
            pl.BlockSpec((seq, HEAD_DIM), lambda b, g, i: (b, g)),
            pl.BlockSpec((seq, 2 * HEAD_DIM), lambda b, g, i: (b, g)),
        ],
        out_specs=pl.BlockSpec((tq, rep * HEAD_DIM), lambda b, g, i: (b * nq + i, g)),
        out_shape=jax.ShapeDtypeStruct((m, GQA_WIDTH), BF16),
        scratch_shapes=[
            pltpu.VMEM((2, rep * tq, tk), F32),
            pltpu.VMEM((rep * tq, HEAD_DIM), F32),
            pltpu.VMEM((rep * tq, 2 * HEAD_DIM), F32),
        ],
        compiler_params=_cp("parallel", "parallel", "parallel"),
        name="gqa_flash",
    )(qn, kn, vaug)


def _with_halo(x, prev, nxt):
    return jnp.concatenate([x, nxt, prev], axis=0)


def _shift_rows(xcat, s, n):
    return pltpu.roll(xcat, s % xcat.shape[0], axis=0)[:n]


def _dwconv(x, prev, nxt, w, b, width):
    n = x.shape[0]
    xcat = _with_halo(x, prev, nxt)
    pad = width // 2
    out = b + x * w[pad:pad + 1, :]
    for i in range(width):
        if i != pad:
            out = out + _shift_rows(xcat, pad - i, n) * w[i:i + 1, :]
    return out


def _halo_specs(tm, width, col_block, m):
    nblk = m // HALO
    per = tm // HALO
    prev = pl.BlockSpec((HALO, width), lambda i, *_: (jnp.maximum(i * per - 1, 0), col_block))
    nxt = pl.BlockSpec((HALO, width), lambda i, *_: (jnp.minimum((i + 1) * per, nblk - 1), col_block))
    return prev, nxt


def _ssd_prep_kernel(x_ref, prev_ref, next_ref, dtr_ref, cw_ref, cb_ref, dtb_ref, xs_ref, bc_ref, dt_ref, *, tps):
    i = pl.program_id(0)
    first = (i % tps) == 0
    last = (i % tps) == tps - 1
    v = dtr_ref[...] + dtb_ref[...]
    dt_ref[...] = jnp.maximum(v, 0.0) + jnp.log(1.0 + jnp.exp(-jnp.abs(v)))
    x = x_ref[...].astype(F32)
    prev = jnp.where(first, 0.0, prev_ref[...].astype(F32))
    nxt = jnp.where(last, 0.0, next_ref[...].astype(F32))
    y = _silu_of_half(_dwconv(x, prev, nxt, 0.5 * cw_ref[...], 0.5 * cb_ref[...], SSD_CONV))
    xs_ref[...] = y[:, :SSD_WIDTH].astype(BF16)
    bc_ref[...] = y[:, SSD_WIDTH:].astype(BF16)


def _ssd_prep(proj, dt_raw, conv_w, conv_b, dt_bias, *, seq, tm):
    m = proj.shape[0]
    cb = OFF_XBC // SSD_XBC
    prev, nxt = _halo_specs(tm, SSD_XBC, cb, m)
    return pl.pallas_call(
        functools.partial(_ssd_prep_kernel, tps=seq // tm),
        grid=(m // tm,),
        in_specs=[
            pl.BlockSpec((tm, SSD_XBC), lambda i: (i, cb)),
            prev,
            nxt,
            pl.BlockSpec((tm, DT_PAD), lambda i: (i, 0)),
            pl.BlockSpec((SSD_CONV, SSD_XBC), lambda i: (0, 0)),
            pl.BlockSpec((1, SSD_XBC), lambda i: (0, 0)),
            pl.BlockSpec((1, DT_PAD), lambda i: (0, 0)),
        ],
        out_specs=[
            pl.BlockSpec((tm, SSD_WIDTH), lambda i: (i, 0)),
            pl.BlockSpec((tm, SSD_BC), lambda i: (i, 0)),
            pl.BlockSpec((tm, DT_PAD), lambda i: (i, 0)),
        ],
        out_shape=[
            jax.ShapeDtypeStruct((m, SSD_WIDTH), BF16),
            jax.ShapeDtypeStruct((m, SSD_BC), BF16),
            jax.ShapeDtypeStruct((m, DT_PAD), F32),
        ],
        compiler_params=_cp("parallel"),
        name="ssd_prep",
    )(proj, proj, proj, dt_raw, conv_w, conv_b, dt_bias)


def _ssd_constants():
    q = SSD_CHUNK
    li = np.arange(q)
    tri = np.stack([li[:, None] >= li[None, :], li[:, None] <= li[None, :]]).astype(np.float32)
    hp = np.arange(SSD_WIDTH) // SSD_HEAD_DIM
    expand = np.zeros((2, 2 * DT_PAD, SSD_WIDTH), np.float32)
    for d in range(2):
        expand[d, d * SSD_HEADS + hp, np.arange(SSD_WIDTH)] = 1.0
        expand[d, DT_PAD + d * SSD_HEADS + hp, np.arange(SSD_WIDTH)] = 1.0
    return jnp.asarray(tri, F32), jnp.asarray(expand, BF16)


def _expand_heads(v, e):
    hi = v.astype(BF16)
    lo = (v - hi.astype(F32)).astype(BF16)
    return jnp.dot(jnp.concatenate([hi, lo], axis=1), e, preferred_element_type=F32)


def _ssd_scan_kernel(xs_f, bc_f, dt_f, xs_b, bc_b, dt_b, alog_ref, tri_ref, exp_ref, yf_ref, yb_ref, h_scr):
    @pl.when(pl.program_id(1) == 0)
    def _():
        h_scr[...] = jnp.zeros(h_scr.shape, F32)

    q = SSD_CHUNK
    gw = SSD_WIDTH // SSD_GROUPS
    a_row = -jnp.exp(alog_ref[...])
    li = lax.broadcasted_iota(jnp.int32, (q, q), 0)
    si = lax.broadcasted_iota(jnp.int32, (q, q), 1)
    lane = lax.broadcasted_iota(jnp.int32, (q, HEAD_DIM), 1)
    left = lane < SSD_HEAD_DIM
    dirs = ((xs_f, bc_f, dt_f, yf_ref, q - 1), (xs_b, bc_b, dt_b, yb_ref, 0))
    for cc, (d, (xs_ref, bc_ref, dt_ref, y_ref, last)) in [(cc, dd) for cc in range(SSD_CPS) for dd in enumerate(dirs)]:
        rows = pl.ds((cc if d == 0 else SSD_CPS - 1 - cc) * q, q)
        keep = (li >= si) if d == 0 else (li <= si)
        e = exp_ref[d]
        dt = dt_ref[rows, :]
        da = dt * a_row
        cum = jnp.dot(tri_ref[d], da, precision=lax.Precision.HIGHEST, preferred_element_type=F32)
        cum_t = cum.T
        tot = cum[last:last + 1, :]
        stack_e = _expand_heads(jnp.concatenate([dt, jnp.exp(cum), jnp.exp(tot - cum)], axis=0), e)
        dt_e, ec_e, w_e = stack_e[:q], stack_e[q:2 * q], stack_e[2 * q:]
        dec_e = ec_e[last:last + 1, :]
        xd = xs_ref[rows, :].astype(F32) * dt_e
        xdb = xd.astype(BF16)
        xw = (xd * w_e).astype(BF16)
        for g in range(SSD_GROUPS):
            gl = slice(g * gw, (g + 1) * gw)
            bg = bc_ref[rows, g * SSD_STATE:(g + 1) * SSD_STATE]
            cg = bc_ref[rows, (SSD_GROUPS + g) * SSD_STATE:(SSD_GROUPS + g + 1) * SSD_STATE]
            cb = lax.dot_general(cg, bg, (((1,), (1,)), ((), ())), preferred_element_type=F32)
            bg_t = bg.astype(F32).T.astype(BF16)
            h_prev = h_scr[d, g]
            y_off = jnp.dot(cg, h_prev.astype(BF16), preferred_element_type=F32) * ec_e[:, gl]
            states_t = jnp.dot(bg_t, xw[:, gl], preferred_element_type=F32)
            h_scr[d, g] = h_prev * dec_e[:, gl] + states_t
            for j in range(gw // HEAD_DIM):
                mats = []
                for hh in range(2):
                    col = d * SSD_HEADS + g * (SSD_HEADS // SSD_GROUPS) + 2 * j + hh
                    seg = cum[:, col:col + 1] - cum_t[col:col + 1, :]
                    lm = jnp.exp(jnp.where(keep, seg, NEG))
                    mats.append((cb * lm).astype(BF16))
                lo = g * gw + j * HEAD_DIM
                slab = xdb[:, lo:lo + HEAD_DIM]
                zero = jnp.zeros_like(slab)
                wbd = jnp.concatenate([jnp.where(left, slab, zero), jnp.where(left, zero, slab)], axis=0)
                y_diag = jnp.dot(jnp.concatenate(mats, axis=1), wbd, preferred_element_type=F32)
                y_ref[rows, lo:lo + HEAD_DIM] = (y_diag + y_off[:, j * HEAD_DIM:(j + 1) * HEAD_DIM]).astype(y_ref.dtype)


def _ssd_scan(xs, bc, dt, a_log, tri, expand, *, bsz, seq):
    m = xs.shape[0]
    nc = seq // (SSD_CPS * SSD_CHUNK)
    blk = SSD_CPS * SSD_CHUNK
    fwd = lambda b, c: (b * nc + c, 0)
    bwd = lambda b, c: (b * nc + nc - 1 - c, 0)
    return pl.pallas_call(
        _ssd_scan_kernel,
        grid=(bsz, nc),
        in_specs=[
            pl.BlockSpec((blk,SSD_WIDTH), fwd),
            pl.BlockSpec((blk,SSD_BC), fwd),
            pl.BlockSpec((blk,DT_PAD), fwd),
            pl.BlockSpec((blk,SSD_WIDTH), bwd),
            pl.BlockSpec((blk,SSD_BC), bwd),
            pl.BlockSpec((blk,DT_PAD), bwd),
            pl.BlockSpec((1, DT_PAD), lambda b, c: (0, 0)),
            pl.BlockSpec((2, SSD_CHUNK, SSD_CHUNK), lambda b, c: (0, 0, 0)),
            pl.BlockSpec((2, 2 * DT_PAD, SSD_WIDTH), lambda b, c: (0, 0, 0)),
        ],
        out_specs=[
            pl.BlockSpec((blk,SSD_WIDTH), fwd),
            pl.BlockSpec((blk,SSD_WIDTH), bwd),
        ],
        out_shape=[
            jax.ShapeDtypeStruct((m, SSD_WIDTH), BF16),
            jax.ShapeDtypeStruct((m, SSD_WIDTH), BF16),
        ],
        scratch_shapes=[pltpu.VMEM((2, SSD_GROUPS, SSD_STATE, SSD_WIDTH // SSD_GROUPS), F32)],
        compiler_params=_cp("parallel", "arbitrary"),
        name="ssd_scan",
    )(xs, bc, dt, xs, bc, dt, a_log, tri, expand)


def _ssd_finish_kernel(yf_ref, yb_ref, xs_ref, z_ref, d_ref, nw_ref, o_ref):
    y = yf_ref[...].astype(F32) + yb_ref[...].astype(F32) + xs_ref[...].astype(F32) * d_ref[...]
    y = y * _silu(z_ref[...].astype(F32))
    gw = SSD_WIDTH // SSD_GROUPS
    for g in range(SSD_GROUPS):
        yg = y[:, g * gw:(g + 1) * gw]
        ms = jnp.mean(yg * yg, axis=-1, keepdims=True)
        o_ref[:, g * gw:(g + 1) * gw] = (yg * lax.rsqrt(ms + EPS) * nw_ref[:, g * gw:(g + 1) * gw]).astype(o_ref.dtype)


def _ssd_finish(yf, yb, xs, proj, d_exp, norm_w, *, tm):
    m = yf.shape[0]
    row = lambda i: (i, 0)
    const = lambda i: (0, 0)
    return pl.pallas_call(
        _ssd_finish_kernel,
        grid=(m // tm,),
        in_specs=[
            pl.BlockSpec((tm, SSD_WIDTH), row),
            pl.BlockSpec((tm, SSD_WIDTH), row),
            pl.BlockSpec((tm, SSD_WIDTH), row),
            pl.BlockSpec((tm, SSD_WIDTH), lambda i: (i, OFF_Z // SSD_WIDTH)),
            pl.BlockSpec((1, SSD_WIDTH), const),
            pl.BlockSpec((1, SSD_WIDTH), const),
        ],
        out_specs=pl.BlockSpec((tm, SSD_WIDTH), row),
        out_shape=jax.ShapeDtypeStruct((m, SSD_WIDTH), BF16),
        compiler_params=_cp("parallel"),
        name="ssd_finish",
    )(yf, yb, xs, proj, d_exp, norm_w)


def _merge_kernel(ona_ref, ossd_ref, ogqa_ref, g_ref, x_ref, wna_ref, wssd_ref, wgqa_ref, wout_ref, nw_ref,
                  o_ref, h_ref, mg_scr, *, tm):
    nc = 512
    branches = ((ona_ref, wna_ref), (ossd_ref, wssd_ref), (ogqa_ref, wgqa_ref))
    for c in range(D_MODEL // nc):
        cols = pl.ds(c * nc, nc)
        acc = None
        for b, (o_b, w_b) in enumerate(branches):
            p = jnp.dot(o_b[...], w_b[:, cols], preferred_element_type=F32)
            gate = _sigmoid(g_ref[:, pl.ds(b * D_MODEL + c * nc, nc)].astype(F32))
            acc = gate * p if acc is None else acc + gate * p
        mg_scr[:, cols] = acc.astype(BF16)
    for c in range(D_MODEL // nc):
        cols = pl.ds(c * nc, nc)
        o_ref[:, cols] = x_ref[:, cols] + jnp.dot(mg_scr[...], wout_ref[:, cols], preferred_element_type=F32)
    _rmsnorm_rows(o_ref, nw_ref, h_ref, tm)


def _merge(o_na, o_ssd, o_gqa, proj, x, w_na, w_ssd, w_gqa, w_out, norm_w, *, tm):
    m = x.shape[0]
    row = lambda i: (i, 0)
    const = lambda i: (0, 0)
    one = pl.Buffered(1)
    return pl.pallas_call(
        functools.partial(_merge_kernel, tm=tm),
        grid=(m // tm,),
        in_specs=[
            pl.BlockSpec((tm, NA_WIDTH), row),
            pl.BlockSpec((tm, SSD_WIDTH), row),
            pl.BlockSpec((tm, GQA_WIDTH), row),
            pl.BlockSpec((tm, GATES), row),
            pl.BlockSpec((tm, D_MODEL), row),
            pl.BlockSpec((NA_WIDTH, D_MODEL), const, pipeline_mode=one),
            pl.BlockSpec((SSD_WIDTH, D_MODEL), const, pipeline_mode=one),
            pl.BlockSpec((GQA_WIDTH, D_MODEL), const, pipeline_mode=one),
            pl.BlockSpec((D_MODEL, D_MODEL), const, pipeline_mode=one),
            pl.BlockSpec((1, D_MODEL), const),
        ],
        out_specs=[pl.BlockSpec((tm, D_MODEL), row), pl.BlockSpec((tm, D_MODEL), row)],
        out_shape=[jax.ShapeDtypeStruct((m, D_MODEL), F32), jax.ShapeDtypeStruct((m, D_MODEL), BF16)],
        scratch_shapes=[pltpu.VMEM((tm, D_MODEL), BF16)],
        compiler_params=_cp("parallel"),
        name="merge",
    )(o_na, o_ssd, o_gqa, proj, x, w_na, w_ssd, w_gqa, w_out, norm_w)


FFN_CHUNK = 256


def _ffn_down_kernel(*refs, tps, tm, with_norm):
    if with_norm:
        ug_ref, prev_ref, next_ref, uv_ref, cw_ref, cb_ref, wd_ref, x_ref, nw_ref, o_ref, h_ref, hid_scr = refs
    else:
        ug_ref, prev_ref, next_ref, uv_ref, cw_ref, cb_ref, wd_ref, x_ref, o_ref, hid_scr = refs
    i = pl.program_id(0)
    first = (i % tps) == 0
    last = (i % tps) == tps - 1

    def gated(c):
        cols = pl.ds(c * FFN_CHUNK, FFN_CHUNK)
        g = ug_ref[:, cols].astype(F32)
        prev = jnp.where(first, 0.0, prev_ref[:, cols].astype(F32))
        nxt = jnp.where(last, 0.0, next_ref[:, cols].astype(F32))
        half = _dwconv(g, prev, nxt, 0.5 * cw_ref[:, cols], 0.5 * cb_ref[:, cols], 3)
        hid_scr[:, cols] = _silu_of_half(half).astype(BF16) * uv_ref[:, cols]

    nchunk = D_FF // FFN_CHUNK
    gated(0)
    for c in range(nchunk):
        rows = pl.ds(c * FFN_CHUNK, FFN_CHUNK)
        part = jnp.dot(hid_scr[:, rows], wd_ref[rows, :], preferred_element_type=F32)
        if c + 1 < nchunk:
            gated(c + 1)
        if c == 0:
            o_ref[...] = x_ref[...] + part
        else:
            o_ref[...] += part
    if with_norm:
        _rmsnorm_rows(o_ref, nw_ref, h_ref, tm)


def _ffn_down(u, conv_w, conv_b, w_down, x, next_norm_w=None, *, seq, tm):
    m = x.shape[0]
    nblk = m // HALO
    per = tm // HALO
    const = lambda i: (0, 0)
    row = lambda i: (i, 0)
    with_norm = next_norm_w is not None
    in_specs = [
        pl.BlockSpec((tm, D_FF), lambda i: (i, 0)),
        pl.BlockSpec((HALO, D_FF), lambda i: (jnp.maximum(i * per - 1, 0), 0)),
        pl.BlockSpec((HALO, D_FF), lambda i: (jnp.minimum((i + 1) * per, nblk - 1), 0)),
        pl.BlockSpec((tm, D_FF), lambda i: (i, 1)),
        pl.BlockSpec((3, D_FF), const),
        pl.BlockSpec((1, D_FF), const),
        pl.BlockSpec((D_FF, D_MODEL), const, pipeline_mode=pl.Buffered(1)),
        pl.BlockSpec((tm, D_MODEL), row),
    ]
    out_specs = [pl.BlockSpec((tm, D_MODEL), row)]
    out_shape = [jax.ShapeDtypeStruct((m, D_MODEL), F32)]
    args = [u, u, u, u, conv_w, conv_b, w_down, x]
    if with_norm:
        in_specs.append(pl.BlockSpec((1, D_MODEL), const))
        out_specs.append(pl.BlockSpec((tm, D_MODEL), row))
        out_shape.append(jax.ShapeDtypeStruct((m, D_MODEL), BF16))
        args.append(next_norm_w)
    res = pl.pallas_call(
        functools.partial(_ffn_down_kernel, tps=seq // tm, tm=tm, with_norm=with_norm),
        grid=(m // tm,),
        in_specs=in_specs,
        out_specs=out_specs,
        out_shape=out_shape,
        scratch_shapes=[pltpu.VMEM((tm, D_FF), BF16)],
        compiler_params=_cp("parallel"),
        name="ffn_down",
    )(*args)
    return res if with_norm else (res[0], None)


MM_TN = 1024


def _tiles(bsz, seq):
    m = bsz * seq
    return dict(tm_mm=2048 if m % 2048 == 0 else 1024, tm_ew=min(seq, 1024), tm_merge=min(seq, 256),
                tm_down=min(seq, 256), tq=min(seq, 512), tk=min(seq, 1024))


def _prep_layer_params(p, na_bias):
    (norm1_w, w_in, na_q_norm, na_k_norm, _, ssd_conv_w, ssd_conv_b, ssd_dt_bias, ssd_a_log, ssd_d,
     ssd_norm_w, gqa_q_norm, gqa_k_norm, w_branch, w_out, norm2_w, ffn_w_up, ffn_conv_w, ffn_conv_b, ffn_w_down) = p
    sizes = (3 * NA_WIDTH, GQA_WIDTH, GQA_KV_WIDTH, GQA_KV_WIDTH, SSD_WIDTH, SSD_XBC, 2 * SSD_HEADS, GATES)
    offs = np.concatenate([[0], np.cumsum(sizes)])
    piece = lambda k: w_in[:, offs[k]:offs[k + 1]].astype(BF16)
    na, gq, gk, gv, z, xbc, dtw, gates = (piece(k) for k in range(8))
    w_main = jnp.concatenate([gates, xbc, na, z, gq, gk, gv], axis=1)
    w_dt = jnp.pad(dtw, ((0, 0), (0, DT_PAD - 2 * SSD_HEADS)))
    pad_dt = lambda v: jnp.pad(v.reshape(1, 2 * SSD_HEADS), ((0, 0), (0, DT_PAD - 2 * SSD_HEADS)))
    row = lambda v: v.reshape(1, -1)
    wb = w_branch.astype(BF16)
    return dict(
        norm1_w=row(norm1_w), w_main=w_main, w_dt=w_dt,
        na_q_norm=row(na_q_norm), na_k_norm=row(na_k_norm), na_bias=na_bias,
        conv_w=ssd_conv_w, conv_b=row(ssd_conv_b), dt_bias=pad_dt(ssd_dt_bias), a_log=pad_dt(ssd_a_log),
        d_exp=row(jnp.repeat(ssd_d, SSD_HEAD_DIM)), ssd_norm_w=row(ssd_norm_w),
        gqa_q_norm=row(gqa_q_norm), gqa_k_norm=row(gqa_k_norm),
        w_na=wb[:NA_WIDTH], w_ssd=wb[NA_WIDTH:NA_WIDTH + SSD_WIDTH], w_gqa=wb[NA_WIDTH + SSD_WIDTH:],
        w_out=w_out.astype(BF16), norm2_w=row(norm2_w), w_up=ffn_w_up.astype(BF16),
        ffn_conv_w=ffn_conv_w, ffn_conv_b=row(ffn_conv_b), w_down=ffn_w_down.astype(BF16),
    )


def _prep_params(params):
    depth = params[0].shape[0]
    na_bias = _na_bias_tables(params[4])
    return [_prep_layer_params([p[i] for p in params], na_bias[i]) for i in range(depth)]


def _layer(x, h, lp, next_norm_w, consts, *, bsz, seq):
    t = _tiles(bsz, seq)
    cos_t, sin_t, tri, expand = consts
    proj, dt_raw = _matmul(h, lp["w_main"], lp["w_dt"], tm=t["tm_mm"], tn=MM_TN)
    o_na = _na_attention(proj, lp["na_bias"], lp["na_q_norm"], lp["na_k_norm"], bsz=bsz, seq=seq)
    qn, kn, vaug = _gqa_prep(proj, cos_t, sin_t, lp["gqa_q_norm"], lp["gqa_k_norm"], seq=seq, tm=t["tm_ew"])
    o_gqa = _gqa_attention(qn, kn, vaug, bsz=bsz, seq=seq, tq=t["tq"], tk=t["tk"])
    xs, bc, dt = _ssd_prep(proj, dt_raw, lp["conv_w"], lp["conv_b"], lp["dt_bias"], seq=seq, tm=t["tm_ew"])
    yf, yb = _ssd_scan(xs, bc, dt, lp["a_log"], tri, expand, bsz=bsz, seq=seq)
    o_ssd = _ssd_finish(yf, yb, xs, proj, lp["d_exp"], lp["ssd_norm_w"], tm=t["tm_ew"])
    x, h2 = _merge(o_na, o_ssd, o_gqa, proj, x, lp["w_na"], lp["w_ssd"], lp["w_gqa"], lp["w_out"], lp["norm2_w"],
                   tm=t["tm_merge"])
    u = _matmul(h2, lp["w_up"], tm=t["tm_mm"], tn=MM_TN)
    return _ffn_down(u, lp["ffn_conv_w"], lp["ffn_conv_b"], lp["w_down"], x, next_norm_w, seq=seq, tm=t["tm_down"])


def _trunk(x, layer_params):
    bsz, seq, d = x.shape
    assert d == D_MODEL and seq % 1024 == 0 and seq // GRID_W >= NA_KROWS
    consts = _rope_tables(seq) + _ssd_constants()
    x = x.reshape(bsz * seq, d)
    h = _rmsnorm(x, layer_params[0]["norm1_w"], tm=_tiles(bsz, seq)["tm_ew"])
    for li, lp in enumerate(layer_params):
        nxt = layer_params[li + 1]["norm1_w"] if li + 1 < len(layer_params) else None
        x, h = _layer(x, h, lp, nxt, consts, bsz=bsz, seq=seq)
    return x.reshape(bsz, seq, d)


def kernel(x_prompt, x_sample, norm1_w, w_in, na_q_norm, na_k_norm, na_rpb, ssd_conv_w, ssd_conv_b, ssd_dt_bias, ssd_a_log, ssd_d, ssd_norm_w, gqa_q_norm, gqa_k_norm, w_branch, w_out, norm2_w, ffn_w_up, ffn_conv_w, ffn_conv_b, ffn_w_down):
    params = (norm1_w, w_in, na_q_norm, na_k_norm, na_rpb, ssd_conv_w, ssd_conv_b, ssd_dt_bias, ssd_a_log,
              ssd_d, ssd_norm_w, gqa_q_norm, gqa_k_norm, w_branch, w_out, norm2_w, ffn_w_up, ffn_conv_w,
              ffn_conv_b, ffn_w_down)
    layer_params = _prep_params(params)
    return (_trunk(x_prompt, layer_params), _trunk(x_sample, layer_params))
```

```python
import functools
import math

import numpy as np
import jax
import jax.numpy as jnp
from jax import lax
from jax.experimental import pallas as pl
from jax.experimental.pallas import tpu as pltpu

F32 = jnp.float32
BF16 = jnp.bfloat16

D_MODEL = 2048
GRID_W = 64
EPS = 1e-6
NA_HEADS = 4
HEAD_DIM = 128
NA_WIDTH = NA_HEADS * HEAD_DIM
NA_WIN_H = 8
NA_WIN_W = 16
SSD_HEADS = 16
SSD_HEAD_DIM = 64
SSD_WIDTH = SSD_HEADS * SSD_HEAD_DIM
SSD_STATE = 128
SSD_GROUPS = 2
SSD_CONV = 5
SSD_CHUNK = 128
SSD_CPS = 8
SSD_BC = 2 * SSD_GROUPS * SSD_STATE
SSD_XBC = SSD_WIDTH + SSD_BC
GQA_HEADS = 4
GQA_KV_HEADS = 2
GQA_WIDTH = GQA_HEADS * HEAD_DIM
GQA_KV_WIDTH = GQA_KV_HEADS * HEAD_DIM
ROPE_THETA = 10000.0
D_FF = 5632
N_BRANCH = 3
GATES = N_BRANCH * D_MODEL
DT_PAD = 128

OFF_GATES = 0
OFF_XBC = OFF_GATES + GATES
OFF_NA = OFF_XBC + SSD_XBC
OFF_Z = OFF_NA + 3 * NA_WIDTH
OFF_GQ = OFF_Z + SSD_WIDTH
OFF_GK = OFF_GQ + GQA_WIDTH
OFF_GV = OFF_GK + GQA_KV_WIDTH
PROJ_COLS = OFF_GV + GQA_KV_WIDTH

NEG = -1e30
LOG2E = math.log2(math.e)
HALO = 8
VMEM_LIMIT = 56 * 2**20


def _cp(*sem):
    return pltpu.CompilerParams(dimension_semantics=sem, vmem_limit_bytes=VMEM_LIMIT)


def _sigmoid(x):
    return 0.5 + 0.5 * jnp.tanh(0.5 * x)


def _silu_of_half(half):
    return half + half * jnp.tanh(half)


def _silu(x):
    return _silu_of_half(0.5 * x)


NORM_ROWS = 128


def _rmsnorm_rows(x_ref, nw_ref, h_ref, n_rows):
    def body(r, c):
        rows = pl.ds(pl.multiple_of(r * NORM_ROWS, NORM_ROWS), NORM_ROWS)
        x = x_ref[rows, :]
        ms = jnp.mean(x * x, axis=-1, keepdims=True)
        h_ref[rows, :] = (x * lax.rsqrt(ms + EPS) * nw_ref[...]).astype(BF16)
        return c

    lax.fori_loop(0, n_rows // NORM_ROWS, body, 0)


def _rmsnorm_kernel(x_ref, nw_ref, h_ref, *, tm):
    _rmsnorm_rows(x_ref, nw_ref, h_ref, tm)


def _rmsnorm(x, nw, *, tm):
    m, d = x.shape
    return pl.pallas_call(
        functools.partial(_rmsnorm_kernel, tm=tm),
        grid=(m // tm,),
        in_specs=[pl.BlockSpec((tm, d), lambda i: (i, 0)), pl.BlockSpec((1, d), lambda i: (0, 0))],
        out_specs=pl.BlockSpec((tm, d), lambda i: (i, 0)),
        out_shape=jax.ShapeDtypeStruct((m, d), BF16),
        compiler_params=_cp("parallel"),
        name="rmsnorm",
    )(x, nw)


def _mm_kernel(*refs, with_dt):
    if with_dt:
        h_ref, w_ref, wdt_ref, o_ref, odt_ref = refs

        @pl.when(pl.program_id(1) == 0)
        def _():
            odt_ref[...] = jnp.dot(h_ref[...], wdt_ref[...], preferred_element_type=F32)
    else:
        h_ref, w_ref, o_ref = refs
    o_ref[...] = jnp.dot(h_ref[...], w_ref[...], preferred_element_type=F32).astype(o_ref.dtype)


def _matmul(h, w, wdt=None, *, tm, tn):
    m, d = h.shape
    n = w.shape[1]
    with_dt = wdt is not None
    in_specs = [
        pl.BlockSpec((tm, d), lambda i, j: (i, 0)),
        pl.BlockSpec((d, tn), lambda i, j: (0, j)),
    ]
    out_shape = [jax.ShapeDtypeStruct((m, n), BF16)]
    out_specs = [pl.BlockSpec((tm, tn), lambda i, j: (i, j))]
    args = [h, w]
    if with_dt:
        in_specs.append(pl.BlockSpec((d, DT_PAD), lambda i, j: (0, 0)))
        out_shape.append(jax.ShapeDtypeStruct((m, DT_PAD), F32))
        out_specs.append(pl.BlockSpec((tm, DT_PAD), lambda i, j: (i, 0)))
        args.append(wdt)
    res = pl.pallas_call(
        functools.partial(_mm_kernel, with_dt=with_dt),
        grid=(m // tm, n // tn),
        in_specs=in_specs,
        out_specs=out_specs,
        out_shape=out_shape,
        compiler_params=_cp("parallel", "arbitrary"),
        name="matmul_dt" if with_dt else "matmul",
    )(*args)
    return res if with_dt else res[0]


NA_QROWS = 8
NA_KROWS = 16
NA_TQ = NA_QROWS * GRID_W
NA_TK = NA_KROWS * GRID_W
NA_KPIECE = 256
NA_NPIECE = NA_TK // NA_KPIECE


def _na_bias_tables(rpb):
    rows = 4 * NA_QROWS
    nb = rows // NA_QROWS
    depth = rpb.shape[0]
    c = np.arange(GRID_W)
    col_start = np.clip(c - NA_WIN_W // 2, 0, GRID_W - NA_WIN_W)
    in_win = (c[None, :] >= col_start[:, None]) & (c[None, :] < col_start[:, None] + NA_WIN_W)
    dc = np.clip(c[None, :] - c[:, None], -(NA_WIN_W - 1), NA_WIN_W - 1) + NA_WIN_W - 1
    onehot_c = (dc[:, :, None] == np.arange(2 * NA_WIN_W - 1)).astype(np.float32)
    by_col = jnp.einsum("lhab,qcb->lhaqc", rpb.astype(F32), jnp.asarray(onehot_c), precision=lax.Precision.HIGHEST)
    by_col = by_col * LOG2E
    tables = []
    for b in (0, 1, nb - 1):
        r = NA_QROWS * b + np.arange(NA_QROWS)
        ks = int(np.clip(NA_QROWS * b - NA_WIN_H // 2, 0, rows - NA_KROWS))
        krow = ks + np.arange(NA_KROWS)
        rs = np.clip(r - NA_WIN_H // 2, 0, rows - NA_WIN_H)
        valid = (krow[None, :] >= rs[:, None]) & (krow[None, :] < rs[:, None] + NA_WIN_H)
        dr = np.clip(krow[None, :] - r[:, None], -(NA_WIN_H - 1), NA_WIN_H - 1) + NA_WIN_H - 1
        bias = jnp.take(by_col, jnp.asarray(dr.reshape(-1)), axis=2)
        bias = bias.reshape(depth, NA_HEADS, NA_QROWS, NA_KROWS, GRID_W, GRID_W).transpose(0, 1, 2, 4, 3, 5)
        mask = valid[:, None, :, None] & in_win[None, :, None, :]
        bias = jnp.where(jnp.asarray(mask)[None, None], bias, NEG)
        tables.append(bias.reshape(depth, NA_HEADS, NA_TQ, NA_TK))
    return jnp.stack(tables, axis=1)


def _head_rmsnorm(x, w):
    ms = jnp.mean(x * x, axis=-1, keepdims=True)
    return x * lax.rsqrt(ms + EPS) * w


NA_SUB = 128
NA_HPS = 4


def _na_kernel(q_ref, k0, k1, k2, k3, v0, v1, v2, v3, bias_ref, qw_ref, kw_ref, o_ref, kn_scr, va_scr, s_scr, *, nb):
    i = pl.program_id(2)
    kind = jnp.where(i == 0, 0, jnp.where(i == nb - 1, 2, 1))
    scale = HEAD_DIM ** -0.5 * LOG2E
    ones = jnp.ones((NA_KPIECE, HEAD_DIM), BF16)
    for hh in range(NA_HPS):
        hc = pl.ds(hh * HEAD_DIM, HEAD_DIM)
        for t, (k_ref, v_ref) in enumerate(zip((k0, k1, k2, k3), (v0, v1, v2, v3))):
            piece = pl.ds(t * NA_KPIECE, NA_KPIECE)
            kn_scr[hh, piece, :] = _head_rmsnorm(k_ref[:, hc].astype(F32), kw_ref[...]).astype(BF16)
            va_scr[hh, piece, :HEAD_DIM] = v_ref[:, hc]
            va_scr[hh, piece, HEAD_DIM:] = ones
    nlane = NA_TK // HEAD_DIM
    nsub = NA_TQ // NA_SUB

    def scores(u):
        hh, qs = divmod(u, nsub)
        rows = pl.ds(qs * NA_SUB, NA_SUB)
        qn = (_head_rmsnorm(q_ref[rows, pl.ds(hh * HEAD_DIM, HEAD_DIM)].astype(F32), qw_ref[...]) * scale).astype(BF16)
        s = lax.dot_general(qn, kn_scr[hh], (((1,), (1,)), ((), ())), preferred_element_type=F32)
        return s + bias_ref[kind, hh, rows, :]

    def finish(u, slot):
        hh, qs = divmod(u, nsub)
        blocks = [s_scr[slot, :, j * HEAD_DIM:(j + 1) * HEAD_DIM] for j in range(nlane)]
        part = blocks[0]
        for blk in blocks[1:]:
            part = jnp.maximum(part, blk)
        m = jnp.broadcast_to(jnp.max(part, axis=-1, keepdims=True), part.shape)
        p = jnp.concatenate([jnp.exp2(blk - m).astype(BF16) for blk in blocks], axis=1)
        pv = jnp.dot(p, va_scr[hh], preferred_element_type=F32)
        o_ref[pl.ds(qs * NA_SUB, NA_SUB), pl.ds(hh * HEAD_DIM, HEAD_DIM)] = (
            pv[:, :HEAD_DIM] / pv[:, HEAD_DIM:]).astype(o_ref.dtype)

    units = NA_HPS * nsub
    s_scr[0] = scores(0)
    for u in range(units):
        if u + 1 < units:
            s_scr[(u + 1) % 2] = scores(u + 1)
        finish(u, u % 2)


def _na_attention(proj, bias, qw, kw, *, bsz, seq):
    m = proj.shape[0]
    rows = seq // GRID_W
    nb = rows // NA_QROWS
    npieces = seq // NA_KPIECE
    width = NA_HPS * HEAD_DIM
    q_col = OFF_NA // width
    k_col = (OFF_NA + NA_WIDTH) // width
    v_col = (OFF_NA + 2 * NA_WIDTH) // width

    def kbase(i):
        return jnp.clip(2 * i - 1, 0, npieces - NA_NPIECE)

    def kv_spec(col, t):
        return pl.BlockSpec((NA_KPIECE, width), lambda h, b, i: (b * npieces + kbase(i) + t, col + h))

    in_specs = [pl.BlockSpec((NA_TQ, width), lambda h, b, i: (b * nb + i, q_col + h))]
    in_specs += [kv_spec(k_col, t) for t in range(NA_NPIECE)]
    in_specs += [kv_spec(v_col, t) for t in range(NA_NPIECE)]
    in_specs += [
        pl.BlockSpec((3, NA_HPS, NA_TQ, NA_TK), lambda h, b, i: (0, h, 0, 0), pipeline_mode=pl.Buffered(1)),
        pl.BlockSpec((1, HEAD_DIM), lambda h, b, i: (0, 0)),
        pl.BlockSpec((1, HEAD_DIM), lambda h, b, i: (0, 0)),
    ]
    return pl.pallas_call(
        functools.partial(_na_kernel, nb=nb),
        grid=(NA_HEADS // NA_HPS, bsz, nb),
        in_specs=in_specs,
        out_specs=pl.BlockSpec((NA_TQ, width), lambda h, b, i: (b * nb + i, h)),
        out_shape=jax.ShapeDtypeStruct((m, NA_WIDTH), BF16),
        scratch_shapes=[
            pltpu.VMEM((NA_HPS, NA_TK, HEAD_DIM), BF16),
            pltpu.VMEM((NA_HPS, NA_TK, 2 * HEAD_DIM), BF16),
            pltpu.VMEM((2, NA_SUB, NA_TK), F32),
        ],
        compiler_params=_cp("parallel", "parallel", "parallel"),
        name="na_attention",
    )(proj, *([proj] * (2 * NA_NPIECE)), bias, qw, kw)


def _rope_tables(seq):
    t = np.arange(seq)
    pos = np.stack([t // GRID_W, t % GRID_W], axis=-1).astype(np.float32)
    n_freq = HEAD_DIM // 4
    inv_freq = jnp.asarray(ROPE_THETA, F32) ** (-jnp.arange(n_freq, dtype=F32) / n_freq)
    ang = jnp.asarray(pos)[:, :, None] * inv_freq
    cos = jnp.cos(ang)
    sin = jnp.sin(ang)
    cos_t = jnp.concatenate([cos[:, 0], cos[:, 0], cos[:, 1], cos[:, 1]], axis=-1)
    sin_t = jnp.concatenate([-sin[:, 0], sin[:, 0], -sin[:, 1], sin[:, 1]], axis=-1)
    return cos_t, sin_t


def _rope(x, cos_t, sin_t, first_half):
    swapped = jnp.where(first_half, pltpu.roll(x, HEAD_DIM - 32, axis=1), pltpu.roll(x, 32, axis=1))
    return x * cos_t + swapped * sin_t


def _gqa_prep_kernel(q_ref, k_ref, v_ref, cos_ref, sin_ref, qw_ref, kw_ref, qo_ref, ko_ref, vo_ref):
    cos_t = cos_ref[...]
    sin_t = sin_ref[...]
    lane = lax.broadcasted_iota(jnp.int32, cos_t.shape, 1)
    first_half = (lane % 64) < 32
    scale = HEAD_DIM ** -0.5 * LOG2E
    for h in range(GQA_HEADS):
        cols = pl.ds(h * HEAD_DIM, HEAD_DIM)
        x = _head_rmsnorm(q_ref[:, cols].astype(F32), qw_ref[...])
        qo_ref[:, cols] = (_rope(x, cos_t, sin_t, first_half) * scale).astype(BF16)
    ones = jnp.ones(cos_t.shape, BF16)
    for h in range(GQA_KV_HEADS):
        cols = pl.ds(h * HEAD_DIM, HEAD_DIM)
        x = _head_rmsnorm(k_ref[:, cols].astype(F32), kw_ref[...])
        ko_ref[:, cols] = _rope(x, cos_t, sin_t, first_half).astype(BF16)
        vo_ref[:, pl.ds(2 * h * HEAD_DIM, HEAD_DIM)] = v_ref[:, cols]
        vo_ref[:, pl.ds((2 * h + 1) * HEAD_DIM, HEAD_DIM)] = ones


def _gqa_prep(proj, cos_t, sin_t, qw, kw, *, seq, tm):
    m = proj.shape[0]
    tps = seq // tm
    return pl.pallas_call(
        _gqa_prep_kernel,
        grid=(m // tm,),
        in_specs=[
            pl.BlockSpec((tm, GQA_WIDTH), lambda i: (i, OFF_GQ // GQA_WIDTH)),
            pl.BlockSpec((tm, GQA_KV_WIDTH), lambda i: (i, OFF_GK // GQA_KV_WIDTH)),
            pl.BlockSpec((tm, GQA_KV_WIDTH), lambda i: (i, OFF_GV // GQA_KV_WIDTH)),
            pl.BlockSpec((tm, HEAD_DIM), lambda i: (i % tps, 0)),
            pl.BlockSpec((tm, HEAD_DIM), lambda i: (i % tps, 0)),
            pl.BlockSpec((1, HEAD_DIM), lambda i: (0, 0)),
            pl.BlockSpec((1, HEAD_DIM), lambda i: (0, 0)),
        ],
        out_specs=[
            pl.BlockSpec((tm, GQA_WIDTH), lambda i: (i, 0)),
            pl.BlockSpec((tm, GQA_KV_WIDTH), lambda i: (i, 0)),
            pl.BlockSpec((tm, 2 * GQA_KV_WIDTH), lambda i: (i, 0)),
        ],
        out_shape=[
            jax.ShapeDtypeStruct((m, GQA_WIDTH), BF16),
            jax.ShapeDtypeStruct((m, GQA_KV_WIDTH), BF16),
            jax.ShapeDtypeStruct((m, 2 * GQA_KV_WIDTH), BF16),
        ],
        compiler_params=_cp("parallel"),
        name="gqa_prep",
    )(proj, proj, proj, cos_t, sin_t, qw, kw)


def _flash_kernel(q_ref, k_ref, v_ref, o_ref, s_scr, m_scr, acc_scr, *, tq, tk, seq):
    q2 = jnp.concatenate([q_ref[:, :HEAD_DIM], q_ref[:, HEAD_DIM:]], axis=0)
    m_scr[...] = jnp.full(m_scr.shape, NEG, F32)
    acc_scr[...] = jnp.zeros(acc_scr.shape, F32)
    nblk = tk // HEAD_DIM
    n = seq // tk

    def chunk(kc):
        return pl.ds(kc * tk, tk)

    def scores(kc):
        return lax.dot_general(q2, k_ref[chunk(kc), :], (((1,), (1,)), ((), ())), preferred_element_type=F32)

    def softmax_pv(kc, slot):
        blocks = [s_scr[slot, :, j * HEAD_DIM:(j + 1) * HEAD_DIM] for j in range(nblk)]
        part = blocks[0]
        for blk in blocks[1:]:
            part = jnp.maximum(part, blk)
        m_prev = m_scr[...]
        m_new = jnp.maximum(m_prev, jnp.max(part, axis=-1, keepdims=True))
        alpha = jnp.exp2(m_prev - m_new)
        p = jnp.concatenate([jnp.exp2(blk - m_new).astype(BF16) for blk in blocks], axis=1)
        pv = jnp.dot(p, v_ref[chunk(kc), :], preferred_element_type=F32)
        acc_scr[...] = jnp.concatenate([alpha, alpha], axis=1) * acc_scr[...] + pv
        m_scr[...] = m_new

    s_scr[0] = scores(0)

    for kc in range(n):
        if kc + 1 < n:
            s_scr[(kc + 1) % 2] = scores(kc + 1)
        softmax_pv(kc, kc % 2)
    o = acc_scr[:, :HEAD_DIM] / acc_scr[:, HEAD_DIM:]
    o_ref[:, :HEAD_DIM] = o[:tq].astype(o_ref.dtype)
    o_ref[:, HEAD_DIM:] = o[tq:].astype(o_ref.dtype)


def _gqa_attention(qn, kn, vaug, *, bsz, seq, tq, tk):
    m = qn.shape[0]
    nq = seq // tq
    rep = GQA_HEADS // GQA_KV_HEADS
    return pl.pallas_call(
        functools.partial(_flash_kernel, tq=tq, tk=tk, seq=seq),
        grid=(bsz, GQA_KV_HEADS, nq),
        in_specs=[
            pl.BlockSpec((tq, rep * HEAD_DIM), lambda b, g, i: (b * nq + i, g)),
            pl.BlockSpec((seq, HEAD_DIM), lambda b, g, i: (b, g)),
            pl.BlockSpec((seq, 2 * HEAD_DIM), lambda b, g, i: (b, g)),
        ],
        out_specs=pl.BlockSpec((tq, rep * HEAD_DIM), lambda b, g, i: (b * nq + i, g)),
        out_shape=jax.ShapeDtypeStruct((m, GQA_WIDTH), BF16),
        scratch_shapes=[
            pltpu.VMEM((2, rep * tq, tk), F32),
            pltpu.VMEM((rep * tq, HEAD_DIM), F32),
            pltpu.VMEM((rep * tq, 2 * HEAD_DIM), F32),
        ],
        compiler_params=_cp("parallel", "parallel", "parallel"),
        name="gqa_flash",
    )(qn, kn, vaug)


def _with_halo(x, prev, nxt):
    return jnp.concatenate([x, nxt, prev], axis=0)


def _shift_rows(xcat, s, n):
    return pltpu.roll(xcat, s % xcat.shape[0], axis=0)[:n]


def _dwconv(x, prev, nxt, w, b, width):
    n = x.shape[0]
    xcat = _with_halo(x, prev, nxt)
    pad = width // 2
    out = b + x * w[pad:pad + 1, :]
    for i in range(width):
        if i != pad:
            out = out + _shift_rows(xcat, pad - i, n) * w[i:i + 1, :]
    return out


def _halo_specs(tm, width, col_block, m):
    nblk = m // HALO
    per = tm // HALO
    prev = pl.BlockSpec((HALO, width), lambda i, *_: (jnp.maximum(i * per - 1, 0), col_block))
    nxt = pl.BlockSpec((HALO, width), lambda i, *_: (jnp.minimum((i + 1) * per, nblk - 1), col_block))
    return prev, nxt


def _ssd_prep_kernel(x_ref, prev_ref, next_ref, dtr_ref, cw_ref, cb_ref, dtb_ref, xs_ref, bc_ref, dt_ref, *, tps):
    i = pl.program_id(0)
    first = (i % tps) == 0
    last = (i % tps) == tps - 1
    v = dtr_ref[...] + dtb_ref[...]
    dt_ref[...] = jnp.maximum(v, 0.0) + jnp.log(1.0 + jnp.exp(-jnp.abs(v)))
    x = x_ref[...].astype(F32)
    prev = jnp.where(first, 0.0, prev_ref[...].astype(F32))
    nxt = jnp.where(last, 0.0, next_ref[...].astype(F32))
    y = _silu_of_half(_dwconv(x, prev, nxt, 0.5 * cw_ref[...], 0.5 * cb_ref[...], SSD_CONV))
    xs_ref[...] = y[:, :SSD_WIDTH].astype(BF16)
    bc_ref[...] = y[:, SSD_WIDTH:].astype(BF16)


def _ssd_prep(proj, dt_raw, conv_w, conv_b, dt_bias, *, seq, tm):
    m = proj.shape[0]
    cb = OFF_XBC // SSD_XBC
    prev, nxt = _halo_specs(tm, SSD_XBC, cb, m)
    return pl.pallas_call(
        functools.partial(_ssd_prep_kernel, tps=seq // tm),
        grid=(m // tm,),
        in_specs=[
            pl.BlockSpec((tm, SSD_XBC), lambda i: (i, cb)),
            prev,
            nxt,
            pl.BlockSpec((tm, DT_PAD), lambda i: (i, 0)),
            pl.BlockSpec((SSD_CONV, SSD_XBC), lambda i: (0, 0)),
            pl.BlockSpec((1, SSD_XBC), lambda i: (0, 0)),
            pl.BlockSpec((1, DT_PAD), lambda i: (0, 0)),
        ],
        out_specs=[
            pl.BlockSpec((tm, SSD_WIDTH), lambda i: (i, 0)),
            pl.BlockSpec((tm, SSD_BC), lambda i: (i, 0)),
            pl.BlockSpec((tm, DT_PAD), lambda i: (i, 0)),
        ],
        out_shape=[
            jax.ShapeDtypeStruct((m, SSD_WIDTH), BF16),
            jax.ShapeDtypeStruct((m, SSD_BC), BF16),
            jax.ShapeDtypeStruct((m, DT_PAD), F32),
        ],
        compiler_params=_cp("parallel"),
        name="ssd_prep",
    )(proj, proj, proj, dt_raw, conv_w, conv_b, dt_bias)


def _ssd_constants():
    q = SSD_CHUNK
    li = np.arange(q)
    tri = np.stack([li[:, None] >= li[None, :], li[:, None] <= li[None, :]]).astype(np.float32)
    hp = np.arange(SSD_WIDTH) // SSD_HEAD_DIM
    expand = np.zeros((2, 2 * DT_PAD, SSD_WIDTH), np.float32)
    for d in range(2):
        expand[d, d * SSD_HEADS + hp, np.arange(SSD_WIDTH)] = 1.0
        expand[d, DT_PAD + d * SSD_HEADS + hp, np.arange(SSD_WIDTH)] = 1.0
    return jnp.asarray(tri, F32), jnp.asarray(expand, BF16)


def _expand_heads(v, e):
    hi = v.astype(BF16)
    lo = (v - hi.astype(F32)).astype(BF16)
    return jnp.dot(jnp.concatenate([hi, lo], axis=1), e, preferred_element_type=F32)


def _ssd_scan_kernel(xs_f, bc_f, dt_f, xs_b, bc_b, dt_b, alog_ref, tri_ref, exp_ref, yf_ref, yb_ref, h_scr):
    @pl.when(pl.program_id(1) == 0)
    def _():
        h_scr[...] = jnp.zeros(h_scr.shape, F32)

    q = SSD_CHUNK
    gw = SSD_WIDTH // SSD_GROUPS
    a_row = -jnp.exp(alog_ref[...])
    li = lax.broadcasted_iota(jnp.int32, (q, q), 0)
    si = lax.broadcasted_iota(jnp.int32, (q, q), 1)
    lane = lax.broadcasted_iota(jnp.int32, (q, HEAD_DIM), 1)
    left = lane < SSD_HEAD_DIM
    dirs = ((xs_f, bc_f, dt_f, yf_ref, q - 1), (xs_b, bc_b, dt_b, yb_ref, 0))
    for cc, (d, (xs_ref, bc_ref, dt_ref, y_ref, last)) in [(cc, dd) for cc in range(SSD_CPS) for dd in enumerate(dirs)]:
        rows = pl.ds((cc if d == 0 else SSD_CPS - 1 - cc) * q, q)
        keep = (li >= si) if d == 0 else (li <= si)
        e = exp_ref[d]
        dt = dt_ref[rows, :]
        da = dt * a_row
        cum = jnp.dot(tri_ref[d], da, precision=lax.Precision.HIGHEST, preferred_element_type=F32)
        cum_t = cum.T
        tot = cum[last:last + 1, :]
        stack_e = _expand_heads(jnp.concatenate([dt, jnp.exp(cum), jnp.exp(tot - cum)], axis=0), e)
        dt_e, ec_e, w_e = stack_e[:q], stack_e[q:2 * q], stack_e[2 * q:]
        dec_e = ec_e[last:last + 1, :]
        xd = xs_ref[rows, :].astype(F32) * dt_e
        xdb = xd.astype(BF16)
        xw = (xd * w_e).astype(BF16)
        for g in range(SSD_GROUPS):
            gl = slice(g * gw, (g + 1) * gw)
            bg = bc_ref[rows, g * SSD_STATE:(g + 1) * SSD_STATE]
            cg = bc_ref[rows, (SSD_GROUPS + g) * SSD_STATE:(SSD_GROUPS + g + 1) * SSD_STATE]
            cb = lax.dot_general(cg, bg, (((1,), (1,)), ((), ())), preferred_element_type=F32)
            bg_t = bg.astype(F32).T.astype(BF16)
            h_prev = h_scr[d, g]
            y_off = jnp.dot(cg, h_prev.astype(BF16), preferred_element_type=F32) * ec_e[:, gl]
            states_t = jnp.dot(bg_t, xw[:, gl], preferred_element_type=F32)
            h_scr[d, g] = h_prev * dec_e[:, gl] + states_t
            for j in range(gw // HEAD_DIM):
                mats = []
                for hh in range(2):
                    col = d * SSD_HEADS + g * (SSD_HEADS // SSD_GROUPS) + 2 * j + hh
                    seg = cum[:, col:col + 1] - cum_t[col:col + 1, :]
                    lm = jnp.exp(jnp.where(keep, seg, NEG))
                    mats.append((cb * lm).astype(BF16))
                lo = g * gw + j * HEAD_DIM
                slab = xdb[:, lo:lo + HEAD_DIM]
                zero = jnp.zeros_like(slab)
                wbd = jnp.concatenate([jnp.where(left, slab, zero), jnp.where(left, zero, slab)], axis=0)
                y_diag = jnp.dot(jnp.concatenate(mats, axis=1), wbd, preferred_element_type=F32)
                y_ref[rows, lo:lo + HEAD_DIM] = (y_diag + y_off[:, j * HEAD_DIM:(j + 1) * HEAD_DIM]).astype(y_ref.dtype)


def _ssd_scan(xs, bc, dt, a_log, tri, expand, *, bsz, seq):
    m = xs.shape[0]
    nc = seq // (SSD_CPS * SSD_CHUNK)
    blk = SSD_CPS * SSD_CHUNK
    fwd = lambda b, c: (b * nc + c, 0)
    bwd = lambda b, c: (b * nc + nc - 1 - c, 0)
    return pl.pallas_call(
        _ssd_scan_kernel,
        grid=(bsz, nc),
        in_specs=[
            pl.BlockSpec((blk,SSD_WIDTH), fwd),
            pl.BlockSpec((blk,SSD_BC), fwd),
            pl.BlockSpec((blk,DT_PAD), fwd),
            pl.BlockSpec((blk,SSD_WIDTH), bwd),
            pl.BlockSpec((blk,SSD_BC), bwd),
            pl.BlockSpec((blk,DT_PAD), bwd),
            pl.BlockSpec((1, DT_PAD), lambda b, c: (0, 0)),
            pl.BlockSpec((2, SSD_CHUNK, SSD_CHUNK), lambda b, c: (0, 0, 0)),
            pl.BlockSpec((2, 2 * DT_PAD, SSD_WIDTH), lambda b, c: (0, 0, 0)),
        ],
        out_specs=[
            pl.BlockSpec((blk,SSD_WIDTH), fwd),
            pl.BlockSpec((blk,SSD_WIDTH), bwd),
        ],
        out_shape=[
            jax.ShapeDtypeStruct((m, SSD_WIDTH), BF16),
            jax.ShapeDtypeStruct((m, SSD_WIDTH), BF16),
        ],
        scratch_shapes=[pltpu.VMEM((2, SSD_GROUPS, SSD_STATE, SSD_WIDTH // SSD_GROUPS), F32)],
        compiler_params=_cp("parallel", "arbitrary"),
        name="ssd_scan",
    )(xs, bc, dt, xs, bc, dt, a_log, tri, expand)


def _ssd_finish_kernel(yf_ref, yb_ref, xs_ref, z_ref, d_ref, nw_ref, o_ref):
    y = yf_ref[...].astype(F32) + yb_ref[...].astype(F32) + xs_ref[...].astype(F32) * d_ref[...]
    y = y * _silu(z_ref[...].astype(F32))
    gw = SSD_WIDTH // SSD_GROUPS
    for g in range(SSD_GROUPS):
        yg = y[:, g * gw:(g + 1) * gw]
        ms = jnp.mean(yg * yg, axis=-1, keepdims=True)
        o_ref[:, g * gw:(g + 1) * gw] = (yg * lax.rsqrt(ms + EPS) * nw_ref[:, g * gw:(g + 1) * gw]).astype(o_ref.dtype)


def _ssd_finish(yf, yb, xs, proj, d_exp, norm_w, *, tm):
    m = yf.shape[0]
    row = lambda i: (i, 0)
    const = lambda i: (0, 0)
    return pl.pallas_call(
        _ssd_finish_kernel,
        grid=(m // tm,),
        in_specs=[
            pl.BlockSpec((tm, SSD_WIDTH), row),
            pl.BlockSpec((tm, SSD_WIDTH), row),
            pl.BlockSpec((tm, SSD_WIDTH), row),
            pl.BlockSpec((tm, SSD_WIDTH), lambda i: (i, OFF_Z // SSD_WIDTH)),
            pl.BlockSpec((1, SSD_WIDTH), const),
            pl.BlockSpec((1, SSD_WIDTH), const),
        ],
        out_specs=pl.BlockSpec((tm, SSD_WIDTH), row),
        out_shape=jax.ShapeDtypeStruct((m, SSD_WIDTH), BF16),
        compiler_params=_cp("parallel"),
        name="ssd_finish",
    )(yf, yb, xs, proj, d_exp, norm_w)


def _merge_kernel(ona_ref, ossd_ref, ogqa_ref, g_ref, x_ref, wna_ref, wssd_ref, wgqa_ref, wout_ref, nw_ref,
                  o_ref, h_ref, mg_scr, *, tm):
    nc = 512
    branches = ((ona_ref, wna_ref), (ossd_ref, wssd_ref), (ogqa_ref, wgqa_ref))
    for c in range(D_MODEL // nc):
        cols = pl.ds(c * nc, nc)
        acc = None
        for b, (o_b, w_b) in enumerate(branches):
            p = jnp.dot(o_b[...], w_b[:, cols], preferred_element_type=F32)
            gate = _sigmoid(g_ref[:, pl.ds(b * D_MODEL + c * nc, nc)].astype(F32))
            acc = gate * p if acc is None else acc + gate * p
        mg_scr[:, cols] = acc.astype(BF16)
    for c in range(D_MODEL // nc):
        cols = pl.ds(c * nc, nc)
        o_ref[:, cols] = x_ref[:, cols] + jnp.dot(mg_scr[...], wout_ref[:, cols], preferred_element_type=F32)
    _rmsnorm_rows(o_ref, nw_ref, h_ref, tm)


def _merge(o_na, o_ssd, o_gqa, proj, x, w_na, w_ssd, w_gqa, w_out, norm_w, *, tm):
    m = x.shape[0]
    row = lambda i: (i, 0)
    const = lambda i: (0, 0)
    one = pl.Buffered(1)
    return pl.pallas_call(
        functools.partial(_merge_kernel, tm=tm),
        grid=(m // tm,),
        in_specs=[
            pl.BlockSpec((tm, NA_WIDTH), row),
            pl.BlockSpec((tm, SSD_WIDTH), row),
            pl.BlockSpec((tm, GQA_WIDTH), row),
            pl.BlockSpec((tm, GATES), row),
            pl.BlockSpec((tm, D_MODEL), row),
            pl.BlockSpec((NA_WIDTH, D_MODEL), const, pipeline_mode=one),
            pl.BlockSpec((SSD_WIDTH, D_MODEL), const, pipeline_mode=one),
            pl.BlockSpec((GQA_WIDTH, D_MODEL), const, pipeline_mode=one),
            pl.BlockSpec((D_MODEL, D_MODEL), const, pipeline_mode=one),
            pl.BlockSpec((1, D_MODEL), const),
        ],
        out_specs=[pl.BlockSpec((tm, D_MODEL), row), pl.BlockSpec((tm, D_MODEL), row)],
        out_shape=[jax.ShapeDtypeStruct((m, D_MODEL), F32), jax.ShapeDtypeStruct((m, D_MODEL), BF16)],
        scratch_shapes=[pltpu.VMEM((tm, D_MODEL), BF16)],
        compiler_params=_cp("parallel"),
        name="merge",
    )(o_na, o_ssd, o_gqa, proj, x, w_na, w_ssd, w_gqa, w_out, norm_w)


FFN_CHUNK = 256


def _ffn_down_kernel(*refs, tps, tm, with_norm):
    if with_norm:
        ug_ref, prev_ref, next_ref, uv_ref, cw_ref, cb_ref, wd_ref, x_ref, nw_ref, o_ref, h_ref, hid_scr = refs
    else:
        ug_ref, prev_ref, next_ref, uv_ref, cw_ref, cb_ref, wd_ref, x_ref, o_ref, hid_scr = refs
    i = pl.program_id(0)
    first = (i % tps) == 0
    last = (i % tps) == tps - 1

    def gated(c):
        cols = pl.ds(c * FFN_CHUNK, FFN_CHUNK)
        g = ug_ref[:, cols].astype(F32)
        prev = jnp.where(first, 0.0, prev_ref[:, cols].astype(F32))
        nxt = jnp.where(last, 0.0, next_ref[:, cols].astype(F32))
        half = _dwconv(g, prev, nxt, 0.5 * cw_ref[:, cols], 0.5 * cb_ref[:, cols], 3)
        hid_scr[:, cols] = _silu_of_half(half).astype(BF16) * uv_ref[:, cols]

    nchunk = D_FF // FFN_CHUNK
    gated(0)
    for c in range(nchunk):
        rows = pl.ds(c * FFN_CHUNK, FFN_CHUNK)
        part = jnp.dot(hid_scr[:, rows], wd_ref[rows, :], preferred_element_type=F32)
        if c + 1 < nchunk:
            gated(c + 1)
        if c == 0:
            o_ref[...] = x_ref[...] + part
        else:
            o_ref[...] += part
    if with_norm:
        _rmsnorm_rows(o_ref, nw_ref, h_ref, tm)


def _ffn_down(u, conv_w, conv_b, w_down, x, next_norm_w=None, *, seq, tm):
    m = x.shape[0]
    nblk = m // HALO
    per = tm // HALO
    const = lambda i: (0, 0)
    row = lambda i: (i, 0)
    with_norm = next_norm_w is not None
    in_specs = [
        pl.BlockSpec((tm, D_FF), lambda i: (i, 0)),
        pl.BlockSpec((HALO, D_FF), lambda i: (jnp.maximum(i * per - 1, 0), 0)),
        pl.BlockSpec((HALO, D_FF), lambda i: (jnp.minimum((i + 1) * per, nblk - 1), 0)),
        pl.BlockSpec((tm, D_FF), lambda i: (i, 1)),
        pl.BlockSpec((3, D_FF), const),
        pl.BlockSpec((1, D_FF), const),
        pl.BlockSpec((D_FF, D_MODEL), const, pipeline_mode=pl.Buffered(1)),
        pl.BlockSpec((tm, D_MODEL), row),
    ]
    out_specs = [pl.BlockSpec((tm, D_MODEL), row)]
    out_shape = [jax.ShapeDtypeStruct((m, D_MODEL), F32)]
    args = [u, u, u, u, conv_w, conv_b, w_down, x]
    if with_norm:
        in_specs.append(pl.BlockSpec((1, D_MODEL), const))
        out_specs.append(pl.BlockSpec((tm, D_MODEL), row))
        out_shape.append(jax.ShapeDtypeStruct((m, D_MODEL), BF16))
        args.append(next_norm_w)
    res = pl.pallas_call(
        functools.partial(_ffn_down_kernel, tps=seq // tm, tm=tm, with_norm=with_norm),
        grid=(m // tm,),
        in_specs=in_specs,
        out_specs=out_specs,
        out_shape=out_shape,
        scratch_shapes=[pltpu.VMEM((tm, D_FF), BF16)],
        compiler_params=_cp("parallel"),
        name="ffn_down",
    )(*args)
    return res if with_norm else (res[0], None)


MM_TN = 1024


def _tiles(bsz, seq):
    m = bsz * seq
    return dict(tm_mm=2048 if m % 2048 == 0 else 1024, tm_ew=min(seq, 1024), tm_merge=min(seq, 256),
                tm_down=min(seq, 256), tq=min(seq, 512), tk=min(seq, 1024))


def _prep_layer_params(p, na_bias):
    (norm1_w, w_in, na_q_norm, na_k_norm, _, ssd_conv_w, ssd_conv_b, ssd_dt_bias, ssd_a_log, ssd_d,
     ssd_norm_w, gqa_q_norm, gqa_k_norm, w_branch, w_out, norm2_w, ffn_w_up, ffn_conv_w, ffn_conv_b, ffn_w_down) = p
    sizes = (3 * NA_WIDTH, GQA_WIDTH, GQA_KV_WIDTH, GQA_KV_WIDTH, SSD_WIDTH, SSD_XBC, 2 * SSD_HEADS, GATES)
    offs = np.concatenate([[0], np.cumsum(sizes)])
    piece = lambda k: w_in[:, offs[k]:offs[k + 1]].astype(BF16)
    na, gq, gk, gv, z, xbc, dtw, gates = (piece(k) for k in range(8))
    w_main = jnp.concatenate([gates, xbc, na, z, gq, gk, gv], axis=1)
    w_dt = jnp.pad(dtw, ((0, 0), (0, DT_PAD - 2 * SSD_HEADS)))
    pad_dt = lambda v: jnp.pad(v.reshape(1, 2 * SSD_HEADS), ((0, 0), (0, DT_PAD - 2 * SSD_HEADS)))
    row = lambda v: v.reshape(1, -1)
    wb = w_branch.astype(BF16)
    return dict(
        norm1_w=row(norm1_w), w_main=w_main, w_dt=w_dt,
        na_q_norm=row(na_q_norm), na_k_norm=row(na_k_norm), na_bias=na_bias,
        conv_w=ssd_conv_w, conv_b=row(ssd_conv_b), dt_bias=pad_dt(ssd_dt_bias), a_log=pad_dt(ssd_a_log),
        d_exp=row(jnp.repeat(ssd_d, SSD_HEAD_DIM)), ssd_norm_w=row(ssd_norm_w),
        gqa_q_norm=row(gqa_q_norm), gqa_k_norm=row(gqa_k_norm),
        w_na=wb[:NA_WIDTH], w_ssd=wb[NA_WIDTH:NA_WIDTH + SSD_WIDTH], w_gqa=wb[NA_WIDTH + SSD_WIDTH:],
        w_out=w_out.astype(BF16), norm2_w=row(norm2_w), w_up=ffn_w_up.astype(BF16),
        ffn_conv_w=ffn_conv_w, ffn_conv_b=row(ffn_conv_b), w_down=ffn_w_down.astype(BF16),
    )


def _prep_params(params):
    depth = params[0].shape[0]
    na_bias = _na_bias_tables(params[4])
    return [_prep_layer_params([p[i] for p in params], na_bias[i]) for i in range(depth)]


def _layer(x, h, lp, next_norm_w, consts, *, bsz, seq):
    t = _tiles(bsz, seq)
    cos_t, sin_t, tri, expand = consts
    proj, dt_raw = _matmul(h, lp["w_main"], lp["w_dt"], tm=t["tm_mm"], tn=MM_TN)
    o_na = _na_attention(proj, lp["na_bias"], lp["na_q_norm"], lp["na_k_norm"], bsz=bsz, seq=seq)
    qn, kn, vaug = _gqa_prep(proj, cos_t, sin_t, lp["gqa_q_norm"], lp["gqa_k_norm"], seq=seq, tm=t["tm_ew"])
    o_gqa = _gqa_attention(qn, kn, vaug, bsz=bsz, seq=seq, tq=t["tq"], tk=t["tk"])
    xs, bc, dt = _ssd_prep(proj, dt_raw, lp["conv_w"], lp["conv_b"], lp["dt_bias"], seq=seq, tm=t["tm_ew"])
    yf, yb = _ssd_scan(xs, bc, dt, lp["a_log"], tri, expand, bsz=bsz, seq=seq)
    o_ssd = _ssd_finish(yf, yb, xs, proj, lp["d_exp"], lp["ssd_norm_w"], tm=t["tm_ew"])
    x, h2 = _merge(o_na, o_ssd, o_gqa, proj, x, lp["w_na"], lp["w_ssd"], lp["w_gqa"], lp["w_out"], lp["norm2_w"],
                   tm=t["tm_merge"])
    u = _matmul(h2, lp["w_up"], tm=t["tm_mm"], tn=MM_TN)
    return _ffn_down(u, lp["ffn_conv_w"], lp["ffn_conv_b"], lp["w_down"], x, next_norm_w, seq=seq, tm=t["tm_down"])


def _trunk(x, layer_params):
    bsz, seq, d = x.shape
    assert d == D_MODEL and seq % 1024 == 0 and seq // GRID_W >= NA_KROWS
    consts = _rope_tables(seq) + _ssd_constants()
    x = x.reshape(bsz * seq, d)
    h = _rmsnorm(x, layer_params[0]["norm1_w"], tm=_tiles(bsz, seq)["tm_ew"])
    for li, lp in enumerate(layer_params):
        nxt = layer_params[li + 1]["norm1_w"] if li + 1 < len(layer_params) else None
        x, h = _layer(x, h, lp, nxt, consts, bsz=bsz, seq=seq)
    return x.reshape(bsz, seq, d)


def kernel(x_prompt, x_sample, norm1_w, w_in, na_q_norm, na_k_norm, na_rpb, ssd_conv_w, ssd_conv_b, ssd_dt_bias, ssd_a_log, ssd_d, ssd_norm_w, gqa_q_norm, gqa_k_norm, w_branch, w_out, norm2_w, ffn_w_up, ffn_conv_w, ffn_conv_b, ffn_w_down):
    params = (norm1_w, w_in, na_q_norm, na_k_norm, na_rpb, ssd_conv_w, ssd_conv_b, ssd_dt_bias, ssd_a_log,
              ssd_d, ssd_norm_w, gqa_q_norm, gqa_k_norm, w_branch, w_out, norm2_w, ffn_w_up, ffn_conv_w,
              ffn_conv_b, ffn_w_down)
    layer_params = _prep_params(params)
    return (_trunk(x_prompt, layer_params), _trunk(x_sample, layer_params))
```

```python
import functools
import math

import numpy as np
import jax
import jax.numpy as jnp
from jax import lax
from jax.experimental import pallas as pl
from jax.experimental.pallas import tpu as pltpu

F32 = jnp.float32
BF16 = jnp.bfloat16

D_MODEL = 2048
GRID_W = 64
EPS = 1e-6
NA_HEADS = 4
HEAD_DIM = 128
NA_WIDTH = NA_HEADS * HEAD_DIM
NA_WIN_H = 8
NA_WIN_W = 16
SSD_HEADS = 16
SSD_HEAD_DIM = 64
SSD_WIDTH = SSD_HEADS * SSD_HEAD_DIM
SSD_STATE = 128
SSD_GROUPS = 2
SSD_CONV = 5
SSD_CHUNK = 128
SSD_CPS = 8
SSD_BC = 2 * SSD_GROUPS * SSD_STATE
SSD_XBC = SSD_WIDTH + SSD_BC
GQA_HEADS = 4
GQA_KV_HEADS = 2
GQA_WIDTH = GQA_HEADS * HEAD_DIM
GQA_KV_WIDTH = GQA_KV_HEADS * HEAD_DIM
ROPE_THETA = 10000.0
D_FF = 5632
N_BRANCH = 3
GATES = N_BRANCH * D_MODEL
DT_PAD = 128

OFF_GATES = 0
OFF_XBC = OFF_GATES + GATES
OFF_NA = OFF_XBC + SSD_XBC
OFF_Z = OFF_NA + 3 * NA_WIDTH
OFF_GQ = OFF_Z + SSD_WIDTH
OFF_GK = OFF_GQ + GQA_WIDTH
OFF_GV = OFF_GK + GQA_KV_WIDTH
PROJ_COLS = OFF_GV + GQA_KV_WIDTH

NEG = -1e30
LOG2E = math.log2(math.e)
HALO = 8
VMEM_LIMIT = 56 * 2**20


def _cp(*sem):
    return pltpu.CompilerParams(dimension_semantics=sem, vmem_limit_bytes=VMEM_LIMIT)


def _sigmoid(x):
    return 0.5 + 0.5 * jnp.tanh(0.5 * x)


def _silu_of_half(half):
    return half + half * jnp.tanh(half)


def _silu(x):
    return _silu_of_half(0.5 * x)


NORM_ROWS = 128


def _rmsnorm_rows(x_ref, nw_ref, h_ref, n_rows):
    def body(r, c):
        rows = pl.ds(pl.multiple_of(r * NORM_ROWS, NORM_ROWS), NORM_ROWS)
        x = x_ref[rows, :]
        ms = jnp.mean(x * x, axis=-1, keepdims=True)
        h_ref[rows, :] = (x * lax.rsqrt(ms + EPS) * nw_ref[...]).astype(BF16)
        return c

    lax.fori_loop(0, n_rows // NORM_ROWS, body, 0)


def _rmsnorm_kernel(x_ref, nw_ref, h_ref, *, tm):
    _rmsnorm_rows(x_ref, nw_ref, h_ref, tm)


def _rmsnorm(x, nw, *, tm):
    m, d = x.shape
    return pl.pallas_call(
        functools.partial(_rmsnorm_kernel, tm=tm),
        grid=(m // tm,),
        in_specs=[pl.BlockSpec((tm, d), lambda i: (i, 0)), pl.BlockSpec((1, d), lambda i: (0, 0))],
        out_specs=pl.BlockSpec((tm, d), lambda i: (i, 0)),
        out_shape=jax.ShapeDtypeStruct((m, d), BF16),
        compiler_params=_cp("parallel"),
        name="rmsnorm",
    )(x, nw)


def _mm_kernel(*refs, with_dt):
    if with_dt:
        h_ref, w_ref, wdt_ref, o_ref, odt_ref = refs

        @pl.when(pl.program_id(1) == 0)
        def _():
            odt_ref[...] = jnp.dot(h_ref[...], wdt_ref[...], preferred_element_type=F32)
    else:
        h_ref, w_ref, o_ref = refs
    o_ref[...] = jnp.dot(h_ref[...], w_ref[...], preferred_element_type=F32).astype(o_ref.dtype)


def _matmul(h, w, wdt=None, *, tm, tn):
    m, d = h.shape
    n = w.shape[1]
    with_dt = wdt is not None
    in_specs = [
        pl.BlockSpec((tm, d), lambda i, j: (i, 0)),
        pl.BlockSpec((d, tn), lambda i, j: (0, j)),
    ]
    out_shape = [jax.ShapeDtypeStruct((m, n), BF16)]
    out_specs = [pl.BlockSpec((tm, tn), lambda i, j: (i, j))]
    args = [h, w]
    if with_dt:
        in_specs.append(pl.BlockSpec((d, DT_PAD), lambda i, j: (0, 0)))
        out_shape.append(jax.ShapeDtypeStruct((m, DT_PAD), F32))
        out_specs.append(pl.BlockSpec((tm, DT_PAD), lambda i, j: (i, 0)))
        args.append(wdt)
    res = pl.pallas_call(
        functools.partial(_mm_kernel, with_dt=with_dt),
        grid=(m // tm, n // tn),
        in_specs=in_specs,
        out_specs=out_specs,
        out_shape=out_shape,
        compiler_params=_cp("parallel", "arbitrary"),
        name="matmul_dt" if with_dt else "matmul",
    )(*args)
    return res if with_dt else res[0]


NA_QROWS = 8
NA_KROWS = 16
NA_TQ = NA_QROWS * GRID_W
NA_TK = NA_KROWS * GRID_W
NA_KPIECE = 256
NA_NPIECE = NA_TK // NA_KPIECE


def _na_bias_tables(rpb):
    rows = 4 * NA_QROWS
    nb = rows // NA_QROWS
    depth = rpb.shape[0]
    c = np.arange(GRID_W)
    col_start = np.clip(c - NA_WIN_W // 2, 0, GRID_W - NA_WIN_W)
    in_win = (c[None, :] >= col_start[:, None]) & (c[None, :] < col_start[:, None] + NA_WIN_W)
    dc = np.clip(c[None, :] - c[:, None], -(NA_WIN_W - 1), NA_WIN_W - 1) + NA_WIN_W - 1
    onehot_c = (dc[:, :, None] == np.arange(2 * NA_WIN_W - 1)).astype(np.float32)
    by_col = jnp.einsum("lhab,qcb->lhaqc", rpb.astype(F32), jnp.asarray(onehot_c), precision=lax.Precision.HIGHEST)
    by_col = by_col * LOG2E
    tables = []
    for b in (0, 1, nb - 1):
        r = NA_QROWS * b + np.arange(NA_QROWS)
        ks = int(np.clip(NA_QROWS * b - NA_WIN_H // 2, 0, rows - NA_KROWS))
        krow = ks + np.arange(NA_KROWS)
        rs = np.clip(r - NA_WIN_H // 2, 0, rows - NA_WIN_H)
        valid = (krow[None, :] >= rs[:, None]) & (krow[None, :] < rs[:, None] + NA_WIN_H)
        dr = np.clip(krow[None, :] - r[:, None], -(NA_WIN_H - 1), NA_WIN_H - 1) + NA_WIN_H - 1
        bias = jnp.take(by_col, jnp.asarray(dr.reshape(-1)), axis=2)
        bias = bias.reshape(depth, NA_HEADS, NA_QROWS, NA_KROWS, GRID_W, GRID_W).transpose(0, 1, 2, 4, 3, 5)
        mask = valid[:, None, :, None] & in_win[None, :, None, :]
        bias = jnp.where(jnp.asarray(mask)[None, None], bias, NEG)
        tables.append(bias.reshape(depth, NA_HEADS, NA_TQ, NA_TK))
    return jnp.stack(tables, axis=1)


def _head_rmsnorm(x, w):
    ms = jnp.mean(x * x, axis=-1, keepdims=True)
    return x * lax.rsqrt(ms + EPS) * w


NA_SUB = 128
NA_HPS = 4


def _na_kernel(q_ref, k0, k1, k2, k3, v0, v1, v2, v3, bias_ref, qw_ref, kw_ref, o_ref, kn_scr, va_scr, s_scr, *, nb):
    i = pl.program_id(2)
    kind = jnp.where(i == 0, 0, jnp.where(i == nb - 1, 2, 1))
    scale = HEAD_DIM ** -0.5 * LOG2E
    ones = jnp.ones((NA_KPIECE, HEAD_DIM), BF16)
    for hh in range(NA_HPS):
        hc = pl.ds(hh * HEAD_DIM, HEAD_DIM)
        for t, (k_ref, v_ref) in enumerate(zip((k0, k1, k2, k3), (v0, v1, v2, v3))):
            piece = pl.ds(t * NA_KPIECE, NA_KPIECE)
            kn_scr[hh, piece, :] = _head_rmsnorm(k_ref[:, hc].astype(F32), kw_ref[...]).astype(BF16)
            va_scr[hh, piece, :HEAD_DIM] = v_ref[:, hc]
            va_scr[hh, piece, HEAD_DIM:] = ones
    nlane = NA_TK // HEAD_DIM
    nsub = NA_TQ // NA_SUB

    def scores(u):
        hh, qs = divmod(u, nsub)
        rows = pl.ds(qs * NA_SUB, NA_SUB)
        qn = (_head_rmsnorm(q_ref[rows, pl.ds(hh * HEAD_DIM, HEAD_DIM)].astype(F32), qw_ref[...]) * scale).astype(BF16)
        s = lax.dot_general(qn, kn_scr[hh], (((1,), (1,)), ((), ())), preferred_element_type=F32)
        return s + bias_ref[kind, hh, rows, :]

    def finish(u, slot):
        hh, qs = divmod(u, nsub)
        blocks = [s_scr[slot, :, j * HEAD_DIM:(j + 1) * HEAD_DIM] for j in range(nlane)]
        part = blocks[0]
        for blk in blocks[1:]:
            part = jnp.maximum(part, blk)
        m = jnp.broadcast_to(jnp.max(part, axis=-1, keepdims=True), part.shape)
        p = jnp.concatenate([jnp.exp2(blk - m).astype(BF16) for blk in blocks], axis=1)
        pv = jnp.dot(p, va_scr[hh], preferred_element_type=F32)
        o_ref[pl.ds(qs * NA_SUB, NA_SUB), pl.ds(hh * HEAD_DIM, HEAD_DIM)] = (
            pv[:, :HEAD_DIM] / pv[:, HEAD_DIM:]).astype(o_ref.dtype)

    units = NA_HPS * nsub
    s_scr[0] = scores(0)
    for u in range(units):
        if u + 1 < units:
            s_scr[(u + 1) % 2] = scores(u + 1)
        finish(u, u % 2)


def _na_attention(proj, bias, qw, kw, *, bsz, seq):
    m = proj.shape[0]
    rows = seq // GRID_W
    nb = rows // NA_QROWS
    npieces = seq // NA_KPIECE
    width = NA_HPS * HEAD_DIM
    q_col = OFF_NA // width
    k_col = (OFF_NA + NA_WIDTH) // width
    v_col = (OFF_NA + 2 * NA_WIDTH) // width

    def kbase(i):
        return jnp.clip(2 * i - 1, 0, npieces - NA_NPIECE)

    def kv_spec(col, t):
        return pl.BlockSpec((NA_KPIECE, width), lambda h, b, i: (b * npieces + kbase(i) + t, col + h))

    in_specs = [pl.BlockSpec((NA_TQ, width), lambda h, b, i: (b * nb + i, q_col + h))]
    in_specs += [kv_spec(k_col, t) for t in range(NA_NPIECE)]
    in_specs += [kv_spec(v_col, t) for t in range(NA_NPIECE)]
    in_specs += [
        pl.BlockSpec((3, NA_HPS, NA_TQ, NA_TK), lambda h, b, i: (0, h, 0, 0), pipeline_mode=pl.Buffered(1)),
        pl.BlockSpec((1, HEAD_DIM), lambda h, b, i: (0, 0)),
        pl.BlockSpec((1, HEAD_DIM), lambda h, b, i: (0, 0)),
    ]
    return pl.pallas_call(
        functools.partial(_na_kernel, nb=nb),
        grid=(NA_HEADS // NA_HPS, bsz, nb),
        in_specs=in_specs,
        out_specs=pl.BlockSpec((NA_TQ, width), lambda h, b, i: (b * nb + i, h)),
        out_shape=jax.ShapeDtypeStruct((m, NA_WIDTH), BF16),
        scratch_shapes=[
            pltpu.VMEM((NA_HPS, NA_TK, HEAD_DIM), BF16),
            pltpu.VMEM((NA_HPS, NA_TK, 2 * HEAD_DIM), BF16),
            pltpu.VMEM((2, NA_SUB, NA_TK), F32),
        ],
        compiler_params=_cp("parallel", "parallel", "parallel"),
        name="na_attention",
    )(proj, *([proj] * (2 * NA_NPIECE)), bias, qw, kw)


def _rope_tables(seq):
    t = np.arange(seq)
    pos = np.stack([t // GRID_W, t % GRID_W], axis=-1).astype(np.float32)
    n_freq = HEAD_DIM // 4
    inv_freq = jnp.asarray(ROPE_THETA, F32) ** (-jnp.arange(n_freq, dtype=F32) / n_freq)
    ang = jnp.asarray(pos)[:, :, None] * inv_freq
    cos = jnp.cos(ang)
    sin = jnp.sin(ang)
    cos_t = jnp.concatenate([cos[:, 0], cos[:, 0], cos[:, 1], cos[:, 1]], axis=-1)
    sin_t = jnp.concatenate([-sin[:, 0], sin[:, 0], -sin[:, 1], sin[:, 1]], axis=-1)
    return cos_t, sin_t


def _rope(x, cos_t, sin_t, first_half):
    swapped = jnp.where(first_half, pltpu.roll(x, HEAD_DIM - 32, axis=1), pltpu.roll(x, 32, axis=1))
    return x * cos_t + swapped * sin_t


def _gqa_prep_kernel(q_ref, k_ref, v_ref, cos_ref, sin_ref, qw_ref, kw_ref, qo_ref, ko_ref, vo_ref):
    cos_t = cos_ref[...]
    sin_t = sin_ref[...]
    lane = lax.broadcasted_iota(jnp.int32, cos_t.shape, 1)
    first_half = (lane % 64) < 32
    scale = HEAD_DIM ** -0.5 * LOG2E
    for h in range(GQA_HEADS):
        cols = pl.ds(h * HEAD_DIM, HEAD_DIM)
        x = _head_rmsnorm(q_ref[:, cols].astype(F32), qw_ref[...])
        qo_ref[:, cols] = (_rope(x, cos_t, sin_t, first_half) * scale).astype(BF16)
    ones = jnp.ones(cos_t.shape, BF16)
    for h in range(GQA_KV_HEADS):
        cols = pl.ds(h * HEAD_DIM, HEAD_DIM)
        x = _head_rmsnorm(k_ref[:, cols].astype(F32), kw_ref[...])
        ko_ref[:, cols] = _rope(x, cos_t, sin_t, first_half).astype(BF16)
        vo_ref[:, pl.ds(2 * h * HEAD_DIM, HEAD_DIM)] = v_ref[:, cols]
        vo_ref[:, pl.ds((2 * h + 1) * HEAD_DIM, HEAD_DIM)] = ones


def _gqa_prep(proj, cos_t, sin_t, qw, kw, *, seq, tm):
    m = proj.shape[0]
    tps = seq // tm
    return pl.pallas_call(
        _gqa_prep_kernel,
        grid=(m // tm,),
        in_specs=[
            pl.BlockSpec((tm, GQA_WIDTH), lambda i: (i, OFF_GQ // GQA_WIDTH)),
            pl.BlockSpec((tm, GQA_KV_WIDTH), lambda i: (i, OFF_GK // GQA_KV_WIDTH)),
            pl.BlockSpec((tm, GQA_KV_WIDTH), lambda i: (i, OFF_GV // GQA_KV_WIDTH)),
            pl.BlockSpec((tm, HEAD_DIM), lambda i: (i % tps, 0)),
            pl.BlockSpec((tm, HEAD_DIM), lambda i: (i % tps, 0)),
            pl.BlockSpec((1, HEAD_DIM), lambda i: (0, 0)),
            pl.BlockSpec((1, HEAD_DIM), lambda i: (0, 0)),
        ],
        out_specs=[
            pl.BlockSpec((tm, GQA_WIDTH), lambda i: (i, 0)),
            pl.BlockSpec((tm, GQA_KV_WIDTH), lambda i: (i, 0)),
            pl.BlockSpec((tm, 2 * GQA_KV_WIDTH), lambda i: (i, 0)),
        ],
        out_shape=[
            jax.ShapeDtypeStruct((m, GQA_WIDTH), BF16),
            jax.ShapeDtypeStruct((m, GQA_KV_WIDTH), BF16),
            jax.ShapeDtypeStruct((m, 2 * GQA_KV_WIDTH), BF16),
        ],
        compiler_params=_cp("parallel"),
        name="gqa_prep",
    )(proj, proj, proj, cos_t, sin_t, qw, kw)


def _flash_kernel(q_ref, k_ref, v_ref, o_ref, s_scr, m_scr, acc_scr, *, tq, tk, seq):
    q2 = jnp.concatenate([q_ref[:, :HEAD_DIM], q_ref[:, HEAD_DIM:]], axis=0)
    m_scr[...] = jnp.full(m_scr.shape, NEG, F32)
    acc_scr[...] = jnp.zeros(acc_scr.shape, F32)
    nblk = tk // HEAD_DIM
    n = seq // tk

    def chunk(kc):
        return pl.ds(kc * tk, tk)

    def scores(kc):
        return lax.dot_general(q2, k_ref[chunk(kc), :], (((1,), (1,)), ((), ())), preferred_element_type=F32)

    def softmax_pv(kc, slot):
        blocks = [s_scr[slot, :, j * HEAD_DIM:(j + 1) * HEAD_DIM] for j in range(nblk)]
        part = blocks[0]
        for blk in blocks[1:]:
            part = jnp.maximum(part, blk)
        m_prev = m_scr[...]
        m_new = jnp.maximum(m_prev, jnp.max(part, axis=-1, keepdims=True))
        alpha = jnp.exp2(m_prev - m_new)
        p = jnp.concatenate([jnp.exp2(blk - m_new).astype(BF16) for blk in blocks], axis=1)
        pv = jnp.dot(p, v_ref[chunk(kc), :], preferred_element_type=F32)
        acc_scr[...] = jnp.concatenate([alpha, alpha], axis=1) * acc_scr[...] + pv
        m_scr[...] = m_new

    s_scr[0] = scores(0)

    for kc in range(n):
        if kc + 1 < n:
            s_scr[(kc + 1) % 2] = scores(kc + 1)
        softmax_pv(kc, kc % 2)
    o = acc_scr[:, :HEAD_DIM] / acc_scr[:, HEAD_DIM:]
    o_ref[:, :HEAD_DIM] = o[:tq].astype(o_ref.dtype)
    o_ref[:, HEAD_DIM:] = o[tq:].astype(o_ref.dtype)


def _gqa_attention(qn, kn, vaug, *, bsz, seq, tq, tk):
    m = qn.shape[0]
    nq = seq // tq
    rep = GQA_HEADS // GQA_KV_HEADS
    return pl.pallas_call(
        functools.partial(_flash_kernel, tq=tq, tk=tk, seq=seq),
        grid=(bsz, GQA_KV_HEADS, nq),
        in_specs=[
            pl.BlockSpec((tq, rep * HEAD_DIM), lambda b, g, i: (b * nq + i, g)),
            pl.BlockSpec((seq, HEAD_DIM), lambda b, g, i: (b, g)),
            pl.BlockSpec((seq, 2 * HEAD_DIM), lambda b, g, i: (b, g)),
        ],
        out_specs=pl.BlockSpec((tq, rep * HEAD_DIM), lambda b, g, i: (b * nq + i, g)),
        out_shape=jax.ShapeDtypeStruct((m, GQA_WIDTH), BF16),
        scratch_shapes=[
            pltpu.VMEM((2, rep * tq, tk), F32),
            pltpu.VMEM((rep * tq, HEAD_DIM), F32),
            pltpu.VMEM((rep * tq, 2 * HEAD_DIM), F32),
        ],
        compiler_params=_cp("parallel", "parallel", "parallel"),
        name="gqa_flash",
    )(qn, kn, vaug)


def _with_halo(x, prev, nxt):
    return jnp.concatenate([x, nxt, prev], axis=0)


def _shift_rows(xcat, s, n):
    return pltpu.roll(xcat, s % xcat.shape[0], axis=0)[:n]


def _dwconv(x, prev, nxt, w, b, width):
    n = x.shape[0]
    xcat = _with_halo(x, prev, nxt)
    pad = width // 2
    out = b + x * w[pad:pad + 1, :]
    for i in range(width):
        if i != pad:
            out = out + _shift_rows(xcat, pad - i, n) * w[i:i + 1, :]
    return out


def _halo_specs(tm, width, col_block, m):
    nblk = m // HALO
    per = tm // HALO
    prev = pl.BlockSpec((HALO, width), lambda i, *_: (jnp.maximum(i * per - 1, 0), col_block))
    nxt = pl.BlockSpec((HALO, width), lambda i, *_: (jnp.minimum((i + 1) * per, nblk - 1), col_block))
    return prev, nxt


def _ssd_prep_kernel(x_ref, prev_ref, next_ref, dtr_ref, cw_ref, cb_ref, dtb_ref, xs_ref, bc_ref, dt_ref, *, tps):
    i = pl.program_id(0)
    first = (i % tps) == 0
    last = (i % tps) == tps - 1
    v = dtr_ref[...] + dtb_ref[...]
    dt_ref[...] = jnp.maximum(v, 0.0) + jnp.log(1.0 + jnp.exp(-jnp.abs(v)))
    x = x_ref[...].astype(F32)
    prev = jnp.where(first, 0.0, prev_ref[...].astype(F32))
    nxt = jnp.where(last, 0.0, next_ref[...].astype(F32))
    y = _silu_of_half(_dwconv(x, prev, nxt, 0.5 * cw_ref[...], 0.5 * cb_ref[...], SSD_CONV))
    xs_ref[...] = y[:, :SSD_WIDTH].astype(BF16)
    bc_ref[...] = y[:, SSD_WIDTH:].astype(BF16)


def _ssd_prep(proj, dt_raw, conv_w, conv_b, dt_bias, *, seq, tm):
    m = proj.shape[0]
    cb = OFF_XBC // SSD_XBC
    prev, nxt = _halo_specs(tm, SSD_XBC, cb, m)
    return pl.pallas_call(
        functools.partial(_ssd_prep_kernel, tps=seq // tm),
        grid=(m // tm,),
        in_specs=[
            pl.BlockSpec((tm, SSD_XBC), lambda i: (i, cb)),
            prev,
            nxt,
            pl.BlockSpec((tm, DT_PAD), lambda i: (i, 0)),
            pl.BlockSpec((SSD_CONV, SSD_XBC), lambda i: (0, 0)),
            pl.BlockSpec((1, SSD_XBC), lambda i: (0, 0)),
            pl.BlockSpec((1, DT_PAD), lambda i: (0, 0)),
        ],
        out_specs=[
            pl.BlockSpec((tm, SSD_WIDTH), lambda i: (i, 0)),
            pl.BlockSpec((tm, SSD_BC), lambda i: (i, 0)),
            pl.BlockSpec((tm, DT_PAD), lambda i: (i, 0)),
        ],
        out_shape=[
            jax.ShapeDtypeStruct((m, SSD_WIDTH), BF16),
            jax.ShapeDtypeStruct((m, SSD_BC), BF16),
            jax.ShapeDtypeStruct((m, DT_PAD), F32),
        ],
        compiler_params=_cp("parallel"),
        name="ssd_prep",
    )(proj, proj, proj, dt_raw, conv_w, conv_b, dt_bias)


def _ssd_constants():
    q = SSD_CHUNK
    li = np.arange(q)
    tri = np.stack([li[:, None] >= li[None, :], li[:, None] <= li[None, :]]).astype(np.float32)
    hp = np.arange(SSD_WIDTH) // SSD_HEAD_DIM
    expand = np.zeros((2, 2 * DT_PAD, SSD_WIDTH), np.float32)
    for d in range(2):
        expand[d, d * SSD_HEADS + hp, np.arange(SSD_WIDTH)] = 1.0
        expand[d, DT_PAD + d * SSD_HEADS + hp, np.arange(SSD_WIDTH)] = 1.0
    return jnp.asarray(tri, F32), jnp.asarray(expand, BF16)


def _expand_heads(v, e):
    hi = v.astype(BF16)
    lo = (v - hi.astype(F32)).astype(BF16)
    return jnp.dot(jnp.concatenate([hi, lo], axis=1), e, preferred_element_type=F32)


def _ssd_scan_kernel(xs_f, bc_f, dt_f, xs_b, bc_b, dt_b, alog_ref, tri_ref, exp_ref, yf_ref, yb_ref, h_scr):
    @pl.when(pl.program_id(1) == 0)
    def _():
        h_scr[...] = jnp.zeros(h_scr.shape, F32)

    q = SSD_CHUNK
    gw = SSD_WIDTH // SSD_GROUPS
    a_row = -jnp.exp(alog_ref[...])
    li = lax.broadcasted_iota(jnp.int32, (q, q), 0)
    si = lax.broadcasted_iota(jnp.int32, (q, q), 1)
    lane = lax.broadcasted_iota(jnp.int32, (q, HEAD_DIM), 1)
    left = lane < SSD_HEAD_DIM
    dirs = ((xs_f, bc_f, dt_f, yf_ref, q - 1), (xs_b, bc_b, dt_b, yb_ref, 0))
    for cc, (d, (xs_ref, bc_ref, dt_ref, y_ref, last)) in [(cc, dd) for cc in range(SSD_CPS) for dd in enumerate(dirs)]:
        rows = pl.ds((cc if d == 0 else SSD_CPS - 1 - cc) * q, q)
        keep = (li >= si) if d == 0 else (li <= si)
        e = exp_ref[d]
        dt = dt_ref[rows, :]
        da = dt * a_row
        cum = jnp.dot(tri_ref[d], da, precision=lax.Precision.HIGHEST, preferred_element_type=F32)
        cum_t = cum.T
        tot = cum[last:last + 1, :]
        stack_e = _expand_heads(jnp.concatenate([dt, jnp.exp(cum), jnp.exp(tot - cum)], axis=0), e)
        dt_e, ec_e, w_e = stack_e[:q], stack_e[q:2 * q], stack_e[2 * q:]
        dec_e = ec_e[last:last + 1, :]
        xd = xs_ref[rows, :].astype(F32) * dt_e
        xdb = xd.astype(BF16)
        xw = (xd * w_e).astype(BF16)
        for g in range(SSD_GROUPS):
            gl = slice(g * gw, (g + 1) * gw)
            bg = bc_ref[rows, g * SSD_STATE:(g + 1) * SSD_STATE]
            cg = bc_ref[rows, (SSD_GROUPS + g) * SSD_STATE:(SSD_GROUPS + g + 1) * SSD_STATE]
            cb = lax.dot_general(cg, bg, (((1,), (1,)), ((), ())), preferred_element_type=F32)
            bg_t = bg.astype(F32).T.astype(BF16)
            h_prev = h_scr[d, g]
            y_off = jnp.dot(cg, h_prev.astype(BF16), preferred_element_type=F32) * ec_e[:, gl]
            states_t = jnp.dot(bg_t, xw[:, gl], preferred_element_type=F32)
            h_scr[d, g] = h_prev * dec_e[:, gl] + states_t
            for j in range(gw // HEAD_DIM):
                mats = []
                for hh in range(2):
                    col = d * SSD_HEADS + g * (SSD_HEADS // SSD_GROUPS) + 2 * j + hh
                    seg = cum[:, col:col + 1] - cum_t[col:col + 1, :]
                    lm = jnp.exp(jnp.where(keep, seg, NEG))
                    mats.append((cb * lm).astype(BF16))
                lo = g * gw + j * HEAD_DIM
                slab = xdb[:, lo:lo + HEAD_DIM]
                zero = jnp.zeros_like(slab)
                wbd = jnp.concatenate([jnp.where(left, slab, zero), jnp.where(left, zero, slab)], axis=0)
                y_diag = jnp.dot(jnp.concatenate(mats, axis=1), wbd, preferred_element_type=F32)
                y_ref[rows, lo:lo + HEAD_DIM] = (y_diag + y_off[:, j * HEAD_DIM:(j + 1) * HEAD_DIM]).astype(y_ref.dtype)


def _ssd_scan(xs, bc, dt, a_log, tri, expand, *, bsz, seq):
    m = xs.shape[0]
    nc = seq // (SSD_CPS * SSD_CHUNK)
    blk = SSD_CPS * SSD_CHUNK
    fwd = lambda b, c: (b * nc + c, 0)
    bwd = lambda b, c: (b * nc + nc - 1 - c, 0)
    return pl.pallas_call(
        _ssd_scan_kernel,
        grid=(bsz, nc),
        in_specs=[
            pl.BlockSpec((blk,SSD_WIDTH), fwd),
            pl.BlockSpec((blk,SSD_BC), fwd),
            pl.BlockSpec((blk,DT_PAD), fwd),
            pl.BlockSpec((blk,SSD_WIDTH), bwd),
            pl.BlockSpec((blk,SSD_BC), bwd),
            pl.BlockSpec((blk,DT_PAD), bwd),
            pl.BlockSpec((1, DT_PAD), lambda b, c: (0, 0)),
            pl.BlockSpec((2, SSD_CHUNK, SSD_CHUNK), lambda b, c: (0, 0, 0)),
            pl.BlockSpec((2, 2 * DT_PAD, SSD_WIDTH), lambda b, c: (0, 0, 0)),
        ],
        out_specs=[
            pl.BlockSpec((blk,SSD_WIDTH), fwd),
            pl.BlockSpec((blk,SSD_WIDTH), bwd),
        ],
        out_shape=[
            jax.ShapeDtypeStruct((m, SSD_WIDTH), BF16),
            jax.ShapeDtypeStruct((m, SSD_WIDTH), BF16),
        ],
        scratch_shapes=[pltpu.VMEM((2, SSD_GROUPS, SSD_STATE, SSD_WIDTH // SSD_GROUPS), F32)],
        compiler_params=_cp("parallel", "arbitrary"),
        name="ssd_scan",
    )(xs, bc, dt, xs, bc, dt, a_log, tri, expand)


def _ssd_finish_kernel(yf_ref, yb_ref, xs_ref, z_ref, d_ref, nw_ref, o_ref):
    y = yf_ref[...].astype(F32) + yb_ref[...].astype(F32) + xs_ref[...].astype(F32) * d_ref[...]
    y = y * _silu(z_ref[...].astype(F32))
    gw = SSD_WIDTH // SSD_GROUPS
    for g in range(SSD_GROUPS):
        yg = y[:, g * gw:(g + 1) * gw]
        ms = jnp.mean(yg * yg, axis=-1, keepdims=True)
        o_ref[:, g * gw:(g + 1) * gw] = (yg * lax.rsqrt(ms + EPS) * nw_ref[:, g * gw:(g + 1) * gw]).astype(o_ref.dtype)


def _ssd_finish(yf, yb, xs, proj, d_exp, norm_w, *, tm):
    m = yf.shape[0]
    row = lambda i: (i, 0)
    const = lambda i: (0, 0)
    return pl.pallas_call(
        _ssd_finish_kernel,
        grid=(m // tm,),
        in_specs=[
            pl.BlockSpec((tm, SSD_WIDTH), row),
            pl.BlockSpec((tm, SSD_WIDTH), row),
            pl.BlockSpec((tm, SSD_WIDTH), row),
            pl.BlockSpec((tm, SSD_WIDTH), lambda i: (i, OFF_Z // SSD_WIDTH)),
            pl.BlockSpec((1, SSD_WIDTH), const),
            pl.BlockSpec((1, SSD_WIDTH), const),
        ],
        out_specs=pl.BlockSpec((tm, SSD_WIDTH), row),
        out_shape=jax.ShapeDtypeStruct((m, SSD_WIDTH), BF16),
        compiler_params=_cp("parallel"),
        name="ssd_finish",
    )(yf, yb, xs, proj, d_exp, norm_w)


def _merge_kernel(ona_ref, ossd_ref, ogqa_ref, g_ref, x_ref, wna_ref, wssd_ref, wgqa_ref, wout_ref, nw_ref,
                  o_ref, h_ref, mg_scr, *, tm):
    nc = 512
    branches = ((ona_ref, wna_ref), (ossd_ref, wssd_ref), (ogqa_ref, wgqa_ref))
    for c in range(D_MODEL // nc):
        cols = pl.ds(c * nc, nc)
        acc = None
        for b, (o_b, w_b) in enumerate(branches):
            p = jnp.dot(o_b[...], w_b[:, cols], preferred_element_type=F32)
            gate = _sigmoid(g_ref[:, pl.ds(b * D_MODEL + c * nc, nc)].astype(F32))
            acc = gate * p if acc is None else acc + gate * p
        mg_scr[:, cols] = acc.astype(BF16)
    ssq = jnp.zeros((tm, 1), F32)
    for c in range(D_MODEL // nc):
        cols = pl.ds(c * nc, nc)
        out_c = x_ref[:, cols] + jnp.dot(mg_scr[...], wout_ref[:, cols], preferred_element_type=F32)
        o_ref[:, cols] = out_c
        ssq = ssq + jnp.sum(out_c * out_c, axis=-1, keepdims=True)
    scale = lax.rsqrt(ssq * (1.0 / D_MODEL) + EPS)
    for c in range(D_MODEL // nc):
        cols = pl.ds(c * nc, nc)
        h_ref[:, cols] = (o_ref[:, cols] * scale * nw_ref[:, cols]).astype(BF16)


def _merge(o_na, o_ssd, o_gqa, proj, x, w_na, w_ssd, w_gqa, w_out, norm_w, *, tm):
    m = x.shape[0]
    row = lambda i: (i, 0)
    const = lambda i: (0, 0)
    one = pl.Buffered(1)
    return pl.pallas_call(
        functools.partial(_merge_kernel, tm=tm),
        grid=(m // tm,),
        in_specs=[
            pl.BlockSpec((tm, NA_WIDTH), row),
            pl.BlockSpec((tm, SSD_WIDTH), row),
            pl.BlockSpec((tm, GQA_WIDTH), row),
            pl.BlockSpec((tm, GATES), row),
            pl.BlockSpec((tm, D_MODEL), row),
            pl.BlockSpec((NA_WIDTH, D_MODEL), const, pipeline_mode=one),
            pl.BlockSpec((SSD_WIDTH, D_MODEL), const, pipeline_mode=one),
            pl.BlockSpec((GQA_WIDTH, D_MODEL), const, pipeline_mode=one),
            pl.BlockSpec((D_MODEL, D_MODEL), const, pipeline_mode=one),
            pl.BlockSpec((1, D_MODEL), const),
        ],
        out_specs=[pl.BlockSpec((tm, D_MODEL), row), pl.BlockSpec((tm, D_MODEL), row)],
        out_shape=[jax.ShapeDtypeStruct((m, D_MODEL), F32), jax.ShapeDtypeStruct((m, D_MODEL), BF16)],
        scratch_shapes=[pltpu.VMEM((tm, D_MODEL), BF16)],
        compiler_params=_cp("parallel"),
        name="merge",
    )(o_na, o_ssd, o_gqa, proj, x, w_na, w_ssd, w_gqa, w_out, norm_w)


FFN_CHUNK = 256


def _ffn_down_kernel(*refs, tps, tm, with_norm):
    if with_norm:
        ug_ref, prev_ref, next_ref, uv_ref, cw_ref, cb_ref, wd_ref, x_ref, nw_ref, o_ref, h_ref, hid_scr = refs
    else:
        ug_ref, prev_ref, next_ref, uv_ref, cw_ref, cb_ref, wd_ref, x_ref, o_ref, hid_scr = refs
    i = pl.program_id(0)
    first = (i % tps) == 0
    last = (i % tps) == tps - 1

    def gated(c):
        cols = pl.ds(c * FFN_CHUNK, FFN_CHUNK)
        g = ug_ref[:, cols].astype(F32)
        prev = jnp.where(first, 0.0, prev_ref[:, cols].astype(F32))
        nxt = jnp.where(last, 0.0, next_ref[:, cols].astype(F32))
        half = _dwconv(g, prev, nxt, 0.5 * cw_ref[:, cols], 0.5 * cb_ref[:, cols], 3)
        hid_scr[:, cols] = _silu_of_half(half).astype(BF16) * uv_ref[:, cols]

    nchunk = D_FF // FFN_CHUNK
    gated(0)
    for c in range(nchunk):
        rows = pl.ds(c * FFN_CHUNK, FFN_CHUNK)
        part = jnp.dot(hid_scr[:, rows], wd_ref[rows, :], preferred_element_type=F32)
        if c + 1 < nchunk:
            gated(c + 1)
        if c == 0:
            o_ref[...] = x_ref[...] + part
        else:
            o_ref[...] += part
    if with_norm:
        _rmsnorm_rows(o_ref, nw_ref, h_ref, tm)


def _ffn_down(u, conv_w, conv_b, w_down, x, next_norm_w=None, *, seq, tm):
    m = x.shape[0]
    nblk = m // HALO
    per = tm // HALO
    const = lambda i: (0, 0)
    row = lambda i: (i, 0)
    with_norm = next_norm_w is not None
    in_specs = [
        pl.BlockSpec((tm, D_FF), lambda i: (i, 0)),
        pl.BlockSpec((HALO, D_FF), lambda i: (jnp.maximum(i * per - 1, 0), 0)),
        pl.BlockSpec((HALO, D_FF), lambda i: (jnp.minimum((i + 1) * per, nblk - 1), 0)),
        pl.BlockSpec((tm, D_FF), lambda i: (i, 1)),
        pl.BlockSpec((3, D_FF), const),
        pl.BlockSpec((1, D_FF), const),
        pl.BlockSpec((D_FF, D_MODEL), const, pipeline_mode=pl.Buffered(1)),
        pl.BlockSpec((tm, D_MODEL), row),
    ]
    out_specs = [pl.BlockSpec((tm, D_MODEL), row)]
    out_shape = [jax.ShapeDtypeStruct((m, D_MODEL), F32)]
    args = [u, u, u, u, conv_w, conv_b, w_down, x]
    if with_norm:
        in_specs.append(pl.BlockSpec((1, D_MODEL), const))
        out_specs.append(pl.BlockSpec((tm, D_MODEL), row))
        out_shape.append(jax.ShapeDtypeStruct((m, D_MODEL), BF16))
        args.append(next_norm_w)
    res = pl.pallas_call(
        functools.partial(_ffn_down_kernel, tps=seq // tm, tm=tm, with_norm=with_norm),
        grid=(m // tm,),
        in_specs=in_specs,
        out_specs=out_specs,
        out_shape=out_shape,
        scratch_shapes=[pltpu.VMEM((tm, D_FF), BF16)],
        compiler_params=_cp("parallel"),
        name="ffn_down",
    )(*args)
    return res if with_norm else (res[0], None)


MM_TN = 1024


def _tiles(bsz, seq):
    m = bsz * seq
    return dict(tm_mm=2048 if m % 2048 == 0 else 1024, tm_ew=min(seq, 1024), tm_merge=min(seq, 256),
                tm_down=min(seq, 256), tq=min(seq, 512), tk=min(seq, 1024))


def _prep_layer_params(p, na_bias):
    (norm1_w, w_in, na_q_norm, na_k_norm, _, ssd_conv_w, ssd_conv_b, ssd_dt_bias, ssd_a_log, ssd_d,
     ssd_norm_w, gqa_q_norm, gqa_k_norm, w_branch, w_out, norm2_w, ffn_w_up, ffn_conv_w, ffn_conv_b, ffn_w_down) = p
    sizes = (3 * NA_WIDTH, GQA_WIDTH, GQA_KV_WIDTH, GQA_KV_WIDTH, SSD_WIDTH, SSD_XBC, 2 * SSD_HEADS, GATES)
    offs = np.concatenate([[0], np.cumsum(sizes)])
    piece = lambda k: w_in[:, offs[k]:offs[k + 1]].astype(BF16)
    na, gq, gk, gv, z, xbc, dtw, gates = (piece(k) for k in range(8))
    w_main = jnp.concatenate([gates, xbc, na, z, gq, gk, gv], axis=1)
    w_dt = jnp.pad(dtw, ((0, 0), (0, DT_PAD - 2 * SSD_HEADS)))
    pad_dt = lambda v: jnp.pad(v.reshape(1, 2 * SSD_HEADS), ((0, 0), (0, DT_PAD - 2 * SSD_HEADS)))
    row = lambda v: v.reshape(1, -1)
    wb = w_branch.astype(BF16)
    return dict(
        norm1_w=row(norm1_w), w_main=w_main, w_dt=w_dt,
        na_q_norm=row(na_q_norm), na_k_norm=row(na_k_norm), na_bias=na_bias,
        conv_w=ssd_conv_w, conv_b=row(ssd_conv_b), dt_bias=pad_dt(ssd_dt_bias), a_log=pad_dt(ssd_a_log),
        d_exp=row(jnp.repeat(ssd_d, SSD_HEAD_DIM)), ssd_norm_w=row(ssd_norm_w),
        gqa_q_norm=row(gqa_q_norm), gqa_k_norm=row(gqa_k_norm),
        w_na=wb[:NA_WIDTH], w_ssd=wb[NA_WIDTH:NA_WIDTH + SSD_WIDTH], w_gqa=wb[NA_WIDTH + SSD_WIDTH:],
        w_out=w_out.astype(BF16), norm2_w=row(norm2_w), w_up=ffn_w_up.astype(BF16),
        ffn_conv_w=ffn_conv_w, ffn_conv_b=row(ffn_conv_b), w_down=ffn_w_down.astype(BF16),
    )


def _prep_params(params):
    depth = params[0].shape[0]
    na_bias = _na_bias_tables(params[4])
    return [_prep_layer_params([p[i] for p in params], na_bias[i]) for i in range(depth)]


def _layer(x, h, lp, next_norm_w, consts, *, bsz, seq):
    t = _tiles(bsz, seq)
    cos_t, sin_t, tri, expand = consts
    proj, dt_raw = _matmul(h, lp["w_main"], lp["w_dt"], tm=t["tm_mm"], tn=MM_TN)
    o_na = _na_attention(proj, lp["na_bias"], lp["na_q_norm"], lp["na_k_norm"], bsz=bsz, seq=seq)
    qn, kn, vaug = _gqa_prep(proj, cos_t, sin_t, lp["gqa_q_norm"], lp["gqa_k_norm"], seq=seq, tm=t["tm_ew"])
    o_gqa = _gqa_attention(qn, kn, vaug, bsz=bsz, seq=seq, tq=t["tq"], tk=t["tk"])
    xs, bc, dt = _ssd_prep(proj, dt_raw, lp["conv_w"], lp["conv_b"], lp["dt_bias"], seq=seq, tm=t["tm_ew"])
    yf, yb = _ssd_scan(xs, bc, dt, lp["a_log"], tri, expand, bsz=bsz, seq=seq)
    o_ssd = _ssd_finish(yf, yb, xs, proj, lp["d_exp"], lp["ssd_norm_w"], tm=t["tm_ew"])
    x, h2 = _merge(o_na, o_ssd, o_gqa, proj, x, lp["w_na"], lp["w_ssd"], lp["w_gqa"], lp["w_out"], lp["norm2_w"],
                   tm=t["tm_merge"])
    u = _matmul(h2, lp["w_up"], tm=t["tm_mm"], tn=MM_TN)
    return _ffn_down(u, lp["ffn_conv_w"], lp["ffn_conv_b"], lp["w_down"], x, next_norm_w, seq=seq, tm=t["tm_down"])


def _trunk(x, layer_params):
    bsz, seq, d = x.shape
    assert d == D_MODEL and seq % 1024 == 0 and seq // GRID_W >= NA_KROWS
    consts = _rope_tables(seq) + _ssd_constants()
    x = x.reshape(bsz * seq, d)
    h = _rmsnorm(x, layer_params[0]["norm1_w"], tm=_tiles(bsz, seq)["tm_ew"])
    for li, lp in enumerate(layer_params):
        nxt = layer_params[li + 1]["norm1_w"] if li + 1 < len(layer_params) else None
        x, h = _layer(x, h, lp, nxt, consts, bsz=bsz, seq=seq)
    return x.reshape(bsz, seq, d)


def kernel(x_prompt, x_sample, norm1_w, w_in, na_q_norm, na_k_norm, na_rpb, ssd_conv_w, ssd_conv_b, ssd_dt_bias, ssd_a_log, ssd_d, ssd_norm_w, gqa_q_norm, gqa_k_norm, w_branch, w_out, norm2_w, ffn_w_up, ffn_conv_w, ffn_conv_b, ffn_w_down):
    params = (norm1_w, w_in, na_q_norm, na_k_norm, na_rpb, ssd_conv_w, ssd_conv_b, ssd_dt_bias, ssd_a_log,
              ssd_d, ssd_norm_w, gqa_q_norm, gqa_k_norm, w_branch, w_out, norm2_w, ffn_w_up, ffn_conv_w,
              ffn_conv_b, ffn_w_down)
    layer_params = _prep_params(params)
    return (_trunk(x_prompt, layer_params), _trunk(x_sample, layer_params))
```
